```python
import math
import jax
import jax.numpy as jnp
from jax import lax
import numpy as np

D_MODEL = 1024
BATCH = 16
SEQ = 2048
DEPTH = 4

GRID_W = 64
CTX_LEN = 256
N_MIXERS = 3
ADA_CHUNKS = 6
NORM_EPS = 1e-6
CONV_KERNEL = 31
DA_HEAD_DIM = 64
DA_HEADS = D_MODEL // (2 * DA_HEAD_DIM)
DA_Q_BLOCK = 128
ROPE_THETA = 10000.0
ROPE_FREQS = DA_HEAD_DIM // 4
GDN_DK = 128
GDN_DV = 128
GDN_HEADS = D_MODEL // GDN_DK
GDN_SHORT_CONV = 5
GDN_CHUNK = 64
GDN_QK_W = GDN_HEADS * GDN_DK
GDN_V_W = GDN_HEADS * GDN_DV
GDN_CONV_CH = 2 * GDN_QK_W + GDN_V_W
GDN_IN_W = GDN_CONV_CH + GDN_V_W + 4 * GDN_HEADS
N_EXPERTS = 16
N_GROUPS = 4
GROUP_SIZE = N_EXPERTS // N_GROUPS
TOP_K = 2
D_EXPERT = 512

kernel_name = 'hybrid_conv_diffattn_gdn_moe_dit'


def _n_layers_of_kind(kind):
    return len(range(kind, DEPTH, N_MIXERS))


def _rms_norm(x, g):
    x32 = x.astype(jnp.float32)
    y = x32 * lax.rsqrt(jnp.mean(x32 * x32, axis=-1, keepdims=True) + NORM_EPS)
    return (y * g.astype(jnp.float32)).astype(x.dtype)


def _layer_norm(x, g, b):
    x32 = x.astype(jnp.float32)
    xc = x32 - jnp.mean(x32, axis=-1, keepdims=True)
    y = xc * lax.rsqrt(jnp.mean(xc * xc, axis=-1, keepdims=True) + NORM_EPS)
    return (y * g.astype(jnp.float32) + b.astype(jnp.float32)).astype(x.dtype)


def _l2_normalize(t):
    return t * lax.rsqrt(jnp.sum(t * t, axis=-1, keepdims=True) + NORM_EPS)


def _modulate(h, shift, scale):
    return h * (1 + scale) + shift


def _depthwise_conv(x, w):
    pad = w.shape[0] // 2
    return lax.conv_general_dilated(x, w[:, None, :], window_strides=(1,), padding=[(pad, pad)],
                                    dimension_numbers=('NWC', 'WIO', 'NWC'),
                                    feature_group_count=x.shape[-1])


def _axial_rope_tables(seqlen, dtype):
    rows = seqlen // GRID_W
    row = jnp.repeat(jnp.arange(rows, dtype=jnp.float32), GRID_W)
    col = jnp.tile(jnp.arange(GRID_W, dtype=jnp.float32), rows)
    inv_freq = ROPE_THETA ** (-jnp.arange(ROPE_FREQS, dtype=jnp.float32) / ROPE_FREQS)
    ang = jnp.stack([row[:, None] * inv_freq, col[:, None] * inv_freq], axis=1)
    return jnp.cos(ang).astype(dtype), jnp.sin(ang).astype(dtype)


def _apply_axial_rope(t, cos, sin):
    shp = t.shape
    t = t.reshape(*shp[:-1], 2, 2, ROPE_FREQS)
    t1, t2 = t[..., 0, :], t[..., 1, :]
    cs = cos[None, :, None, None]
    sn = sin[None, :, None, None]
    out = jnp.stack([t1 * cs - t2 * sn, t2 * cs + t1 * sn], axis=-2)
    return out.reshape(shp)


def _conv_module(h, pw1_w, pw1_b, dw_w, dw_b, ln_g, ln_b, pw2_w, pw2_b):
    a = h @ pw1_w + pw1_b
    half = a.shape[-1] // 2
    a = a[..., :half] * jax.nn.sigmoid(a[..., half:])
    a = _depthwise_conv(a, dw_w) + dw_b
    a = jax.nn.silu(_layer_norm(a, ln_g, ln_b))
    return a @ pw2_w + pw2_b


def _diff_attention(h_ctx, h_lat, w_qkv, lam_vecs, subln_g, w_o, lam_init, cos, sin, ctx_out):
    scale = DA_HEAD_DIM ** -0.5

    def project(h):
        bsz, n = h.shape[:2]
        q, k, v = jnp.split(h @ w_qkv, 3, axis=-1)
        q = q.reshape(bsz, n, DA_HEADS, 2, DA_HEAD_DIM)
        k = k.reshape(bsz, n, DA_HEADS, 2, DA_HEAD_DIM)
        v = v.reshape(bsz, n, DA_HEADS, 2 * DA_HEAD_DIM)
        return q, k, v

    lv = lam_vecs.astype(jnp.float32)
    lam = jnp.exp(jnp.sum(lv[0] * lv[1])) - jnp.exp(jnp.sum(lv[2] * lv[3])) + lam_init

    def attend(q, k, v):
        s = jnp.einsum('bqhcd,bkhcd->bhcqk', q, k, preferred_element_type=jnp.float32) * scale
        p = jax.nn.softmax(s, axis=-1)
        a = (p[:, :, 0] - lam * p[:, :, 1]).astype(v.dtype)
        return jnp.einsum('bhqk,bkhe->bqhe', a, v)

    def finish(o):
        bsz, n = o.shape[:2]
        o = _rms_norm(o, subln_g) * (1.0 - lam_init)
        return o.reshape(bsz, n, -1) @ w_o

    q_c, k_c, v_c = project(h_ctx)
    q_l, k_l, v_l = project(h_lat)
    q_l = _apply_axial_rope(q_l, cos, sin)
    k_l = _apply_axial_rope(k_l, cos, sin)
    k_all = jnp.concatenate([k_c, k_l], axis=1)
    v_all = jnp.concatenate([v_c, v_l], axis=1)
    bsz, seqlen = h_lat.shape[:2]
    n_blk = seqlen // DA_Q_BLOCK
    q_blocks = jnp.moveaxis(q_l.reshape(bsz, n_blk, DA_Q_BLOCK, DA_HEADS, 2, DA_HEAD_DIM), 1, 0)
    o_blocks = lax.map(lambda qb: attend(qb, k_all, v_all), q_blocks)
    o_lat = jnp.moveaxis(o_blocks, 0, 1).reshape(bsz, seqlen, DA_HEADS, 2 * DA_HEAD_DIM)
    y_lat = finish(o_lat)
    y_ctx = finish(attend(q_c, k_c, v_c)) if ctx_out else None
    return y_ctx, y_lat


def _gated_delta_chunked(q, k, v, g, beta, s0, with_output):
    bsz, seqlen, nh = q.shape[:3]
    dv = v.shape[-1]
    n = seqlen // GDN_CHUNK

    def to_chunks(t):
        t = t.reshape(bsz, n, GDN_CHUNK, nh, *t.shape[3:])
        return jnp.moveaxis(t, (1, 2), (0, 3))

    q, k, v, g, beta = (to_chunks(t) for t in (q, k, v, g, beta))
    G = jnp.cumsum(g, axis=-1)
    idx = jnp.arange(GDN_CHUNK)
    incl = idx[:, None] >= idx[None, :]
    strict = idx[:, None] > idx[None, :]
    decay = jnp.exp(jnp.where(incl, G[..., :, None] - G[..., None, :], -jnp.inf))
    kk = jnp.einsum('nbhid,nbhjd->nbhij', k, k)
    a_mat = jnp.where(strict, beta[..., :, None] * kk * decay, 0.0) + jnp.eye(GDN_CHUNK, dtype=k.dtype)
    rhs = jnp.concatenate([v * beta[..., None], k * (beta * jnp.exp(G))[..., None]], axis=-1)
    sol = lax.linalg.triangular_solve(a_mat, rhs, left_side=True, lower=True, unit_diagonal=True)
    u, w = sol[..., :dv], sol[..., dv:]
    k_tail = k * jnp.exp(G[..., -1:] - G)[..., None]
    chunk_decay = jnp.exp(G[..., -1])
    xs = (u, w, k_tail, chunk_decay)
    if with_output:
        qk = jnp.einsum('nbhid,nbhjd->nbhij', q, k) * decay
        xs = xs + (q * jnp.exp(G)[..., None], qk)

    def step(s, inp):
        u_c, w_c, kt_c, cd_c = inp[:4]
        v_new = u_c - jnp.einsum('bhcd,bhde->bhce', w_c, s)
        s_next = cd_c[..., None, None] * s + jnp.einsum('bhcd,bhce->bhde', kt_c, v_new)
        if with_output:
            qd_c, qk_c = inp[4:]
            o = jnp.einsum('bhcd,bhde->bhce', qd_c, s) + jnp.einsum('bhij,bhje->bhie', qk_c, v_new)
            return s_next, o
        return s_next, None

    s_final, o = lax.scan(step, s0, xs)
    if with_output:
        o = jnp.moveaxis(o, (0, 3), (1, 2)).reshape(bsz, seqlen, nh, dv)
    return o, s_final


def _gdn_mixer(h_ctx, h_lat, w_in, conv_w, a_log, dt_bias, norm_g, w_o, ctx_out):
    def project(h):
        bsz, n = h.shape[:2]
        zz = h @ w_in
        qkv = jax.nn.silu(_depthwise_conv(zz[..., :GDN_CONV_CH], conv_w)).astype(jnp.float32)
        q = _l2_normalize(qkv[..., :GDN_QK_W].reshape(bsz, n, GDN_HEADS, GDN_DK)) * (GDN_DK ** -0.5)
        k = _l2_normalize(qkv[..., GDN_QK_W:2 * GDN_QK_W].reshape(bsz, n, GDN_HEADS, GDN_DK))
        v = qkv[..., 2 * GDN_QK_W:].reshape(bsz, n, GDN_HEADS, GDN_DV)
        z = zz[..., GDN_CONV_CH:GDN_CONV_CH + GDN_V_W]
        ba = zz[..., GDN_CONV_CH + GDN_V_W:].astype(jnp.float32)
        beta = jax.nn.sigmoid(ba[..., :2 * GDN_HEADS]).reshape(bsz, n, 2, GDN_HEADS)
        g = -jnp.exp(a_log.astype(jnp.float32)) * jax.nn.softplus(
            ba[..., 2 * GDN_HEADS:].reshape(bsz, n, 2, GDN_HEADS) + dt_bias.astype(jnp.float32))
        return q, k, v, z, beta, g

    def finish(o, z):
        bsz, n = o.shape[:2]
        zg = jax.nn.silu(z.reshape(bsz, n, GDN_HEADS, GDN_DV).astype(jnp.float32))
        o = (_rms_norm(o, norm_g) * zg).astype(z.dtype)
        return o.reshape(bsz, n, GDN_V_W) @ w_o

    rev = lambda t: t[:, ::-1]
    q_c, k_c, v_c, z_c, b_c, g_c = project(h_ctx)
    q_l, k_l, v_l, z_l, b_l, g_l = project(h_lat)
    s0 = jnp.zeros((h_lat.shape[0], GDN_HEADS, GDN_DK, GDN_DV), jnp.float32)
    o_cf, s_f = _gated_delta_chunked(q_c, k_c, v_c, g_c[:, :, 0], b_c[:, :, 0], s0, ctx_out)
    o_cb, s_b = _gated_delta_chunked(rev(q_c), rev(k_c), rev(v_c), rev(g_c[:, :, 1]), rev(b_c[:, :, 1]), s0, ctx_out)
    o_lf, _ = _gated_delta_chunked(q_l, k_l, v_l, g_l[:, :, 0], b_l[:, :, 0], s_f, True)
    o_lb, _ = _gated_delta_chunked(rev(q_l), rev(k_l), rev(v_l), rev(g_l[:, :, 1]), rev(b_l[:, :, 1]), s_b, True)
    y_lat = finish(o_lf + rev(o_lb), z_l)
    y_ctx = finish(o_cf + rev(o_cb), z_c) if ctx_out else None
    return y_ctx, y_lat


def _moe(h, router_w, router_b, w_gate, w_up, w_down):
    n_tok = h.shape[0]
    scores = jax.nn.sigmoid((h @ router_w).astype(jnp.float32))
    sel = scores + router_b.astype(jnp.float32)
    group_score = jnp.sum(lax.top_k(sel.reshape(n_tok, N_GROUPS, GROUP_SIZE), TOP_K)[0], axis=-1)
    best_group = jnp.argmax(group_score, axis=-1)
    in_group = (jnp.arange(N_EXPERTS) // GROUP_SIZE)[None, :] == best_group[:, None]
    _, idx = lax.top_k(jnp.where(in_group, sel, -jnp.inf), TOP_K)
    wts = jnp.take_along_axis(scores, idx, axis=-1)
    wts = wts / jnp.sum(wts, axis=-1, keepdims=True)
    gates = jnp.sum(jax.nn.one_hot(idx, N_EXPERTS, dtype=jnp.float32) * wts[..., None], axis=1).astype(h.dtype)
    out = jnp.zeros_like(h)
    for e in range(N_EXPERTS):
        a = jax.nn.silu(h @ w_gate[e]) * (h @ w_up[e])
        out = out + gates[:, e:e + 1] * (a @ w_down[e])
    return out


def setup_inputs(seed: int = 0) -> dict:
    key = jax.random.key(seed)
    keys = iter(jax.random.split(key, 40))

    def nrm(shape, scale):
        return jax.random.normal(next(keys), shape, jnp.float32) * scale

    d = D_MODEL
    n_conv, n_diff, n_gdn = (_n_layers_of_kind(kk) for kk in range(N_MIXERS))
    dt = jnp.exp(jax.random.uniform(next(keys), (n_gdn, 2, GDN_HEADS), jnp.float32,
                                    math.log(1e-3), math.log(1e-1)))
    return {
        'x': nrm((BATCH, SEQ, d), 1.0),
        'c': nrm((BATCH, d), 1.0),
        'ctx': nrm((BATCH, CTX_LEN, d), 1.0),
        'c_ctx': nrm((d,), 1.0),
        'ada_w': nrm((DEPTH, d, ADA_CHUNKS * d), 0.5 * d ** -0.5),
        'ada_b': nrm((DEPTH, ADA_CHUNKS * d), 0.02),
        'norm_mix_g': 1.0 + nrm((DEPTH, d), 0.02),
        'norm_ffn_g': 1.0 + nrm((DEPTH, d), 0.02),
        'final_norm_g': 1.0 + nrm((d,), 0.02),
        'conv_pw1_w': nrm((n_conv, d, 2 * d), d ** -0.5),
        'conv_pw1_b': nrm((n_conv, 2 * d), 0.02),
        'conv_dw_w': nrm((n_conv, CONV_KERNEL, d), CONV_KERNEL ** -0.5),
        'conv_dw_b': nrm((n_conv, d), 0.02),
        'conv_ln_g': 1.0 + nrm((n_conv, d), 0.02),
        'conv_ln_b': nrm((n_conv, d), 0.02),
        'conv_pw2_w': nrm((n_conv, d, d), d ** -0.5),
        'conv_pw2_b': nrm((n_conv, d), 0.02),
        'diff_w_qkv': nrm((n_diff, d, 3 * d), d ** -0.5),
        'diff_lambda': nrm((n_diff, 4, DA_HEAD_DIM), 0.1),
        'diff_subln_g': 1.0 + nrm((n_diff, 2 * DA_HEAD_DIM), 0.02),
        'diff_w_o': nrm((n_diff, d, d), d ** -0.5),
        'gdn_w_in': nrm((n_gdn, d, GDN_IN_W), d ** -0.5),
        'gdn_conv_w': nrm((n_gdn, GDN_SHORT_CONV, GDN_CONV_CH), GDN_SHORT_CONV ** -0.5),
        'gdn_a_log': jnp.log(jax.random.uniform(next(keys), (n_gdn, 2, GDN_HEADS), jnp.float32, 1.0, 16.0)),
        'gdn_dt_bias': dt + jnp.log(-jnp.expm1(-dt)),
        'gdn_norm_g': 1.0 + nrm((n_gdn, GDN_DV), 0.02),
        'gdn_w_o': nrm((n_gdn, GDN_V_W, d), GDN_V_W ** -0.5),
        'router_w': nrm((d, N_EXPERTS), d ** -0.5),
        'router_b': nrm((N_EXPERTS,), 0.01),
        'moe_w_gate': nrm((DEPTH, N_EXPERTS, d, D_EXPERT), d ** -0.5),
        'moe_w_up': nrm((DEPTH, N_EXPERTS, d, D_EXPERT), d ** -0.5),
        'moe_w_down': nrm((DEPTH, N_EXPERTS, D_EXPERT, d), D_EXPERT ** -0.5),
    }


def reference(x, c, ctx, c_ctx, ada_w, ada_b, norm_mix_g, norm_ffn_g, final_norm_g,
              conv_pw1_w, conv_pw1_b, conv_dw_w, conv_dw_b, conv_ln_g, conv_ln_b, conv_pw2_w, conv_pw2_b,
              diff_w_qkv, diff_lambda, diff_subln_g, diff_w_o,
              gdn_w_in, gdn_conv_w, gdn_a_log, gdn_dt_bias, gdn_norm_g, gdn_w_o,
              router_w, router_b, moe_w_gate, moe_w_up, moe_w_down):
    bsz, seqlen, d = x.shape
    n_ctx = ctx.shape[1]
    cos, sin = _axial_rope_tables(seqlen, x.dtype)
    silu_c = jax.nn.silu(c)
    silu_cc = jax.nn.silu(c_ctx)
    lat, cx = x, ctx
    for i in range(DEPTH):
        last = i == DEPTH - 1
        kind, j = i % N_MIXERS, i // N_MIXERS
        m_lat = jnp.split((silu_c @ ada_w[i] + ada_b[i])[:, None, :], ADA_CHUNKS, axis=-1)
        h_lat = _modulate(_rms_norm(lat, norm_mix_g[i]), m_lat[0], m_lat[1])
        need_ctx = (not last) or kind != 0
        if need_ctx:
            m_ctx = jnp.split(silu_cc @ ada_w[i] + ada_b[i], ADA_CHUNKS, axis=-1)
            h_ctx = _modulate(_rms_norm(cx, norm_mix_g[i]), m_ctx[0], m_ctx[1])
        if kind == 0:
            p = (conv_pw1_w[j], conv_pw1_b[j], conv_dw_w[j], conv_dw_b[j],
                 conv_ln_g[j], conv_ln_b[j], conv_pw2_w[j], conv_pw2_b[j])
            y_lat = _conv_module(h_lat, *p)
            y_ctx = None if last else _conv_module(h_ctx, *p)
        elif kind == 1:
            lam_init = 0.8 - 0.6 * math.exp(-0.3 * i)
            y_ctx, y_lat = _diff_attention(h_ctx, h_lat, diff_w_qkv[j], diff_lambda[j], diff_subln_g[j],
                                           diff_w_o[j], lam_init, cos, sin, not last)
        else:
            y_ctx, y_lat = _gdn_mixer(h_ctx, h_lat, gdn_w_in[j], gdn_conv_w[j], gdn_a_log[j], gdn_dt_bias[j],
                                      gdn_norm_g[j], gdn_w_o[j], not last)
        lat = lat + m_lat[2] * y_lat
        h2_lat = _modulate(_rms_norm(lat, norm_ffn_g[i]), m_lat[3], m_lat[4])
        if last:
            y = _moe(h2_lat.reshape(-1, d), router_w, router_b, moe_w_gate[i], moe_w_up[i], moe_w_down[i])
            lat = lat + m_lat[5] * y.reshape(bsz, seqlen, d)
        else:
            cx = cx + m_ctx[2] * y_ctx
            h2_ctx = _modulate(_rms_norm(cx, norm_ffn_g[i]), m_ctx[3], m_ctx[4])
            tokens = jnp.concatenate([h2_ctx.reshape(-1, d), h2_lat.reshape(-1, d)], axis=0)
            y = _moe(tokens, router_w, router_b, moe_w_gate[i], moe_w_up[i], moe_w_down[i])
            cx = cx + m_ctx[5] * y[:bsz * n_ctx].reshape(bsz, n_ctx, d)
            lat = lat + m_lat[5] * y[bsz * n_ctx:].reshape(bsz, seqlen, d)
    return _rms_norm(lat, final_norm_g)
```

```python
import functools
import math

import jax
import jax.numpy as jnp
from jax import lax
from jax.experimental import pallas as pl
from jax.experimental.pallas import tpu as pltpu

F32 = jnp.float32
BF16 = jnp.bfloat16
HIGHEST = lax.Precision.HIGHEST

NORM_EPS = 1e-6
N_MIXERS = 3
GRID_W = 64
ROPE_THETA = 10000.0
DA_HEAD_DIM = 64
ROPE_FREQS = DA_HEAD_DIM // 4
HEAD_W = 128
GDN_CHUNK = 64
GDN_CHUNKS_PER_STEP = 4
N_EXPERTS = 16
N_GROUPS = 4
GROUP_SIZE = N_EXPERTS // N_GROUPS
LANES = 128
CONV_HALO = 16
VMEM_LIMIT = 56 * 1024 * 1024


def _cparams(n_axes):
    return pltpu.CompilerParams(dimension_semantics=("arbitrary",) * n_axes,
                                vmem_limit_bytes=VMEM_LIMIT)


def _dot(a, b, **kw):
    return jnp.dot(a, b, preferred_element_type=F32, **kw)


def _dot_nt(a, b, **kw):
    return lax.dot_general(a, b, (((1,), (1,)), ((), ())), preferred_element_type=F32, **kw)


def _dot_tn(a, b, **kw):
    return lax.dot_general(a, b, (((0,), (0,)), ((), ())), preferred_element_type=F32, **kw)


def _sigmoid(x):
    return 1.0 / (1.0 + jnp.exp(-x))


def _silu(x):
    return x * _sigmoid(x)


def _rms(x, g):
    return x * lax.rsqrt(jnp.mean(x * x, axis=-1, keepdims=True) + NORM_EPS) * g


def _norm_mod(x, g, shift, scale):
    return _rms(x, g) * (1.0 + scale) + shift


def _row_tile(n, want):
    t = min(n, want)
    assert n % t == 0, (n, t)
    return t


def _ada_kernel(c_ref, w_ref, b_ref, o_ref):
    s = _silu(c_ref[...]).astype(BF16)
    o_ref[0] = _dot(s, w_ref[0].astype(BF16)) + b_ref[0]


def _ada_all(cvec, ada_w, ada_b):
    depth, d, n = ada_w.shape
    r = cvec.shape[0]
    tn = d
    return pl.pallas_call(
        _ada_kernel,
        grid=(depth, n // tn),
        in_specs=[pl.BlockSpec((r, d), lambda i, j: (0, 0)),
                  pl.BlockSpec((1, d, tn), lambda i, j: (i, 0, j)),
                  pl.BlockSpec((1, 1, tn), lambda i, j: (i, 0, j))],
        out_specs=pl.BlockSpec((1, r, tn), lambda i, j: (i, 0, j)),
        out_shape=jax.ShapeDtypeStruct((depth, r, n), F32),
        compiler_params=_cparams(2),
        name="adaln",
    )(cvec, ada_w, ada_b.reshape(depth, 1, n))


def _glu_kernel(x_ref, mod_ref, g_ref, w_ref, b_ref, o_ref, *, tn):
    h = _norm_mod(x_ref[0], g_ref[...], mod_ref[0, 0:1, :], mod_ref[0, 1:2, :]).astype(BF16)
    d = o_ref.shape[-1]
    for j in range(0, d, tn):
        a = _dot(h, w_ref[:, j:j + tn]) + b_ref[:, j:j + tn]
        gt = _dot(h, w_ref[:, d + j:d + j + tn]) + b_ref[:, d + j:d + j + tn]
        o_ref[0, :, j:j + tn] = (a * _sigmoid(gt)).astype(o_ref.dtype)


def _rope_lanes(t, cos, sin_signed, first_half):
    swapped = jnp.where(first_half, pltpu.roll(t, LANES - ROPE_FREQS, 1), pltpu.roll(t, ROPE_FREQS, 1))
    return t * cos + swapped * sin_signed


def _qkv_kernel(*refs, tn, rope, qscale):
    if rope:
        x_ref, mod_ref, g_ref, w_ref, cos_ref, sin_ref, o_ref = refs
    else:
        x_ref, mod_ref, g_ref, w_ref, o_ref = refs
    h = _norm_mod(x_ref[0], g_ref[...], mod_ref[0, 0:1, :], mod_ref[0, 1:2, :]).astype(BF16)
    d = x_ref.shape[-1]
    if rope:
        cos = cos_ref[...]
        sin = sin_ref[...]
        lane = lax.broadcasted_iota(jnp.int32, cos.shape, 1)
        first_half = (lane % (2 * ROPE_FREQS)) < ROPE_FREQS
    for j in range(0, 3 * d, tn):
        y = _dot(h, w_ref[:, j:j + tn])
        if j < 2 * d and rope:
            y = jnp.concatenate(
                [_rope_lanes(y[:, s:s + LANES], cos, sin, first_half) for s in range(0, tn, LANES)], axis=1)
        if j < d:
            y = y * qscale
        o_ref[0, :, j:j + tn] = y.astype(o_ref.dtype)


def _gdn_in_kernel(x_ref, mod_ref, g_ref, w_ref, wba_ref, zz_ref, ba_ref, *, tn):
    h = _norm_mod(x_ref[0], g_ref[...], mod_ref[0, 0:1, :], mod_ref[0, 1:2, :]).astype(BF16)
    for j in range(0, zz_ref.shape[-1], tn):
        zz_ref[0, :, j:j + tn] = _dot(h, w_ref[:, j:j + tn]).astype(zz_ref.dtype)
    ba_ref[0] = _dot(h, wba_ref[...])


def _in_proj(kind, x, mod, norm_g, weights, extra=(), *, tm=512, tn=512, rope=False, qscale=1.0):
    g_, s, d = x.shape
    tm = _row_tile(s, tm)
    grid = (g_, s // tm)
    full2 = lambda a: pl.BlockSpec(a.shape, lambda b, t: (0, 0))
    in_specs = [pl.BlockSpec((1, tm, d), lambda b, t: (b, t, 0)),
                pl.BlockSpec((1, 6, d), lambda b, t: (b, 0, 0)),
                full2(norm_g)] + [full2(w) for w in weights]
    args = [x, mod, norm_g] + list(weights)
    if kind == "glu":
        body = functools.partial(_glu_kernel, tn=tn)
        out_shape = jax.ShapeDtypeStruct((g_, s, d), BF16)
        out_specs = pl.BlockSpec((1, tm, d), lambda b, t: (b, t, 0))
    elif kind == "qkv":
        body = functools.partial(_qkv_kernel, tn=tn, rope=rope, qscale=qscale)
        if rope:
            in_specs += [pl.BlockSpec((tm, LANES), lambda b, t: (t, 0))] * 2
            args += list(extra)
        out_shape = jax.ShapeDtypeStruct((g_, s, 3 * d), BF16)
        out_specs = pl.BlockSpec((1, tm, 3 * d), lambda b, t: (b, t, 0))
    else:
        body = functools.partial(_gdn_in_kernel, tn=tn)
        nz = weights[0].shape[1]
        out_shape = (jax.ShapeDtypeStruct((g_, s, nz), BF16), jax.ShapeDtypeStruct((g_, s, LANES), F32))
        out_specs = (pl.BlockSpec((1, tm, nz), lambda b, t: (b, t, 0)),
                     pl.BlockSpec((1, tm, LANES), lambda b, t: (b, t, 0)))
    return pl.pallas_call(body, grid=grid, in_specs=in_specs, out_specs=out_specs, out_shape=out_shape,
                          compiler_params=_cparams(2), name="in_proj_" + kind)(*args)


def _dwconv_kernel(a_ref, prev_ref, next_ref, w_ref, b_ref, lg_ref, lb_ref, o_ref, buf_ref, acc_ref,
                   *, kw, rows, cols):
    t = pl.program_id(1)
    nt = pl.num_programs(1)
    tm, d = a_ref.shape[1], a_ref.shape[2]
    halo = CONV_HALO
    buf_ref[0:halo, :] = jnp.where(t > 0, prev_ref[0].astype(F32), 0.0)
    buf_ref[halo:halo + tm, :] = a_ref[0].astype(F32)
    buf_ref[halo + tm:, :] = jnp.where(t < nt - 1, next_ref[0].astype(F32), 0.0)
    base = halo - kw // 2
    for r0 in range(0, tm, rows):
        for c0 in range(0, d, cols):
            acc = jnp.zeros((rows, cols), F32)
            for k in range(kw):
                acc = acc + w_ref[k:k + 1, c0:c0 + cols] * buf_ref[base + k + r0:base + k + r0 + rows, c0:c0 + cols]
            acc_ref[r0:r0 + rows, c0:c0 + cols] = acc
    y = acc_ref[...] + b_ref[...]
    yc = y - jnp.mean(y, axis=-1, keepdims=True)
    yn = yc * lax.rsqrt(jnp.mean(yc * yc, axis=-1, keepdims=True) + NORM_EPS) * lg_ref[...] + lb_ref[...]
    o_ref[0] = _silu(yn).astype(o_ref.dtype)


def _dwconv_ln_silu(a, dw_w, dw_b, ln_g, ln_b, *, tm=256):
    bsz, s, d = a.shape
    kw = dw_w.shape[0]
    assert kw // 2 < CONV_HALO
    tm = _row_tile(s, tm)
    hb = tm // CONV_HALO
    nhb = s // CONV_HALO
    full2 = lambda v: pl.BlockSpec(v.shape, lambda b, t: (0, 0))
    return pl.pallas_call(
        functools.partial(_dwconv_kernel, kw=kw, rows=64, cols=256),
        grid=(bsz, s // tm),
        in_specs=[pl.BlockSpec((1, tm, d), lambda b, t: (b, t, 0)),
                  pl.BlockSpec((1, CONV_HALO, d), lambda b, t: (b, jnp.maximum(t * hb - 1, 0), 0)),
                  pl.BlockSpec((1, CONV_HALO, d), lambda b, t: (b, jnp.minimum((t + 1) * hb, nhb - 1), 0)),
                  full2(dw_w), full2(dw_b), full2(ln_g), full2(ln_b)],
        out_specs=pl.BlockSpec((1, tm, d), lambda b, t: (b, t, 0)),
        out_shape=jax.ShapeDtypeStruct((bsz, s, d), BF16),
        scratch_shapes=[pltpu.VMEM((tm + 2 * CONV_HALO, d), F32), pltpu.VMEM((tm, d), F32)],
        compiler_params=_cparams(2), name="dwconv_ln_silu",
    )(a, a, a, dw_w, dw_b, ln_g, ln_b)


def _attn_kernel(*refs, n_kv, lam_init):
    lam_ref, sg_ref, q_ref = refs[:3]
    k_refs = refs[3:3 + n_kv]
    v_refs = refs[3 + n_kv:3 + 2 * n_kv]
    o_ref = refs[3 + 2 * n_kv]
    lv = lam_ref[...]
    lam = (jnp.exp(jnp.sum(lv[0:1] * lv[1:2], axis=-1, keepdims=True))
           - jnp.exp(jnp.sum(lv[2:3] * lv[3:4], axis=-1, keepdims=True)) + lam_init)
    q = q_ref[0]
    lane = lax.broadcasted_iota(jnp.int32, q.shape, 1)
    outs = []
    for c in range(2):
        in_map = (lane < DA_HEAD_DIM) if c == 0 else (lane >= DA_HEAD_DIM)
        qc = jnp.where(in_map, q, jnp.zeros_like(q))
        ss = [_dot_nt(qc, k_ref[0]) for k_ref in k_refs]
        m = ss[0].max(axis=-1, keepdims=True)
        for s_ in ss[1:]:
            m = jnp.maximum(m, s_.max(axis=-1, keepdims=True))
        l = None
        acc = None
        for s_, v_ref in zip(ss, v_refs):
            p = jnp.exp2(s_ - m)
            ls = jnp.sum(p, axis=-1, keepdims=True)
            pv = _dot(p.astype(BF16), v_ref[0])
            l = ls if l is None else l + ls
            acc = pv if acc is None else acc + pv
        outs.append(acc / l)
    o = outs[0] - lam * outs[1]
    o = _rms(o, sg_ref[...]) * (1.0 - lam_init)
    o_ref[0] = o.astype(o_ref.dtype)


def _diff_attn_core(q_src, kv_srcs, lam_vecs, subln_g, lam_init, *, tq=512):
    bsz, sq, d3 = q_src.shape
    d = d3 // 3
    nh = d // HEAD_W
    tq = _row_tile(sq, tq)
    n_kv = len(kv_srcs)
    in_specs = [pl.BlockSpec(lam_vecs.shape, lambda b, h, t: (0, 0)),
                pl.BlockSpec(subln_g.shape, lambda b, h, t: (0, 0)),
                pl.BlockSpec((1, tq, HEAD_W), lambda b, h, t: (b, t, h))]
    in_specs += [pl.BlockSpec((1, s.shape[1], HEAD_W), lambda b, h, t: (b, 0, nh + h)) for s in kv_srcs]
    in_specs += [pl.BlockSpec((1, s.shape[1], HEAD_W), lambda b, h, t: (b, 0, 2 * nh + h)) for s in kv_srcs]
    return pl.pallas_call(
        functools.partial(_attn_kernel, n_kv=n_kv, lam_init=lam_init),
        grid=(bsz, nh, sq // tq),
        in_specs=in_specs,
        out_specs=pl.BlockSpec((1, tq, HEAD_W), lambda b, h, t: (b, t, h)),
        out_shape=jax.ShapeDtypeStruct((bsz, sq, d), BF16),
        compiler_params=_cparams(3), name="diff_attn",
    )(lam_vecs, subln_g, q_src, *kv_srcs, *kv_srcs)


def _gdn_prep_kernel(zz_ref, w_ref, o_ref, buf_ref, *, kw, heads, qk_scale):
    s = zz_ref.shape[1]
    pad = 8
    buf_ref[0:pad, :] = jnp.zeros((pad, LANES), F32)
    buf_ref[pad:pad + s, :] = zz_ref[0].astype(F32)
    buf_ref[pad + s:, :] = jnp.zeros((pad, LANES), F32)
    acc = jnp.zeros((s, LANES), F32)
    for k in range(kw):
        off = pad + k - kw // 2
        acc = acc + w_ref[k:k + 1, :] * buf_ref[off:off + s, :]
    y = _silu(acc)
    nrm = lax.rsqrt(jnp.sum(y * y, axis=-1, keepdims=True) + NORM_EPS)
    kind = pl.program_id(1) // heads
    fac = jnp.where(kind == 0, nrm * qk_scale, jnp.where(kind == 1, nrm, jnp.ones_like(nrm)))
    o_ref[0, 0] = (y * fac).astype(o_ref.dtype)


def _gdn_prep(zz, conv_w):
    bsz, s, _ = zz.shape
    kw, cch = conv_w.shape
    nblk = cch // LANES
    heads = nblk // 3
    return pl.pallas_call(
        functools.partial(_gdn_prep_kernel, kw=kw, heads=heads, qk_scale=HEAD_W ** -0.5),
        grid=(bsz, nblk),
        in_specs=[pl.BlockSpec((1, s, LANES), lambda b, j: (b, 0, j)),
                  pl.BlockSpec((kw, LANES), lambda b, j: (0, j))],
        out_specs=pl.BlockSpec((1, 1, s, LANES), lambda b, j: (b, j, 0, 0)),
        out_shape=jax.ShapeDtypeStruct((bsz, nblk, s, LANES), BF16),
        scratch_shapes=[pltpu.VMEM((s + 16, LANES), F32)],
        compiler_params=_cparams(2), name="gdn_prep",
    )(zz, conv_w)


def _gdn_gates_kernel(ba_ref, alog_ref, dtb_ref, o_ref, *, heads):
    s = ba_ref.shape[1]
    c = GDN_CHUNK
    ii = lax.broadcasted_iota(jnp.int32, (c, c), 0)
    jj = lax.broadcasted_iota(jnp.int32, (c, c), 1)
    tri_f = (ii >= jj).astype(F32)
    tri_b = (ii <= jj).astype(F32)
    lane = lax.broadcasted_iota(jnp.int32, (c, LANES), 1)
    for r0 in range(0, s, c):
        x = ba_ref[0, r0:r0 + c, :]
        beta = _sigmoid(x)
        z = x + dtb_ref[...]
        softplus = jnp.maximum(z, 0.0) + jnp.log1p(jnp.exp(-jnp.abs(z)))
        g = -jnp.exp(alog_ref[...]) * softplus
        gf = _dot(tri_f, g, precision=HIGHEST)
        gb = _dot(tri_b, g, precision=HIGHEST)
        gsum = jnp.where(lane >= 3 * heads, gb, gf)
        o_ref[0, r0:r0 + c, :] = jnp.where(lane < 2 * heads, beta, gsum)


def _gdn_gates(ba, a_log, dt_bias, heads):
    bsz, s, _ = ba.shape
    pad = lambda v: jnp.zeros((1, LANES), F32).at[0, 2 * heads:4 * heads].set(v.reshape(-1).astype(F32))
    return pl.pallas_call(
        functools.partial(_gdn_gates_kernel, heads=heads),
        grid=(bsz,),
        in_specs=[pl.BlockSpec((1, s, LANES), lambda b: (b, 0, 0)),
                  pl.BlockSpec((1, LANES), lambda b: (0, 0)),
                  pl.BlockSpec((1, LANES), lambda b: (0, 0))],
        out_specs=pl.BlockSpec((1, s, LANES), lambda b: (b, 0, 0)),
        out_shape=jax.ShapeDtypeStruct((bsz, s, LANES), F32),
        compiler_params=_cparams(1), name="gdn_gates",
    )(ba, pad(a_log), pad(dt_bias))


def _gdn_chunk_kernel(qkv_f, qkv_b, gc_f, gc_b, gr_f, gr_b, s0_ref, of_ref, ob_ref, sout_ref,
                      s_ref, u_ref, wq_ref, kt_ref, qkd_ref, cd_ref, *, heads, nsub):
    t = pl.program_id(1)
    nt = pl.num_programs(1)
    c = GDN_CHUNK

    @pl.when(t == 0)
    def _():
        s_ref[...] = s0_ref[0]

    ii = lax.broadcasted_iota(jnp.int32, (c, c), 0)
    jj = lax.broadcasted_iota(jnp.int32, (c, c), 1)
    eye = (ii == jj).astype(F32)
    dirs = ((qkv_f, gc_f, gr_f, ii >= jj, ii > jj, c - 1), (qkv_b, gc_b, gr_b, ii <= jj, ii < jj, 0))

    def local(ci, carry):
        rows = pl.ds(pl.multiple_of(ci * c, c), c)
        for di, (qkv, gcr, grr, incl, strict, last) in enumerate(dirs):
            gc = gcr[0, rows, :]
            gr = grr[0, ci]
            for h in range(heads):
                ch = di * heads + h
                q = qkv[0, h, rows, :]
                k = qkv[0, heads + h, rows, :]
                v = qkv[0, 2 * heads + h, rows, :]
                beta = gc[:, ch:ch + 1]
                gcol = gc[:, 2 * heads + ch:2 * heads + ch + 1]
                grow = gr[2 * heads + ch:2 * heads + ch + 1, :]
                kk = _dot_nt(k, k)
                qk = _dot_nt(q, k)
                dec = jnp.where(incl, jnp.exp(jnp.where(incl, gcol - grow, 0.0)), 0.0)
                p = -jnp.where(strict, beta * kk * dec, 0.0)
                x = eye + p
                for _ in range(int(math.log2(c)) - 1):
                    p = _dot(p, p, precision=HIGHEST)
                    x = x + _dot(p, x, precision=HIGHEST)
                k32 = k.astype(F32)
                eg = jnp.exp(gcol)
                rhs = jnp.concatenate([v.astype(F32) * beta, k32 * (beta * eg)], axis=1)
                sol = _dot(x, rhs, precision=HIGHEST)
                glast = gcol[last:last + 1, :]
                u_ref[ch, ci] = sol[:, :HEAD_W]
                wq_ref[ch, ci, 0:c, :] = sol[:, HEAD_W:].astype(BF16)
                wq_ref[ch, ci, c:2 * c, :] = (q.astype(F32) * eg).astype(BF16)
                kt_ref[ch, ci] = (k32 * jnp.exp(glast - gcol)).astype(BF16)
                qkd_ref[ch, ci] = (qk * dec).astype(BF16)
                cd_ref[ch, ci] = jnp.broadcast_to(jnp.exp(glast), (1, LANES))
        return carry

    lax.fori_loop(0, nsub, local, 0)

    def step(si, carry):
        for di, o_ref in enumerate((of_ref, ob_ref)):
            ci = si if di == 0 else nsub - 1 - si
            rows = pl.ds(pl.multiple_of(ci * c, c), c)
            for h in range(heads):
                ch = di * heads + h
                s = s_ref[ch]
                ws = _dot(wq_ref[ch, ci], s.astype(BF16))
                vn = (u_ref[ch, ci] - ws[0:c]).astype(BF16)
                o = ws[c:2 * c] + _dot(qkd_ref[ch, ci], vn)
                o_ref[0, h, rows, :] = o.astype(o_ref.dtype)
                s_ref[ch] = cd_ref[ch, ci] * s + _dot_tn(kt_ref[ch, ci], vn)
        return carry

    lax.fori_loop(0, nsub, step, 0)

    @pl.when(t == nt - 1)
    def _():
        sout_ref[0] = s_ref[...]


def _gdn_chunked(qkv_hm, gcol, s0, heads):
    bsz, _, s, _ = qkv_hm.shape
    c = GDN_CHUNK
    nsub = min(GDN_CHUNKS_PER_STEP, s // c)
    tb = nsub * c
    nt = s // tb
    grow = gcol[:, :, :4 * heads].reshape(bsz, s // c, c, 4 * heads).transpose(0, 1, 3, 2)
    fwd = lambda b, t: (b, 0, t, 0)
    bwd = lambda b, t: (b, 0, nt - 1 - t, 0)
    nch = 2 * heads
    o_sds = jax.ShapeDtypeStruct((bsz, heads, s, HEAD_W), BF16)
    return pl.pallas_call(
        functools.partial(_gdn_chunk_kernel, heads=heads, nsub=nsub),
        grid=(bsz, nt),
        in_specs=[pl.BlockSpec((1, 3 * heads, tb, HEAD_W), fwd),
                  pl.BlockSpec((1, 3 * heads, tb, HEAD_W), bwd),
                  pl.BlockSpec((1, tb, LANES), lambda b, t: (b, t, 0)),
                  pl.BlockSpec((1, tb, LANES), lambda b, t: (b, nt - 1 - t, 0)),
                  pl.BlockSpec((1, nsub, 4 * heads, c), lambda b, t: (b, t, 0, 0)),
                  pl.BlockSpec((1, nsub, 4 * heads, c), lambda b, t: (b, nt - 1 - t, 0, 0)),
                  pl.BlockSpec((1, nch, HEAD_W, HEAD_W), lambda b, t: (b, 0, 0, 0))],
        out_specs=(pl.BlockSpec((1, heads, tb, HEAD_W), fwd),
                   pl.BlockSpec((1, heads, tb, HEAD_W), bwd),
                   pl.BlockSpec((1, nch, HEAD_W, HEAD_W), lambda b, t: (b, 0, 0, 0))),
        out_shape=(o_sds, o_sds, jax.ShapeDtypeStruct(s0.shape, F32)),
        scratch_shapes=[pltpu.VMEM((nch, HEAD_W, HEAD_W), F32),
                        pltpu.VMEM((nch, nsub, c, HEAD_W), F32),
                        pltpu.VMEM((nch, nsub, 2 * c, HEAD_W), BF16),
                        pltpu.VMEM((nch, nsub, c, HEAD_W), BF16),
                        pltpu.VMEM((nch, nsub, c, c), BF16),
                        pltpu.VMEM((nch, nsub, 1, LANES), F32)],
        compiler_params=_cparams(2), name="gdn_chunk",
    )(qkv_hm, qkv_hm, gcol, gcol, grow, grow, s0)


def _route_rows(logits_t, rb):
    s = _sigmoid(logits_t)
    sel = s + rb
    rows = [sel[i:i + 1, :] for i in range(N_EXPERTS)]
    srows = [s[i:i + 1, :] for i in range(N_EXPERTS)]
    gscore = []
    for g in range(N_GROUPS):
        a, b, c, d = rows[GROUP_SIZE * g:GROUP_SIZE * (g + 1)]
        hi_ab, lo_ab = jnp.maximum(a, b), jnp.minimum(a, b)
        hi_cd, lo_cd = jnp.maximum(c, d), jnp.minimum(c, d)
        m1 = jnp.maximum(hi_ab, hi_cd)
        m2 = jnp.maximum(jnp.maximum(lo_ab, lo_cd), jnp.minimum(hi_ab, hi_cd))
        gscore.append(m1 + m2)
    best = gscore[0]
    bg = jnp.zeros_like(best)
    for g in range(1, N_GROUPS):
        upd = gscore[g] > best
        bg = jnp.where(upd, float(g), bg)
        best = jnp.where(upd, gscore[g], best)

    def pick(rws, j):
        out = rws[(N_GROUPS - 1) * GROUP_SIZE + j]
        for g in range(N_GROUPS - 2, -1, -1):
            out = jnp.where(bg == float(g), rws[g * GROUP_SIZE + j], out)
        return out

    v = [pick(rows, j) for j in range(GROUP_SIZE)]
    sv = [pick(srows, j) for j in range(GROUP_SIZE)]
    i1, b1, w1 = jnp.zeros_like(best), v[0], sv[0]
    for j in range(1, GROUP_SIZE):
        upd = v[j] > b1
        i1 = jnp.where(upd, float(j), i1)
        b1 = jnp.where(upd, v[j], b1)
        w1 = jnp.where(upd, sv[j], w1)
    i2 = b2 = w2 = None
    for j in range(GROUP_SIZE):
        ok = i1 != float(j)
        if b2 is None:
            i2 = jnp.where(ok, 0.0, 1.0)
            b2 = jnp.where(ok, v[0], v[1])
            w2 = jnp.where(ok, sv[0], sv[1])
            continue
        upd = ok & (v[j] > b2)
        i2 = jnp.where(upd, float(j), i2)
        b2 = jnp.where(upd, v[j], b2)
        w2 = jnp.where(upd, sv[j], w2)
    tot = w1 + w2
    zero = jnp.zeros_like(best)
    return jnp.concatenate([bg * GROUP_SIZE + i1, bg * GROUP_SIZE + i2, w1 / tot, w2 / tot,
                            zero, zero, zero, zero], axis=0)


def _out_proj_kernel(*refs, pro, heads):
    if pro == "gdn":
        of_ref, ob_ref, z_ref, gn_ref = refs[:4]
        rest = refs[4:]
        parts = []
        for h in range(heads):
            o = of_ref[0, h].astype(F32) + ob_ref[0, h].astype(F32)
            parts.append(_rms(o, gn_ref[...]))
        a = (jnp.concatenate(parts, axis=1) * _silu(z_ref[0].astype(F32))).astype(BF16)
    else:
        a = refs[0][0]
        rest = refs[1:]
    x_ref, mod_ref, w_ref, b_ref, g_ref, rw_ref, rb_ref, x1_ref, h2_ref, rt_ref = rest
    y = _dot(a, w_ref[...]) + b_ref[...]
    x1 = x_ref[0] + mod_ref[0, 2:3, :] * y
    x1_ref[0] = x1
    h2 = _norm_mod(x1, g_ref[...], mod_ref[0, 3:4, :], mod_ref[0, 4:5, :]).astype(BF16)
    h2_ref[0] = h2
    rt_ref[0] = _route_rows(_dot_nt(rw_ref[...], h2), rb_ref[...])


def _out_proj(pro, acts, x, mod, w, bias, norm_g, router_wt, router_b, *, tm=512):
    g_, s, d = x.shape
    tm = _row_tile(s, tm)
    heads = d // HEAD_W
    full2 = lambda v: pl.BlockSpec(v.shape, lambda b, t: (0, 0))
    row3 = lambda width: pl.BlockSpec((1, tm, width), lambda b, t: (b, t, 0))
    if pro == "gdn":
        hm = pl.BlockSpec((1, heads, tm, HEAD_W), lambda b, t: (b, 0, t, 0))
        zcol = 3
        act_specs = [hm, hm, pl.BlockSpec((1, tm, d), lambda b, t: (b, t, zcol)), full2(acts[3])]
    else:
        act_specs = [row3(d)]
    in_specs = act_specs + [row3(d), pl.BlockSpec((1, 6, d), lambda b, t: (b, 0, 0)),
                            full2(w), full2(bias), full2(norm_g), full2(router_wt), full2(router_b)]
    return pl.pallas_call(
        functools.partial(_out_proj_kernel, pro=pro, heads=heads),
        grid=(g_, s // tm),
        in_specs=in_specs,
        out_specs=(row3(d), row3(d), pl.BlockSpec((1, 8, tm), lambda b, t: (b, 0, t))),
        out_shape=(jax.ShapeDtypeStruct((g_, s, d), F32), jax.ShapeDtypeStruct((g_, s, d), BF16),
                   jax.ShapeDtypeStruct((g_, 8, s), F32)),
        compiler_params=_cparams(2), name="out_proj_" + pro,
    )(*acts, x, mod, w, bias, norm_g, router_wt, router_b)


def _moe_ffn_kernel(te_ref, nv_ref, x_ref, wg_ref, wu_ref, wd_ref, o_ref):
    i = pl.program_id(0)

    @pl.when(i < nv_ref[0])
    def _():
        x = x_ref[...]
        a = (_silu(_dot(x, wg_ref[0])) * _dot(x, wu_ref[0])).astype(BF16)
        o_ref[...] = _dot(a, wd_ref[0]).astype(o_ref.dtype)

    @pl.when(i >= nv_ref[0])
    def _():
        o_ref[...] = jnp.zeros_like(o_ref)


def _moe_ffn(xs, tile_expert, n_valid, wg, wu, wd, *, tm):
    r, d = xs.shape
    f = wg.shape[-1]
    return pl.pallas_call(
        _moe_ffn_kernel,
        grid_spec=pltpu.PrefetchScalarGridSpec(
            num_scalar_prefetch=2, grid=(r // tm,),
            in_specs=[pl.BlockSpec((tm, d), lambda i, te, nv: (i, 0)),
                      pl.BlockSpec((1, d, f), lambda i, te, nv: (te[i], 0, 0)),
                      pl.BlockSpec((1, d, f), lambda i, te, nv: (te[i], 0, 0)),
                      pl.BlockSpec((1, f, d), lambda i, te, nv: (te[i], 0, 0))],
            out_specs=pl.BlockSpec((tm, d), lambda i, te, nv: (i, 0))),
        out_shape=jax.ShapeDtypeStruct((r, d), BF16),
        compiler_params=_cparams(1), name="moe_ffn",
    )(tile_expert, n_valid, xs, wg, wu, wd)


def _combine_kernel(x_ref, y_ref, rt_ref, mod_ref, g_ref, o_ref, *, final):
    d = x_ref.shape[-1]
    rt = rt_ref[0]
    y = rt[:, 2:3] * y_ref[0, :, 0:d].astype(F32) + rt[:, 3:4] * y_ref[0, :, d:2 * d].astype(F32)
    x2 = x_ref[0] + mod_ref[0, 5:6, :] * y
    o_ref[0] = _rms(x2, g_ref[...]) if final else x2


def _combine(x1, y_pair, route_t, mod, final_g, *, final, tm=512):
    g_, s, d = x1.shape
    tm = _row_tile(s, tm)
    return pl.pallas_call(
        functools.partial(_combine_kernel, final=final),
        grid=(g_, s // tm),
        in_specs=[pl.BlockSpec((1, tm, d), lambda b, t: (b, t, 0)),
                  pl.BlockSpec((1, tm, 2 * d), lambda b, t: (b, t, 0)),
                  pl.BlockSpec((1, tm, 8), lambda b, t: (b, t, 0)),
                  pl.BlockSpec((1, 6, d), lambda b, t: (b, 0, 0)),
                  pl.BlockSpec(final_g.shape, lambda b, t: (0, 0))],
        out_specs=pl.BlockSpec((1, tm, d), lambda b, t: (b, t, 0)),
        out_shape=jax.ShapeDtypeStruct((g_, s, d), F32),
        compiler_params=_cparams(2), name="moe_combine",
    )(x1, y_pair, route_t, mod, final_g)


def _moe(h2_list, route_list, wg, wu, wd, *, tm=512):
    d = h2_list[0].shape[-1]
    h2 = jnp.concatenate([h.reshape(-1, d) for h in h2_list], axis=0)
    route_t = [r.transpose(0, 2, 1) for r in route_list]
    rt = jnp.concatenate([r.reshape(-1, 8) for r in route_t], axis=0)
    n = h2.shape[0]
    e_flat = rt[:, 0:2].astype(jnp.int32).reshape(-1)
    onehot = (e_flat[:, None] == jnp.arange(N_EXPERTS, dtype=jnp.int32)[None, :]).astype(jnp.int32)
    rank = jnp.sum((jnp.cumsum(onehot, axis=0) - onehot) * onehot, axis=1)
    counts = jnp.sum(onehot, axis=0)
    padded = ((counts + tm - 1) // tm) * tm
    ends = jnp.cumsum(padded)
    pos = (ends - padded)[e_flat] + rank
    n_tiles = (2 * n) // tm + N_EXPERTS
    src = jnp.zeros((n_tiles * tm,), jnp.int32).at[pos].set(jnp.arange(2 * n, dtype=jnp.int32) // 2)
    n_valid = (ends[-1] // tm).astype(jnp.int32)
    tile_start = jnp.minimum(jnp.arange(n_tiles, dtype=jnp.int32), n_valid - 1) * tm
    tile_expert = jnp.minimum(jnp.searchsorted(ends, tile_start, side="right"), N_EXPERTS - 1).astype(jnp.int32)
    xs = jnp.take(h2, src, axis=0)
    ys = _moe_ffn(xs, tile_expert, n_valid.reshape(1), wg, wu, wd, tm=tm)
    y_pair = jnp.take(ys, pos, axis=0).reshape(n, 2 * d)
    outs, off = [], 0
    for h, r in zip(h2_list, route_t):
        cnt = h.shape[0] * h.shape[1]
        outs.append((y_pair[off:off + cnt].reshape(h.shape[0], h.shape[1], 2 * d), r))
        off += cnt
    return outs


def _rope_tables(seqlen):
    pos = jnp.arange(seqlen, dtype=jnp.int32)
    row = (pos // GRID_W).astype(F32)
    col = (pos % GRID_W).astype(F32)
    inv_freq = ROPE_THETA ** (-jnp.arange(ROPE_FREQS, dtype=F32) / ROPE_FREQS)
    lane = jnp.arange(LANES)
    within = lane % DA_HEAD_DIM
    axis = within // (2 * ROPE_FREQS)
    half = (within % (2 * ROPE_FREQS)) // ROPE_FREQS
    ang = jnp.where(axis[None, :] == 0, row[:, None], col[:, None]) * inv_freq[within % ROPE_FREQS][None, :]
    sign = jnp.where(half == 0, -1.0, 1.0).astype(F32)
    return jnp.cos(ang), jnp.sin(ang) * sign[None, :]


def kernel(x, c, ctx, c_ctx, ada_w, ada_b, norm_mix_g, norm_ffn_g, final_norm_g, conv_pw1_w, conv_pw1_b, conv_dw_w, conv_dw_b, conv_ln_g, conv_ln_b, conv_pw2_w, conv_pw2_b, diff_w_qkv, diff_lambda, diff_subln_g, diff_w_o, gdn_w_in, gdn_conv_w, gdn_a_log, gdn_dt_bias, gdn_norm_g, gdn_w_o, router_w, router_b, moe_w_gate, moe_w_up, moe_w_down):
    bsz, seqlen, d = x.shape
    n_ctx = ctx.shape[1]
    depth = ada_w.shape[0]
    heads = d // HEAD_W
    row = lambda v: v.reshape(1, -1).astype(F32)

    mods = _ada_all(jnp.concatenate([c, c_ctx[None, :]], axis=0), ada_w, ada_b)
    mods = mods.reshape(depth, bsz + 1, 6, d)
    router_wt = router_w.T.astype(BF16)
    router_bc = router_b.reshape(N_EXPERTS, 1).astype(F32)
    cos_t, sin_t = _rope_tables(seqlen)
    zero_bias = jnp.zeros((1, d), F32)

    lat = x
    cx = ctx.reshape(1, bsz * n_ctx, d)
    for i in range(depth):
        last = i == depth - 1
        kind, j = i % N_MIXERS, i // N_MIXERS
        need_ctx = (not last) or kind != 0
        m_lat = mods[i, :bsz]
        m_ctx = mods[i, bsz:]
        gmix = row(norm_mix_g[i])
        streams = [(lat, m_lat)] + ([(cx, m_ctx)] if need_ctx else [])
        acts = []
        if kind == 0:
            w1 = conv_pw1_w[j].astype(BF16)
            for xs, ms in streams:
                a = _in_proj("glu", xs, ms, gmix, [w1, row(conv_pw1_b[j])])
                shp = a.shape
                a = a.reshape(bsz, -1, d)
                a = _dwconv_ln_silu(a, conv_dw_w[j].astype(F32), row(conv_dw_b[j]),
                                    row(conv_ln_g[j]), row(conv_ln_b[j]))
                acts.append(("plain", [a.reshape(shp)]))
            w_out, b_out = conv_pw2_w[j].astype(BF16), row(conv_pw2_b[j])
        elif kind == 1:
            lam_init = 0.8 - 0.6 * math.exp(-0.3 * i)
            wq = diff_w_qkv[j].astype(BF16)
            qscale = DA_HEAD_DIM ** -0.5 * math.log2(math.e)
            qkv_l = _in_proj("qkv", lat, m_lat, gmix, [wq], (cos_t, sin_t), rope=True, qscale=qscale)
            qkv_c = _in_proj("qkv", cx, m_ctx, gmix, [wq], qscale=qscale).reshape(bsz, n_ctx, 3 * d)
            lam_v, sub_g = diff_lambda[j].astype(F32), row(diff_subln_g[j])
            o_l = _diff_attn_core(qkv_l, [qkv_c, qkv_l], lam_v, sub_g, lam_init)
            acts.append(("plain", [o_l]))
            if not last:
                o_c = _diff_attn_core(qkv_c, [qkv_c], lam_v, sub_g, lam_init)
                acts.append(("plain", [o_c.reshape(cx.shape)]))
            w_out, b_out = diff_w_o[j].astype(BF16), zero_bias
        else:
            cch = gdn_conv_w.shape[-1]
            w_in = gdn_w_in[j]
            w_main = w_in[:, :cch + d].astype(BF16)
            w_ba = jnp.zeros((d, LANES), BF16).at[:, :4 * heads].set(w_in[:, cch + d:].astype(BF16))
            conv_w = gdn_conv_w[j].astype(F32)
            zz_l, ba_l = _in_proj("gdn", lat, m_lat, gmix, [w_main, w_ba])
            zz_c, ba_c = _in_proj("gdn", cx, m_ctx, gmix, [w_main, w_ba])
            zz_c = zz_c.reshape(bsz, n_ctx, -1)
            ba_c = ba_c.reshape(bsz, n_ctx, LANES)
            s0 = jnp.zeros((bsz, 2 * heads, HEAD_W, HEAD_W), F32)
            ocf, ocb, s1 = _gdn_chunked(_gdn_prep(zz_c, conv_w),
                                        _gdn_gates(ba_c, gdn_a_log[j], gdn_dt_bias[j], heads), s0, heads)
            olf, olb, _ = _gdn_chunked(_gdn_prep(zz_l, conv_w),
                                       _gdn_gates(ba_l, gdn_a_log[j], gdn_dt_bias[j], heads), s1, heads)
            gn = row(gdn_norm_g[j])
            acts.append(("gdn", [olf, olb, zz_l, gn]))
            if not last:
                acts.append(("gdn", [ocf, ocb, zz_c, gn]))
            w_out, b_out = gdn_w_o[j].astype(BF16), zero_bias

        gffn = row(norm_ffn_g[i])
        x1s, h2s, routes = [], [], []
        for (xs, ms), (pro, a) in zip(streams, acts):
            if pro == "gdn" and xs.shape[0] != a[0].shape[0]:
                xin = xs.reshape(bsz, -1, d)
                msb = jnp.broadcast_to(ms, (bsz, 6, d))
                x1, h2, rt = _out_proj(pro, a, xin, msb, w_out, b_out, gffn, router_wt, router_bc)
                x1, h2 = x1.reshape(xs.shape), h2.reshape(xs.shape)
                rt = rt.transpose(1, 0, 2).reshape(1, 8, -1)
            else:
                x1, h2, rt = _out_proj(pro, a, xs, ms, w_out, b_out, gffn, router_wt, router_bc)
            x1s.append(x1)
            h2s.append(h2)
            routes.append(rt)
        if last:
            x1s, h2s, routes = x1s[:1], h2s[:1], routes[:1]
        pairs = _moe(h2s, routes, moe_w_gate[i].astype(BF16), moe_w_up[i].astype(BF16),
                     moe_w_down[i].astype(BF16))
        fg = row(final_norm_g)
        lat = _combine(x1s[0], pairs[0][0], pairs[0][1], m_lat, fg, final=last)
        if not last:
            cx = _combine(x1s[1], pairs[1][0], pairs[1][1], m_ctx, fg, final=False)
    return lat
```

```python
import functools
import math

import jax
import jax.numpy as jnp
from jax import lax
from jax.experimental import pallas as pl
from jax.experimental.pallas import tpu as pltpu

F32 = jnp.float32
BF16 = jnp.bfloat16
HIGHEST = lax.Precision.HIGHEST

NORM_EPS = 1e-6
N_MIXERS = 3
GRID_W = 64
ROPE_THETA = 10000.0
DA_HEAD_DIM = 64
ROPE_FREQS = DA_HEAD_DIM // 4
HEAD_W = 128
GDN_CHUNK = 64
GDN_CHUNKS_PER_STEP = 4
N_EXPERTS = 16
N_GROUPS = 4
GROUP_SIZE = N_EXPERTS // N_GROUPS
LANES = 128
SUBLANES = 8
CONV_HALO = 16
VMEM_LIMIT = 56 * 1024 * 1024


def _cparams(n_axes):
    return pltpu.CompilerParams(dimension_semantics=("arbitrary",) * n_axes,
                                vmem_limit_bytes=VMEM_LIMIT)


def _dot(a, b, **kw):
    return jnp.dot(a, b, preferred_element_type=F32, **kw)


def _dot_nt(a, b, **kw):
    return lax.dot_general(a, b, (((1,), (1,)), ((), ())), preferred_element_type=F32, **kw)


def _dot_tn(a, b, **kw):
    return lax.dot_general(a, b, (((0,), (0,)), ((), ())), preferred_element_type=F32, **kw)


def _sigmoid(x):
    return 1.0 / (1.0 + jnp.exp(-x))


def _silu(x):
    return x * _sigmoid(x)


def _rms(x, g):
    return x * lax.rsqrt(jnp.mean(x * x, axis=-1, keepdims=True) + NORM_EPS) * g


def _norm_mod(x, g, shift, scale):
    return _rms(x, g) * (1.0 + scale) + shift


def _row_tile(n, want):
    t = min(n, want)
    assert n % t == 0, (n, t)
    return t


def _ada_kernel(c_ref, w_ref, b_ref, o_ref):
    s = _silu(c_ref[...]).astype(BF16)
    o_ref[0] = _dot(s, w_ref[0].astype(BF16)) + b_ref[0]


def _ada_all(cvec, ada_w, ada_b):
    depth, d, n = ada_w.shape
    r = cvec.shape[0]
    tn = d
    return pl.pallas_call(
        _ada_kernel,
        grid=(depth, n // tn),
        in_specs=[pl.BlockSpec((r, d), lambda i, j: (0, 0)),
                  pl.BlockSpec((1, d, tn), lambda i, j: (i, 0, j)),
                  pl.BlockSpec((1, 1, tn), lambda i, j: (i, 0, j))],
        out_specs=pl.BlockSpec((1, r, tn), lambda i, j: (i, 0, j)),
        out_shape=jax.ShapeDtypeStruct((depth, r, n), F32),
        compiler_params=_cparams(2),
        name="adaln",
    )(cvec, ada_w, ada_b.reshape(depth, 1, n))


def _glu_kernel(x_ref, mod_ref, g_ref, w_ref, b_ref, o_ref, *, tn):
    h = _norm_mod(x_ref[0], g_ref[...], mod_ref[0, 0:1, :], mod_ref[0, 1:2, :]).astype(BF16)
    d = o_ref.shape[-1]
    for j in range(0, d, tn):
        a = _dot(h, w_ref[:, j:j + tn]) + b_ref[:, j:j + tn]
        gt = _dot(h, w_ref[:, d + j:d + j + tn]) + b_ref[:, d + j:d + j + tn]
        o_ref[0, :, j:j + tn] = (a * _sigmoid(gt)).astype(o_ref.dtype)


def _rope_lanes(t, cos, sin_signed, first_half):
    swapped = jnp.where(first_half, pltpu.roll(t, LANES - ROPE_FREQS, 1), pltpu.roll(t, ROPE_FREQS, 1))
    return t * cos + swapped * sin_signed


def _qkv_kernel(*refs, tn, rope, qscale):
    if rope:
        x_ref, mod_ref, g_ref, w_ref, cos_ref, sin_ref, o_ref = refs
    else:
        x_ref, mod_ref, g_ref, w_ref, o_ref = refs
    h = _norm_mod(x_ref[0], g_ref[...], mod_ref[0, 0:1, :], mod_ref[0, 1:2, :]).astype(BF16)
    d = x_ref.shape[-1]
    if rope:
        cos = cos_ref[...]
        sin = sin_ref[...]
        lane = lax.broadcasted_iota(jnp.int32, cos.shape, 1)
        first_half = (lane % (2 * ROPE_FREQS)) < ROPE_FREQS
    for j in range(0, 3 * d, tn):
        y = _dot(h, w_ref[:, j:j + tn])
        if j < 2 * d and rope:
            y = jnp.concatenate(
                [_rope_lanes(y[:, s:s + LANES], cos, sin, first_half) for s in range(0, tn, LANES)], axis=1)
        if j < d:
            y = y * qscale
        o_ref[0, :, j:j + tn] = y.astype(o_ref.dtype)


def _gdn_in_kernel(x_ref, mod_ref, g_ref, w_ref, wba_ref, zz_ref, ba_ref, *, tn):
    h = _norm_mod(x_ref[0], g_ref[...], mod_ref[0, 0:1, :], mod_ref[0, 1:2, :]).astype(BF16)
    for j in range(0, zz_ref.shape[-1], tn):
        zz_ref[0, :, j:j + tn] = _dot(h, w_ref[:, j:j + tn]).astype(zz_ref.dtype)
    ba_ref[0] = _dot(h, wba_ref[...])


def _in_proj(kind, x, mod, norm_g, weights, extra=(), *, tm=512, tn=512, rope=False, qscale=1.0):
    g_, s, d = x.shape
    tm = _row_tile(s, tm)
    grid = (g_, s // tm)
    full2 = lambda a: pl.BlockSpec(a.shape, lambda b, t: (0, 0))
    in_specs = [pl.BlockSpec((1, tm, d), lambda b, t: (b, t, 0)),
                pl.BlockSpec((1, 6, d), lambda b, t: (b, 0, 0)),
                full2(norm_g)] + [full2(w) for w in weights]
    args = [x, mod, norm_g] + list(weights)
    if kind == "glu":
        body = functools.partial(_glu_kernel, tn=tn)
        out_shape = jax.ShapeDtypeStruct((g_, s, d), BF16)
        out_specs = pl.BlockSpec((1, tm, d), lambda b, t: (b, t, 0))
    elif kind == "qkv":
        body = functools.partial(_qkv_kernel, tn=tn, rope=rope, qscale=qscale)
        if rope:
            in_specs += [pl.BlockSpec((tm, LANES), lambda b, t: (t, 0))] * 2
            args += list(extra)
        out_shape = jax.ShapeDtypeStruct((g_, s, 3 * d), BF16)
        out_specs = pl.BlockSpec((1, tm, 3 * d), lambda b, t: (b, t, 0))
    else:
        body = functools.partial(_gdn_in_kernel, tn=tn)
        nz = weights[0].shape[1]
        out_shape = (jax.ShapeDtypeStruct((g_, s, nz), BF16), jax.ShapeDtypeStruct((g_, s, LANES), F32))
        out_specs = (pl.BlockSpec((1, tm, nz), lambda b, t: (b, t, 0)),
                     pl.BlockSpec((1, tm, LANES), lambda b, t: (b, t, 0)))
    return pl.pallas_call(body, grid=grid, in_specs=in_specs, out_specs=out_specs, out_shape=out_shape,
                          compiler_params=_cparams(2), name="in_proj_" + kind)(*args)


def _dwconv_kernel(a_ref, prev_ref, next_ref, w_ref, b_ref, lg_ref, lb_ref, o_ref, buf_ref, acc_ref,
                   *, kw, rows, cols):
    t = pl.program_id(1)
    nt = pl.num_programs(1)
    tm, d = a_ref.shape[1], a_ref.shape[2]
    halo = CONV_HALO
    span = tm + 2 * halo
    buf_ref[0, 0:halo, :] = jnp.where(t > 0, prev_ref[0].astype(F32), 0.0)
    buf_ref[0, halo:halo + tm, :] = a_ref[0].astype(F32)
    buf_ref[0, halo + tm:span, :] = jnp.where(t < nt - 1, next_ref[0].astype(F32), 0.0)
    buf_ref[0, span:, :] = jnp.zeros((SUBLANES, d), F32)
    for s in range(1, SUBLANES):
        buf_ref[s, 0:span, :] = buf_ref[0, s:s + span, :]
    base = halo - kw // 2
    for r0 in range(0, tm, rows):
        for c0 in range(0, d, cols):
            acc = jnp.zeros((rows, cols), F32)
            for k in range(kw):
                off = base + k + r0
                s = off % SUBLANES
                acc = acc + w_ref[k:k + 1, c0:c0 + cols] * buf_ref[s, off - s:off - s + rows, c0:c0 + cols]
            acc_ref[r0:r0 + rows, c0:c0 + cols] = acc
    y = acc_ref[...] + b_ref[...]
    yc = y - jnp.mean(y, axis=-1, keepdims=True)
    yn = yc * lax.rsqrt(jnp.mean(yc * yc, axis=-1, keepdims=True) + NORM_EPS) * lg_ref[...] + lb_ref[...]
    o_ref[0] = _silu(yn).astype(o_ref.dtype)


def _dwconv_ln_silu(a, dw_w, dw_b, ln_g, ln_b, *, tm=256):
    bsz, s, d = a.shape
    kw = dw_w.shape[0]
    assert kw // 2 < CONV_HALO
    tm = _row_tile(s, tm)
    hb = tm // CONV_HALO
    nhb = s // CONV_HALO
    full2 = lambda v: pl.BlockSpec(v.shape, lambda b, t: (0, 0))
    return pl.pallas_call(
        functools.partial(_dwconv_kernel, kw=kw, rows=64, cols=256),
        grid=(bsz, s // tm),
        in_specs=[pl.BlockSpec((1, tm, d), lambda b, t: (b, t, 0)),
                  pl.BlockSpec((1, CONV_HALO, d), lambda b, t: (b, jnp.maximum(t * hb - 1, 0), 0)),
                  pl.BlockSpec((1, CONV_HALO, d), lambda b, t: (b, jnp.minimum((t + 1) * hb, nhb - 1), 0)),
                  full2(dw_w), full2(dw_b), full2(ln_g), full2(ln_b)],
        out_specs=pl.BlockSpec((1, tm, d), lambda b, t: (b, t, 0)),
        out_shape=jax.ShapeDtypeStruct((bsz, s, d), BF16),
        scratch_shapes=[pltpu.VMEM((SUBLANES, tm + 2 * CONV_HALO + SUBLANES, d), F32), pltpu.VMEM((tm, d), F32)],
        compiler_params=_cparams(2), name="dwconv_ln_silu",
    )(a, a, a, dw_w, dw_b, ln_g, ln_b)


def _attn_kernel(*refs, n_kv, lam_init):
    lam_ref, sg_ref, q_ref = refs[:3]
    k_refs = refs[3:3 + n_kv]
    v_refs = refs[3 + n_kv:3 + 2 * n_kv]
    o_ref = refs[3 + 2 * n_kv]
    lv = lam_ref[...]
    lam = (jnp.exp(jnp.sum(lv[0:1] * lv[1:2], axis=-1, keepdims=True))
           - jnp.exp(jnp.sum(lv[2:3] * lv[3:4], axis=-1, keepdims=True)) + lam_init)
    q = q_ref[0]
    lane = lax.broadcasted_iota(jnp.int32, q.shape, 1)
    outs = []
    scores = []
    for c in range(2):
        in_map = (lane < DA_HEAD_DIM) if c == 0 else (lane >= DA_HEAD_DIM)
        qc = jnp.where(in_map, q, jnp.zeros_like(q))
        scores.append([_dot_nt(qc, k_ref[0]) for k_ref in k_refs])
    for ss in scores:
        m = ss[0].max(axis=-1, keepdims=True)
        for s_ in ss[1:]:
            m = jnp.maximum(m, s_.max(axis=-1, keepdims=True))
        l = None
        acc = None
        for s_, v_ref in zip(ss, v_refs):
            p = jnp.exp2(s_ - m)
            ls = jnp.sum(p, axis=-1, keepdims=True)
            pv = _dot(p.astype(BF16), v_ref[0])
            l = ls if l is None else l + ls
            acc = pv if acc is None else acc + pv
        outs.append(acc / l)
    o = outs[0] - lam * outs[1]
    o = _rms(o, sg_ref[...]) * (1.0 - lam_init)
    o_ref[0] = o.astype(o_ref.dtype)


def _diff_attn_core(q_src, kv_srcs, lam_vecs, subln_g, lam_init, *, tq=512):
    bsz, sq, d3 = q_src.shape
    d = d3 // 3
    nh = d // HEAD_W
    tq = _row_tile(sq, tq)
    n_kv = len(kv_srcs)
    in_specs = [pl.BlockSpec(lam_vecs.shape, lambda b, h, t: (0, 0)),
                pl.BlockSpec(subln_g.shape, lambda b, h, t: (0, 0)),
                pl.BlockSpec((1, tq, HEAD_W), lambda b, h, t: (b, t, h))]
    in_specs += [pl.BlockSpec((1, s.shape[1], HEAD_W), lambda b, h, t: (b, 0, nh + h)) for s in kv_srcs]
    in_specs += [pl.BlockSpec((1, s.shape[1], HEAD_W), lambda b, h, t: (b, 0, 2 * nh + h)) for s in kv_srcs]
    return pl.pallas_call(
        functools.partial(_attn_kernel, n_kv=n_kv, lam_init=lam_init),
        grid=(bsz, nh, sq // tq),
        in_specs=in_specs,
        out_specs=pl.BlockSpec((1, tq, HEAD_W), lambda b, h, t: (b, t, h)),
        out_shape=jax.ShapeDtypeStruct((bsz, sq, d), BF16),
        compiler_params=_cparams(3), name="diff_attn",
    )(lam_vecs, subln_g, q_src, *kv_srcs, *kv_srcs)


def _gdn_prep_kernel(zz_ref, w_ref, o_ref, buf_ref, *, kw, heads, qk_scale):
    s = zz_ref.shape[1]
    pad = 8
    buf_ref[0:pad, :] = jnp.zeros((pad, LANES), F32)
    buf_ref[pad:pad + s, :] = zz_ref[0].astype(F32)
    buf_ref[pad + s:, :] = jnp.zeros((pad, LANES), F32)
    acc = jnp.zeros((s, LANES), F32)
    for k in range(kw):
        off = pad + k - kw // 2
        acc = acc + w_ref[k:k + 1, :] * buf_ref[off:off + s, :]
    y = _silu(acc)
    nrm = lax.rsqrt(jnp.sum(y * y, axis=-1, keepdims=True) + NORM_EPS)
    kind = pl.program_id(1) // heads
    fac = jnp.where(kind == 0, nrm * qk_scale, jnp.where(kind == 1, nrm, jnp.ones_like(nrm)))
    o_ref[0, 0] = (y * fac).astype(o_ref.dtype)


def _gdn_prep(zz, conv_w):
    bsz, s, _ = zz.shape
    kw, cch = conv_w.shape
    nblk = cch // LANES
    heads = nblk // 3
    return pl.pallas_call(
        functools.partial(_gdn_prep_kernel, kw=kw, heads=heads, qk_scale=HEAD_W ** -0.5),
        grid=(bsz, nblk),
        in_specs=[pl.BlockSpec((1, s, LANES), lambda b, j: (b, 0, j)),
                  pl.BlockSpec((kw, LANES), lambda b, j: (0, j))],
        out_specs=pl.BlockSpec((1, 1, s, LANES), lambda b, j: (b, j, 0, 0)),
        out_shape=jax.ShapeDtypeStruct((bsz, nblk, s, LANES), BF16),
        scratch_shapes=[pltpu.VMEM((s + 16, LANES), F32)],
        compiler_params=_cparams(2), name="gdn_prep",
    )(zz, conv_w)


def _gdn_gates_kernel(ba_ref, alog_ref, dtb_ref, o_ref, *, heads):
    s = ba_ref.shape[1]
    c = GDN_CHUNK
    ii = lax.broadcasted_iota(jnp.int32, (c, c), 0)
    jj = lax.broadcasted_iota(jnp.int32, (c, c), 1)
    tri_f = (ii >= jj).astype(F32)
    tri_b = (ii <= jj).astype(F32)
    lane = lax.broadcasted_iota(jnp.int32, (c, LANES), 1)
    for r0 in range(0, s, c):
        x = ba_ref[0, r0:r0 + c, :]
        beta = _sigmoid(x)
        z = x + dtb_ref[...]
        softplus = jnp.maximum(z, 0.0) + jnp.log1p(jnp.exp(-jnp.abs(z)))
        g = -jnp.exp(alog_ref[...]) * softplus
        gf = _dot(tri_f, g, precision=HIGHEST)
        gb = _dot(tri_b, g, precision=HIGHEST)
        gsum = jnp.where(lane >= 3 * heads, gb, gf)
        o_ref[0, r0:r0 + c, :] = jnp.where(lane < 2 * heads, beta, gsum)


def _gdn_gates(ba, a_log, dt_bias, heads):
    bsz, s, _ = ba.shape
    pad = lambda v: jnp.zeros((1, LANES), F32).at[0, 2 * heads:4 * heads].set(v.reshape(-1).astype(F32))
    return pl.pallas_call(
        functools.partial(_gdn_gates_kernel, heads=heads),
        grid=(bsz,),
        in_specs=[pl.BlockSpec((1, s, LANES), lambda b: (b, 0, 0)),
                  pl.BlockSpec((1, LANES), lambda b: (0, 0)),
                  pl.BlockSpec((1, LANES), lambda b: (0, 0))],
        out_specs=pl.BlockSpec((1, s, LANES), lambda b: (b, 0, 0)),
        out_shape=jax.ShapeDtypeStruct((bsz, s, LANES), F32),
        compiler_params=_cparams(1), name="gdn_gates",
    )(ba, pad(a_log), pad(dt_bias))


def _inv_dot(a, b):
    return _dot(a.astype(BF16), b.astype(BF16))


def _gdn_chunk_kernel(qkv_f, qkv_b, gc_f, gc_b, gr_f, gr_b, s0_ref, of_ref, ob_ref, sout_ref,
                      s_ref, u_ref, wq_ref, kt_ref, qkd_ref, cd_ref, *, heads, nsub):
    t = pl.program_id(1)
    nt = pl.num_programs(1)
    c = GDN_CHUNK

    @pl.when(t == 0)
    def _():
        s_ref[...] = s0_ref[0]

    ii = lax.broadcasted_iota(jnp.int32, (c, c), 0)
    jj = lax.broadcasted_iota(jnp.int32, (c, c), 1)
    eye = (ii == jj).astype(F32)
    dirs = ((qkv_f, gc_f, gr_f, ii >= jj, ii > jj, c - 1), (qkv_b, gc_b, gr_b, ii <= jj, ii < jj, 0))

    def local(ci, carry):
        rows = pl.ds(pl.multiple_of(ci * c, c), c)
        chains = []
        for di, (qkv, gcr, grr, incl, strict, last) in enumerate(dirs):
            gc = gcr[0, rows, :]
            gr = grr[0, ci]
            for h in range(heads):
                ch = di * heads + h
                chains.append(dict(
                    ch=ch, incl=incl, strict=strict, last=last,
                    q=qkv[0, h, rows, :], k=qkv[0, heads + h, rows, :], v=qkv[0, 2 * heads + h, rows, :],
                    beta=gc[:, ch:ch + 1],
                    gcol=gc[:, 2 * heads + ch:2 * heads + ch + 1],
                    grow=gr[2 * heads + ch:2 * heads + ch + 1, :]))
        kk = [_dot_nt(a["k"], a["k"]) for a in chains]
        qk = [_dot_nt(a["q"], a["k"]) for a in chains]
        dec = [jnp.where(a["incl"], jnp.exp(jnp.where(a["incl"], a["gcol"] - a["grow"], 0.0)), 0.0)
               for a in chains]
        ps = [-jnp.where(a["strict"], a["beta"] * kki * dci, 0.0) for a, kki, dci in zip(chains, kk, dec)]
        xs = [eye + p for p in ps]
        for _ in range(int(math.log2(c)) - 1):
            ps = [_inv_dot(p, p) for p in ps]
            xs = [x + _inv_dot(p, x) for p, x in zip(ps, xs)]
        egs = [jnp.exp(a["gcol"]) for a in chains]
        k32 = [a["k"].astype(F32) for a in chains]
        rhs = [jnp.concatenate([a["v"].astype(F32) * a["beta"], kf * (a["beta"] * eg)], axis=1)
               for a, kf, eg in zip(chains, k32, egs)]
        sols = [_inv_dot(x, r) for x, r in zip(xs, rhs)]
        for a, sol, kf, eg, qki, dci in zip(chains, sols, k32, egs, qk, dec):
            ch = a["ch"]
            glast = a["gcol"][a["last"]:a["last"] + 1, :]
            u_ref[ch, ci] = sol[:, :HEAD_W]
            wq_ref[ch, ci, 0:c, :] = sol[:, HEAD_W:].astype(BF16)
            wq_ref[ch, ci, c:2 * c, :] = (a["q"].astype(F32) * eg).astype(BF16)
            kt_ref[ch, ci] = (kf * jnp.exp(glast - a["gcol"])).astype(BF16)
            qkd_ref[ch, ci] = (qki * dci).astype(BF16)
            cd_ref[ch, ci] = jnp.broadcast_to(jnp.exp(glast), (1, LANES))
        return carry

    lax.fori_loop(0, nsub, local, 0)

    def step(si, carry):
        chains = []
        for di, o_ref in enumerate((of_ref, ob_ref)):
            ci = si if di == 0 else nsub - 1 - si
            rows = pl.ds(pl.multiple_of(ci * c, c), c)
            chains += [(di * heads + h, h, ci, rows, o_ref) for h in range(heads)]
        ss = [s_ref[ch] for ch, _, _, _, _ in chains]
        ws = [_dot(wq_ref[ch, ci], s.astype(BF16))
              for (ch, _, ci, _, _), s in zip(chains, ss)]
        vn = [(u_ref[ch, ci] - w[0:c]).astype(BF16) for (ch, _, ci, _, _), w in zip(chains, ws)]
        outs = [w[c:2 * c] + _dot(qkd_ref[ch, ci], v) for (ch, _, ci, _, _), w, v in zip(chains, ws, vn)]
        sn = [cd_ref[ch, ci] * s + _dot_tn(kt_ref[ch, ci], v) for (ch, _, ci, _, _), s, v in zip(chains, ss, vn)]
        for (ch, h, ci, rows, o_ref), o, s in zip(chains, outs, sn):
            o_ref[0, h, rows, :] = o.astype(o_ref.dtype)
            s_ref[ch] = s
        return carry

    lax.fori_loop(0, nsub, step, 0)

    @pl.when(t == nt - 1)
    def _():
        sout_ref[0] = s_ref[...]


def _gdn_chunked(qkv_hm, gcol, s0, heads):
    bsz, _, s, _ = qkv_hm.shape
    c = GDN_CHUNK
    nsub = min(GDN_CHUNKS_PER_STEP, s // c)
    tb = nsub * c
    nt = s // tb
    grow = gcol[:, :, :4 * heads].reshape(bsz, s // c, c, 4 * heads).transpose(0, 1, 3, 2)
    fwd = lambda b, t: (b, 0, t, 0)
    bwd = lambda b, t: (b, 0, nt - 1 - t, 0)
    nch = 2 * heads
    o_sds = jax.ShapeDtypeStruct((bsz, heads, s, HEAD_W), BF16)
    return pl.pallas_call(
        functools.partial(_gdn_chunk_kernel, heads=heads, nsub=nsub),
        grid=(bsz, nt),
        in_specs=[pl.BlockSpec((1, 3 * heads, tb, HEAD_W), fwd),
                  pl.BlockSpec((1, 3 * heads, tb, HEAD_W), bwd),
                  pl.BlockSpec((1, tb, LANES), lambda b, t: (b, t, 0)),
                  pl.BlockSpec((1, tb, LANES), lambda b, t: (b, nt - 1 - t, 0)),
                  pl.BlockSpec((1, nsub, 4 * heads, c), lambda b, t: (b, t, 0, 0)),
                  pl.BlockSpec((1, nsub, 4 * heads, c), lambda b, t: (b, nt - 1 - t, 0, 0)),
                  pl.BlockSpec((1, nch, HEAD_W, HEAD_W), lambda b, t: (b, 0, 0, 0))],
        out_specs=(pl.BlockSpec((1, heads, tb, HEAD_W), fwd),
                   pl.BlockSpec((1, heads, tb, HEAD_W), bwd),
                   pl.BlockSpec((1, nch, HEAD_W, HEAD_W), lambda b, t: (b, 0, 0, 0))),
        out_shape=(o_sds, o_sds, jax.ShapeDtypeStruct(s0.shape, F32)),
        scratch_shapes=[pltpu.VMEM((nch, HEAD_W, HEAD_W), F32),
                        pltpu.VMEM((nch, nsub, c, HEAD_W), F32),
                        pltpu.VMEM((nch, nsub, 2 * c, HEAD_W), BF16),
                        pltpu.VMEM((nch, nsub, c, HEAD_W), BF16),
                        pltpu.VMEM((nch, nsub, c, c), BF16),
                        pltpu.VMEM((nch, nsub, 1, LANES), F32)],
        compiler_params=_cparams(2), name="gdn_chunk",
    )(qkv_hm, qkv_hm, gcol, gcol, grow, grow, s0)


def _route_rows(logits_t, rb):
    s = _sigmoid(logits_t)
    sel = s + rb
    rows = [sel[i:i + 1, :] for i in range(N_EXPERTS)]
    srows = [s[i:i + 1, :] for i in range(N_EXPERTS)]
    gscore = []
    for g in range(N_GROUPS):
        a, b, c, d = rows[GROUP_SIZE * g:GROUP_SIZE * (g + 1)]
        hi_ab, lo_ab = jnp.maximum(a, b), jnp.minimum(a, b)
        hi_cd, lo_cd = jnp.maximum(c, d), jnp.minimum(c, d)
        m1 = jnp.maximum(hi_ab, hi_cd)
        m2 = jnp.maximum(jnp.maximum(lo_ab, lo_cd), jnp.minimum(hi_ab, hi_cd))
        gscore.append(m1 + m2)
    best = gscore[0]
    bg = jnp.zeros_like(best)
    for g in range(1, N_GROUPS):
        upd = gscore[g] > best
        bg = jnp.where(upd, float(g), bg)
        best = jnp.where(upd, gscore[g], best)

    def pick(rws, j):
        out = rws[(N_GROUPS - 1) * GROUP_SIZE + j]
        for g in range(N_GROUPS - 2, -1, -1):
            out = jnp.where(bg == float(g), rws[g * GROUP_SIZE + j], out)
        return out

    v = [pick(rows, j) for j in range(GROUP_SIZE)]
    sv = [pick(srows, j) for j in range(GROUP_SIZE)]
    i1, b1, w1 = jnp.zeros_like(best), v[0], sv[0]
    for j in range(1, GROUP_SIZE):
        upd = v[j] > b1
        i1 = jnp.where(upd, float(j), i1)
        b1 = jnp.where(upd, v[j], b1)
        w1 = jnp.where(upd, sv[j], w1)
    i2 = b2 = w2 = None
    for j in range(GROUP_SIZE):
        ok = i1 != float(j)
        if b2 is None:
            i2 = jnp.where(ok, 0.0, 1.0)
            b2 = jnp.where(ok, v[0], v[1])
            w2 = jnp.where(ok, sv[0], sv[1])
            continue
        upd = ok & (v[j] > b2)
        i2 = jnp.where(upd, float(j), i2)
        b2 = jnp.where(upd, v[j], b2)
        w2 = jnp.where(upd, sv[j], w2)
    tot = w1 + w2
    return bg * GROUP_SIZE + i1, bg * GROUP_SIZE + i2, w1 / tot, w2 / tot


def _slot_ranks(e1, e2, base):
    tm = e1.shape[1]
    eid = lax.broadcasted_iota(jnp.int32, (N_EXPERTS, tm), 0).astype(F32)
    oh1 = jnp.where(eid == e1, 1.0, 0.0)
    oh2 = jnp.where(eid == e2, 1.0, 0.0)
    before = (lax.broadcasted_iota(jnp.int32, (tm, tm), 0) < lax.broadcasted_iota(jnp.int32, (tm, tm), 1))
    before = jnp.where(before, 1.0, 0.0).astype(BF16)
    pre = _dot(jnp.concatenate([oh1, oh2], axis=0).astype(BF16), before)
    n1 = jnp.sum(oh1, axis=1, keepdims=True)
    n2 = jnp.sum(oh2, axis=1, keepdims=True)
    r1 = jnp.sum(oh1 * (pre[0:N_EXPERTS] + base), axis=0, keepdims=True)
    r2 = jnp.sum(oh2 * (pre[N_EXPERTS:] + (base + n1)), axis=0, keepdims=True)
    return r1, r2, base + (n1 + n2)


def _out_proj_kernel(*refs, pro, heads):
    if pro == "gdn":
        of_ref, ob_ref, z_ref, gn_ref = refs[:4]
        rest = refs[4:]
        parts = []
        for h in range(heads):
            o = of_ref[0, h].astype(F32) + ob_ref[0, h].astype(F32)
            parts.append(_rms(o, gn_ref[...]))
        a = (jnp.concatenate(parts, axis=1) * _silu(z_ref[0].astype(F32))).astype(BF16)
    else:
        a = refs[0][0]
        rest = refs[1:]
    x_ref, mod_ref, w_ref, b_ref, g_ref, rw_ref, rb_ref, cnt0_ref, x1_ref, h2_ref, rt_ref, cnt_ref = rest

    @pl.when((pl.program_id(0) == 0) & (pl.program_id(1) == 0))
    def _():
        cnt_ref[...] = cnt0_ref[...]

    y = _dot(a, w_ref[...]) + b_ref[...]
    x1 = x_ref[0] + mod_ref[0, 2:3, :] * y
    x1_ref[0] = x1
    h2 = _norm_mod(x1, g_ref[...], mod_ref[0, 3:4, :], mod_ref[0, 4:5, :]).astype(BF16)
    h2_ref[0] = h2
    e1, e2, w1, w2 = _route_rows(_dot_nt(rw_ref[...], h2), rb_ref[...])
    r1, r2, cnt = _slot_ranks(e1, e2, cnt_ref[:, 0:1])
    zero = jnp.zeros_like(e1)
    rt_ref[0] = jnp.concatenate([e1, e2, w1, w2, r1, r2, zero, zero], axis=0)
    cnt_ref[...] = jnp.broadcast_to(cnt, cnt_ref.shape)


def _out_proj(pro, acts, x, mod, w, bias, norm_g, router_wt, router_b, cnt0, *, tm=512):
    g_, s, d = x.shape
    tm = _row_tile(s, tm)
    heads = d // HEAD_W
    full2 = lambda v: pl.BlockSpec(v.shape, lambda b, t: (0, 0))
    row3 = lambda width: pl.BlockSpec((1, tm, width), lambda b, t: (b, t, 0))
    if pro == "gdn":
        hm = pl.BlockSpec((1, heads, tm, HEAD_W), lambda b, t: (b, 0, t, 0))
        zcol = 3
        act_specs = [hm, hm, pl.BlockSpec((1, tm, d), lambda b, t: (b, t, zcol)), full2(acts[3])]
    else:
        act_specs = [row3(d)]
    in_specs = act_specs + [row3(d), pl.BlockSpec((1, 6, d), lambda b, t: (b, 0, 0)),
                            full2(w), full2(bias), full2(norm_g), full2(router_wt), full2(router_b), full2(cnt0)]
    return pl.pallas_call(
        functools.partial(_out_proj_kernel, pro=pro, heads=heads),
        grid=(g_, s // tm),
        in_specs=in_specs,
        out_specs=(row3(d), row3(d), pl.BlockSpec((1, 8, tm), lambda b, t: (b, 0, t)), full2(cnt0)),
        out_shape=(jax.ShapeDtypeStruct((g_, s, d), F32), jax.ShapeDtypeStruct((g_, s, d), BF16),
                   jax.ShapeDtypeStruct((g_, 8, s), F32), jax.ShapeDtypeStruct(cnt0.shape, F32)),
        compiler_params=_cparams(2), name="out_proj_" + pro,
    )(*acts, x, mod, w, bias, norm_g, router_wt, router_b, cnt0)


def _moe_ffn_kernel(te_ref, nv_ref, x_ref, wg_ref, wu_ref, wd_ref, o_ref):
    i = pl.program_id(0)

    @pl.when(i < nv_ref[0])
    def _():
        x = x_ref[...]
        a = (_silu(_dot(x, wg_ref[0].astype(BF16))) * _dot(x, wu_ref[0].astype(BF16))).astype(BF16)
        o_ref[...] = _dot(a, wd_ref[0].astype(BF16)).astype(o_ref.dtype)

    @pl.when(i >= nv_ref[0])
    def _():
        o_ref[...] = jnp.zeros_like(o_ref)


def _moe_ffn(xs, tile_expert, n_valid, wg, wu, wd, *, tm):
    r, d = xs.shape
    f = wg.shape[-1]
    return pl.pallas_call(
        _moe_ffn_kernel,
        grid_spec=pltpu.PrefetchScalarGridSpec(
            num_scalar_prefetch=2, grid=(r // tm,),
            in_specs=[pl.BlockSpec((tm, d), lambda i, te, nv: (i, 0)),
                      pl.BlockSpec((1, d, f), lambda i, te, nv: (te[i], 0, 0)),
                      pl.BlockSpec((1, d, f), lambda i, te, nv: (te[i], 0, 0)),
                      pl.BlockSpec((1, f, d), lambda i, te, nv: (te[i], 0, 0))],
            out_specs=pl.BlockSpec((tm, d), lambda i, te, nv: (i, 0))),
        out_shape=jax.ShapeDtypeStruct((r, d), BF16),
        compiler_params=_cparams(1), name="moe_ffn",
    )(tile_expert, n_valid, xs, wg, wu, wd)


def _combine_kernel(x_ref, y0_ref, y1_ref, rt_ref, mod_ref, g_ref, o_ref, *, final):
    rt = rt_ref[0]
    y = rt[:, 2:3] * y0_ref[0].astype(F32) + rt[:, 3:4] * y1_ref[0].astype(F32)
    x2 = x_ref[0] + mod_ref[0, 5:6, :] * y
    o_ref[0] = _rms(x2, g_ref[...]) if final else x2


def _combine(x1, y0, y1, route_t, mod, final_g, *, final, tm=512):
    g_, s, d = x1.shape
    tm = _row_tile(s, tm)
    row3 = pl.BlockSpec((1, tm, d), lambda b, t: (b, t, 0))
    return pl.pallas_call(
        functools.partial(_combine_kernel, final=final),
        grid=(g_, s // tm),
        in_specs=[row3, row3, row3,
                  pl.BlockSpec((1, tm, 8), lambda b, t: (b, t, 0)),
                  pl.BlockSpec((1, 6, d), lambda b, t: (b, 0, 0)),
                  pl.BlockSpec(final_g.shape, lambda b, t: (0, 0))],
        out_specs=row3,
        out_shape=jax.ShapeDtypeStruct((g_, s, d), F32),
        compiler_params=_cparams(2), name="moe_combine",
    )(x1, y0, y1, route_t, mod, final_g)


def _moe(h2_list, route_list, counts, wg, wu, wd, *, tm=512):
    d = h2_list[0].shape[-1]
    h2 = jnp.concatenate([h.reshape(-1, d) for h in h2_list], axis=0)
    n = h2.shape[0]
    cnt = counts[:, 0].astype(jnp.int32)
    padded = ((cnt + tm - 1) // tm) * tm
    ends = jnp.cumsum(padded)
    starts = (ends - padded).astype(F32)
    eids = jnp.arange(N_EXPERTS, dtype=F32)
    pos_list = []
    for r in route_list:
        e, rank = r[:, 0:2, :], r[:, 4:6, :]
        first = jnp.sum(jnp.where(e[..., None] == eids, starts, 0.0), axis=-1)
        pos_list.append((first + rank).astype(jnp.int32))
    pos_all = jnp.concatenate([p.transpose(1, 0, 2).reshape(2, -1) for p in pos_list], axis=1)
    n_tiles = (2 * n) // tm + N_EXPERTS
    tok = jnp.arange(n, dtype=jnp.int32)
    src = jnp.zeros((n_tiles * tm,), jnp.int32).at[pos_all.reshape(-1)].set(jnp.concatenate([tok, tok]))
    n_valid = (ends[-1] // tm).astype(jnp.int32)
    tile_start = jnp.minimum(jnp.arange(n_tiles, dtype=jnp.int32), n_valid - 1) * tm
    tile_expert = jnp.minimum(jnp.searchsorted(ends, tile_start, side="right"), N_EXPERTS - 1).astype(jnp.int32)
    xs = jnp.take(h2, src, axis=0)
    ys = _moe_ffn(xs, tile_expert, n_valid.reshape(1), wg, wu, wd, tm=tm)
    outs = []
    for h, r, p in zip(h2_list, route_list, pos_list):
        y0 = jnp.take(ys, p[:, 0, :].reshape(-1), axis=0).reshape(h.shape)
        y1 = jnp.take(ys, p[:, 1, :].reshape(-1), axis=0).reshape(h.shape)
        outs.append((y0, y1, r.transpose(0, 2, 1)))
    return outs


def _rope_tables(seqlen):
    pos = jnp.arange(seqlen, dtype=jnp.int32)
    row = (pos // GRID_W).astype(F32)
    col = (pos % GRID_W).astype(F32)
    inv_freq = ROPE_THETA ** (-jnp.arange(ROPE_FREQS, dtype=F32) / ROPE_FREQS)
    lane = jnp.arange(LANES)
    within = lane % DA_HEAD_DIM
    axis = within // (2 * ROPE_FREQS)
    half = (within % (2 * ROPE_FREQS)) // ROPE_FREQS
    ang = jnp.where(axis[None, :] == 0, row[:, None], col[:, None]) * inv_freq[within % ROPE_FREQS][None, :]
    sign = jnp.where(half == 0, -1.0, 1.0).astype(F32)
    return jnp.cos(ang), jnp.sin(ang) * sign[None, :]


def kernel(x, c, ctx, c_ctx, ada_w, ada_b, norm_mix_g, norm_ffn_g, final_norm_g, conv_pw1_w, conv_pw1_b, conv_dw_w, conv_dw_b, conv_ln_g, conv_ln_b, conv_pw2_w, conv_pw2_b, diff_w_qkv, diff_lambda, diff_subln_g, diff_w_o, gdn_w_in, gdn_conv_w, gdn_a_log, gdn_dt_bias, gdn_norm_g, gdn_w_o, router_w, router_b, moe_w_gate, moe_w_up, moe_w_down):
    bsz, seqlen, d = x.shape
    n_ctx = ctx.shape[1]
    depth = ada_w.shape[0]
    heads = d // HEAD_W
    row = lambda v: v.reshape(1, -1).astype(F32)

    mods = _ada_all(jnp.concatenate([c, c_ctx[None, :]], axis=0), ada_w, ada_b)
    mods = mods.reshape(depth, bsz + 1, 6, d)
    router_wt = router_w.T.astype(BF16)
    router_bc = router_b.reshape(N_EXPERTS, 1).astype(F32)
    cos_t, sin_t = _rope_tables(seqlen)
    zero_bias = jnp.zeros((1, d), F32)

    lat = x
    cx = ctx.reshape(1, bsz * n_ctx, d)
    for i in range(depth):
        last = i == depth - 1
        kind, j = i % N_MIXERS, i // N_MIXERS
        need_ctx = (not last) or kind != 0
        m_lat = mods[i, :bsz]
        m_ctx = mods[i, bsz:]
        gmix = row(norm_mix_g[i])
        streams = [(lat, m_lat)] + ([(cx, m_ctx)] if need_ctx else [])
        acts = []
        if kind == 0:
            w1 = conv_pw1_w[j].astype(BF16)
            for xs, ms in streams:
                a = _in_proj("glu", xs, ms, gmix, [w1, row(conv_pw1_b[j])])
                shp = a.shape
                a = a.reshape(bsz, -1, d)
                a = _dwconv_ln_silu(a, conv_dw_w[j].astype(F32), row(conv_dw_b[j]),
                                    row(conv_ln_g[j]), row(conv_ln_b[j]))
                acts.append(("plain", [a.reshape(shp)]))
            w_out, b_out = conv_pw2_w[j].astype(BF16), row(conv_pw2_b[j])
        elif kind == 1:
            lam_init = 0.8 - 0.6 * math.exp(-0.3 * i)
            wq = diff_w_qkv[j].astype(BF16)
            qscale = DA_HEAD_DIM ** -0.5 * math.log2(math.e)
            qkv_l = _in_proj("qkv", lat, m_lat, gmix, [wq], (cos_t, sin_t), rope=True, qscale=qscale)
            qkv_c = _in_proj("qkv", cx, m_ctx, gmix, [wq], qscale=qscale).reshape(bsz, n_ctx, 3 * d)
            lam_v, sub_g = diff_lambda[j].astype(F32), row(diff_subln_g[j])
            o_l = _diff_attn_core(qkv_l, [qkv_c, qkv_l], lam_v, sub_g, lam_init)
            acts.append(("plain", [o_l]))
            if not last:
                o_c = _diff_attn_core(qkv_c, [qkv_c], lam_v, sub_g, lam_init)
                acts.append(("plain", [o_c.reshape(cx.shape)]))
            w_out, b_out = diff_w_o[j].astype(BF16), zero_bias
        else:
            cch = gdn_conv_w.shape[-1]
            w_in = gdn_w_in[j]
            w_main = w_in[:, :cch + d].astype(BF16)
            w_ba = jnp.zeros((d, LANES), BF16).at[:, :4 * heads].set(w_in[:, cch + d:].astype(BF16))
            conv_w = gdn_conv_w[j].astype(F32)
            zz_l, ba_l = _in_proj("gdn", lat, m_lat, gmix, [w_main, w_ba])
            zz_c, ba_c = _in_proj("gdn", cx, m_ctx, gmix, [w_main, w_ba])
            zz_c = zz_c.reshape(bsz, n_ctx, -1)
            ba_c = ba_c.reshape(bsz, n_ctx, LANES)
            s0 = jnp.zeros((bsz, 2 * heads, HEAD_W, HEAD_W), F32)
            ocf, ocb, s1 = _gdn_chunked(_gdn_prep(zz_c, conv_w),
                                        _gdn_gates(ba_c, gdn_a_log[j], gdn_dt_bias[j], heads), s0, heads)
            olf, olb, _ = _gdn_chunked(_gdn_prep(zz_l, conv_w),
                                       _gdn_gates(ba_l, gdn_a_log[j], gdn_dt_bias[j], heads), s1, heads)
            gn = row(gdn_norm_g[j])
            acts.append(("gdn", [olf, olb, zz_l, gn]))
            if not last:
                acts.append(("gdn", [ocf, ocb, zz_c, gn]))
            w_out, b_out = gdn_w_o[j].astype(BF16), zero_bias

        gffn = row(norm_ffn_g[i])
        x1s, h2s, routes = [], [], []
        counts = jnp.zeros((N_EXPERTS, LANES), F32)
        for (xs, ms), (pro, a) in zip(streams, acts):
            if pro == "gdn" and xs.shape[0] != a[0].shape[0]:
                xin = xs.reshape(bsz, -1, d)
                msb = jnp.broadcast_to(ms, (bsz, 6, d))
                x1, h2, rt, counts = _out_proj(pro, a, xin, msb, w_out, b_out, gffn, router_wt, router_bc, counts)
                x1, h2 = x1.reshape(xs.shape), h2.reshape(xs.shape)
                rt = rt.transpose(1, 0, 2).reshape(1, 8, -1)
            else:
                x1, h2, rt, counts = _out_proj(pro, a, xs, ms, w_out, b_out, gffn, router_wt, router_bc, counts)
            x1s.append(x1)
            h2s.append(h2)
            routes.append(rt)
        moe_out = _moe(h2s, routes, counts, moe_w_gate[i], moe_w_up[i], moe_w_down[i])
        fg = row(final_norm_g)
        lat = _combine(x1s[0], *moe_out[0], m_lat, fg, final=last)
        if not last:
            cx = _combine(x1s[1], *moe_out[1], m_ctx, fg, final=False)
    return lat
```

```python
import functools
import math

import jax
import jax.numpy as jnp
from jax import lax
from jax.experimental import pallas as pl
from jax.experimental.pallas import tpu as pltpu

F32 = jnp.float32
BF16 = jnp.bfloat16
HIGHEST = lax.Precision.HIGHEST

NORM_EPS = 1e-6
N_MIXERS = 3
GRID_W = 64
ROPE_THETA = 10000.0
DA_HEAD_DIM = 64
ROPE_FREQS = DA_HEAD_DIM // 4
HEAD_W = 128
GDN_CHUNK = 64
GDN_CHUNKS_PER_STEP = 4
BATCH_GROUPS = 2
N_EXPERTS = 16
N_GROUPS = 4
GROUP_SIZE = N_EXPERTS // N_GROUPS
LANES = 128
SUBLANES = 8
CONV_HALO = 16
VMEM_LIMIT = 56 * 1024 * 1024


def _cparams(n_axes):
    return pltpu.CompilerParams(dimension_semantics=("arbitrary",) * n_axes,
                                vmem_limit_bytes=VMEM_LIMIT)


def _dot(a, b, **kw):
    return jnp.dot(a, b, preferred_element_type=F32, **kw)


def _dot_nt(a, b, **kw):
    return lax.dot_general(a, b, (((1,), (1,)), ((), ())), preferred_element_type=F32, **kw)


def _dot_tn(a, b, **kw):
    return lax.dot_general(a, b, (((0,), (0,)), ((), ())), preferred_element_type=F32, **kw)


def _sigmoid(x):
    return 1.0 / (1.0 + jnp.exp(-x))


def _silu(x):
    return x * _sigmoid(x)


def _rms(x, g):
    return x * lax.rsqrt(jnp.mean(x * x, axis=-1, keepdims=True) + NORM_EPS) * g


def _norm_mod(x, g, shift, scale):
    return _rms(x, g) * (1.0 + scale) + shift


def _row_tile(n, want):
    t = min(n, want)
    assert n % t == 0, (n, t)
    return t


def _ada_kernel(c_ref, w_ref, b_ref, o_ref):
    s = _silu(c_ref[...]).astype(BF16)
    o_ref[0] = _dot(s, w_ref[0].astype(BF16)) + b_ref[0]


def _ada_all(cvec, ada_w, ada_b):
    depth, d, n = ada_w.shape
    r = cvec.shape[0]
    tn = d
    return pl.pallas_call(
        _ada_kernel,
        grid=(depth, n // tn),
        in_specs=[pl.BlockSpec((r, d), lambda i, j: (0, 0)),
                  pl.BlockSpec((1, d, tn), lambda i, j: (i, 0, j)),
                  pl.BlockSpec((1, 1, tn), lambda i, j: (i, 0, j))],
        out_specs=pl.BlockSpec((1, r, tn), lambda i, j: (i, 0, j)),
        out_shape=jax.ShapeDtypeStruct((depth, r, n), F32),
        compiler_params=_cparams(2),
        name="adaln",
    )(cvec, ada_w, ada_b.reshape(depth, 1, n))


def _glu_kernel(x_ref, mod_ref, g_ref, w_ref, b_ref, o_ref, *, tn):
    h = _norm_mod(x_ref[0], g_ref[...], mod_ref[0, 0:1, :], mod_ref[0, 1:2, :]).astype(BF16)
    d = o_ref.shape[-1]
    for j in range(0, d, tn):
        a = _dot(h, w_ref[:, j:j + tn]) + b_ref[:, j:j + tn]
        gt = _dot(h, w_ref[:, d + j:d + j + tn]) + b_ref[:, d + j:d + j + tn]
        o_ref[0, :, j:j + tn] = (a * _sigmoid(gt)).astype(o_ref.dtype)


def _rope_lanes(t, cos, sin_signed, first_half):
    swapped = jnp.where(first_half, pltpu.roll(t, LANES - ROPE_FREQS, 1), pltpu.roll(t, ROPE_FREQS, 1))
    return t * cos + swapped * sin_signed


def _qkv_kernel(*refs, tn, rope, qscale):
    if rope:
        x_ref, mod_ref, g_ref, w_ref, cos_ref, sin_ref, o_ref = refs
    else:
        x_ref, mod_ref, g_ref, w_ref, o_ref = refs
    h = _norm_mod(x_ref[0], g_ref[...], mod_ref[0, 0:1, :], mod_ref[0, 1:2, :]).astype(BF16)
    d = x_ref.shape[-1]
    if rope:
        cos = cos_ref[...]
        sin = sin_ref[...]
        lane = lax.broadcasted_iota(jnp.int32, cos.shape, 1)
        first_half = (lane % (2 * ROPE_FREQS)) < ROPE_FREQS
    for j in range(0, 3 * d, tn):
        y = _dot(h, w_ref[:, j:j + tn])
        if j < 2 * d and rope:
            y = jnp.concatenate(
                [_rope_lanes(y[:, s:s + LANES], cos, sin, first_half) for s in range(0, tn, LANES)], axis=1)
        if j < d:
            y = y * qscale
        o_ref[0, :, j:j + tn] = y.astype(o_ref.dtype)


def _gdn_in_kernel(x_ref, mod_ref, g_ref, w_ref, wba_ref, zz_ref, ba_ref, *, tn):
    h = _norm_mod(x_ref[0], g_ref[...], mod_ref[0, 0:1, :], mod_ref[0, 1:2, :]).astype(BF16)
    for j in range(0, zz_ref.shape[-1], tn):
        zz_ref[0, :, j:j + tn] = _dot(h, w_ref[:, j:j + tn]).astype(zz_ref.dtype)
    ba_ref[0] = _dot(h, wba_ref[...])


def _in_proj(kind, x, mod, norm_g, weights, extra=(), *, tm=512, tn=512, rope=False, qscale=1.0):
    g_, s, d = x.shape
    tm = _row_tile(s, tm)
    grid = (g_, s // tm)
    full2 = lambda a: pl.BlockSpec(a.shape, lambda b, t: (0, 0))
    in_specs = [pl.BlockSpec((1, tm, d), lambda b, t: (b, t, 0)),
                pl.BlockSpec((1, 6, d), lambda b, t: (b, 0, 0)),
                full2(norm_g)] + [full2(w) for w in weights]
    args = [x, mod, norm_g] + list(weights)
    if kind == "glu":
        body = functools.partial(_glu_kernel, tn=tn)
        out_shape = jax.ShapeDtypeStruct((g_, s, d), BF16)
        out_specs = pl.BlockSpec((1, tm, d), lambda b, t: (b, t, 0))
    elif kind == "qkv":
        body = functools.partial(_qkv_kernel, tn=tn, rope=rope, qscale=qscale)
        if rope:
            in_specs += [pl.BlockSpec((tm, LANES), lambda b, t: (t, 0))] * 2
            args += list(extra)
        out_shape = jax.ShapeDtypeStruct((g_, s, 3 * d), BF16)
        out_specs = pl.BlockSpec((1, tm, 3 * d), lambda b, t: (b, t, 0))
    else:
        body = functools.partial(_gdn_in_kernel, tn=tn)
        nz = weights[0].shape[1]
        out_shape = (jax.ShapeDtypeStruct((g_, s, nz), BF16), jax.ShapeDtypeStruct((g_, s, LANES), F32))
        out_specs = (pl.BlockSpec((1, tm, nz), lambda b, t: (b, t, 0)),
                     pl.BlockSpec((1, tm, LANES), lambda b, t: (b, t, 0)))
    return pl.pallas_call(body, grid=grid, in_specs=in_specs, out_specs=out_specs, out_shape=out_shape,
                          compiler_params=_cparams(2), name="in_proj_" + kind)(*args)


def _dwconv_kernel(a_ref, prev_ref, next_ref, w_ref, b_ref, lg_ref, lb_ref, o_ref, buf_ref, acc_ref,
                   *, kw, rows, cols):
    t = pl.program_id(1)
    nt = pl.num_programs(1)
    tm, d = a_ref.shape[1], a_ref.shape[2]
    halo = CONV_HALO
    span = tm + 2 * halo
    buf_ref[0, 0:halo, :] = jnp.where(t > 0, prev_ref[0].astype(F32), 0.0)
    buf_ref[0, halo:halo + tm, :] = a_ref[0].astype(F32)
    buf_ref[0, halo + tm:span, :] = jnp.where(t < nt - 1, next_ref[0].astype(F32), 0.0)
    buf_ref[0, span:, :] = jnp.zeros((SUBLANES, d), F32)
    for s in range(1, SUBLANES):
        buf_ref[s, 0:span, :] = buf_ref[0, s:s + span, :]
    base = halo - kw // 2
    for r0 in range(0, tm, rows):
        for c0 in range(0, d, cols):
            acc = jnp.zeros((rows, cols), F32)
            for k in range(kw):
                off = base + k + r0
                s = off % SUBLANES
                acc = acc + w_ref[k:k + 1, c0:c0 + cols] * buf_ref[s, off - s:off - s + rows, c0:c0 + cols]
            acc_ref[r0:r0 + rows, c0:c0 + cols] = acc
    y = acc_ref[...] + b_ref[...]
    yc = y - jnp.mean(y, axis=-1, keepdims=True)
    yn = yc * lax.rsqrt(jnp.mean(yc * yc, axis=-1, keepdims=True) + NORM_EPS) * lg_ref[...] + lb_ref[...]
    o_ref[0] = _silu(yn).astype(o_ref.dtype)


def _dwconv_ln_silu(a, dw_w, dw_b, ln_g, ln_b, *, tm=256):
    bsz, s, d = a.shape
    kw = dw_w.shape[0]
    assert kw // 2 < CONV_HALO
    tm = _row_tile(s, tm)
    hb = tm // CONV_HALO
    nhb = s // CONV_HALO
    full2 = lambda v: pl.BlockSpec(v.shape, lambda b, t: (0, 0))
    return pl.pallas_call(
        functools.partial(_dwconv_kernel, kw=kw, rows=64, cols=256),
        grid=(bsz, s // tm),
        in_specs=[pl.BlockSpec((1, tm, d), lambda b, t: (b, t, 0)),
                  pl.BlockSpec((1, CONV_HALO, d), lambda b, t: (b, jnp.maximum(t * hb - 1, 0), 0)),
                  pl.BlockSpec((1, CONV_HALO, d), lambda b, t: (b, jnp.minimum((t + 1) * hb, nhb - 1), 0)),
                  full2(dw_w), full2(dw_b), full2(ln_g), full2(ln_b)],
        out_specs=pl.BlockSpec((1, tm, d), lambda b, t: (b, t, 0)),
        out_shape=jax.ShapeDtypeStruct((bsz, s, d), BF16),
        scratch_shapes=[pltpu.VMEM((SUBLANES, tm + 2 * CONV_HALO + SUBLANES, d), F32), pltpu.VMEM((tm, d), F32)],
        compiler_params=_cparams(2), name="dwconv_ln_silu",
    )(a, a, a, dw_w, dw_b, ln_g, ln_b)


def _attn_kernel(*refs, n_kv, lam_init, hps):
    lam_ref, sg_ref, q_ref = refs[:3]
    k_refs = refs[3:3 + n_kv]
    v_refs = refs[3 + n_kv:3 + 2 * n_kv]
    o_ref = refs[3 + 2 * n_kv]
    lv = lam_ref[...]
    lam = (jnp.exp(jnp.sum(lv[0:1] * lv[1:2], axis=-1, keepdims=True))
           - jnp.exp(jnp.sum(lv[2:3] * lv[3:4], axis=-1, keepdims=True)) + lam_init)
    lane = lax.broadcasted_iota(jnp.int32, (q_ref.shape[1], HEAD_W), 1)
    scores = []
    for h in range(hps):
        hl = slice(h * HEAD_W, (h + 1) * HEAD_W)
        q = q_ref[0, :, hl]
        for c in range(2):
            in_map = (lane < DA_HEAD_DIM) if c == 0 else (lane >= DA_HEAD_DIM)
            qc = jnp.where(in_map, q, jnp.zeros_like(q))
            scores.append([_dot_nt(qc, k_ref[0, :, hl]) for k_ref in k_refs])
    outs = []
    for i, ss in enumerate(scores):
        hl = slice((i // 2) * HEAD_W, (i // 2 + 1) * HEAD_W)
        m = ss[0].max(axis=-1, keepdims=True)
        for s_ in ss[1:]:
            m = jnp.maximum(m, s_.max(axis=-1, keepdims=True))
        l = None
        acc = None
        for s_, v_ref in zip(ss, v_refs):
            p = jnp.exp2(s_ - m)
            ls = jnp.sum(p, axis=-1, keepdims=True)
            pv = _dot(p.astype(BF16), v_ref[0, :, hl])
            l = ls if l is None else l + ls
            acc = pv if acc is None else acc + pv
        outs.append(acc / l)
    for h in range(hps):
        o = outs[2 * h] - lam * outs[2 * h + 1]
        o = _rms(o, sg_ref[...]) * (1.0 - lam_init)
        o_ref[0, :, h * HEAD_W:(h + 1) * HEAD_W] = o.astype(o_ref.dtype)


def _diff_attn_core(q_src, kv_srcs, lam_vecs, subln_g, lam_init, *, tq=512, hps=2):
    bsz, sq, d3 = q_src.shape
    d = d3 // 3
    nh = d // HEAD_W
    assert nh % hps == 0
    nhb = nh // hps
    wb = hps * HEAD_W
    tq = _row_tile(sq, tq)
    n_kv = len(kv_srcs)
    in_specs = [pl.BlockSpec(lam_vecs.shape, lambda b, h, t: (0, 0)),
                pl.BlockSpec(subln_g.shape, lambda b, h, t: (0, 0)),
                pl.BlockSpec((1, tq, wb), lambda b, h, t: (b, t, h))]
    in_specs += [pl.BlockSpec((1, s.shape[1], wb), lambda b, h, t: (b, 0, nhb + h)) for s in kv_srcs]
    in_specs += [pl.BlockSpec((1, s.shape[1], wb), lambda b, h, t: (b, 0, 2 * nhb + h)) for s in kv_srcs]
    return pl.pallas_call(
        functools.partial(_attn_kernel, n_kv=n_kv, lam_init=lam_init, hps=hps),
        grid=(bsz, nhb, sq // tq),
        in_specs=in_specs,
        out_specs=pl.BlockSpec((1, tq, wb), lambda b, h, t: (b, t, h)),
        out_shape=jax.ShapeDtypeStruct((bsz, sq, d), BF16),
        compiler_params=_cparams(3), name="diff_attn",
    )(lam_vecs, subln_g, q_src, *kv_srcs, *kv_srcs)


def _gdn_prep_kernel(zz_ref, w_ref, o_ref, buf_ref, *, kw, heads, qk_scale):
    s = zz_ref.shape[1]
    width = heads * HEAD_W
    pad = SUBLANES
    rows = min(s, 256)
    buf_ref[0:pad, :] = jnp.zeros((pad, width), F32)
    buf_ref[pad:pad + s, :] = zz_ref[0].astype(F32)
    buf_ref[pad + s:, :] = jnp.zeros((pad, width), F32)
    kind = pl.program_id(1)
    for h in range(heads):
        lanes = slice(h * HEAD_W, (h + 1) * HEAD_W)
        for r0 in range(0, s, rows):
            acc = jnp.zeros((rows, HEAD_W), F32)
            for k in range(kw):
                off = pad + k - kw // 2 + r0
                acc = acc + w_ref[k:k + 1, lanes] * buf_ref[off:off + rows, lanes]
            y = _silu(acc)
            nrm = lax.rsqrt(jnp.sum(y * y, axis=-1, keepdims=True) + NORM_EPS)
            fac = jnp.where(kind == 0, nrm * qk_scale, jnp.where(kind == 1, nrm, jnp.ones_like(nrm)))
            o_ref[0, h, r0:r0 + rows, :] = (y * fac).astype(o_ref.dtype)


def _gdn_prep(zz, conv_w):
    bsz, s, _ = zz.shape
    kw, cch = conv_w.shape
    heads = cch // (3 * HEAD_W)
    width = heads * HEAD_W
    return pl.pallas_call(
        functools.partial(_gdn_prep_kernel, kw=kw, heads=heads, qk_scale=HEAD_W ** -0.5),
        grid=(bsz, 3),
        in_specs=[pl.BlockSpec((1, s, width), lambda b, j: (b, 0, j)),
                  pl.BlockSpec((kw, width), lambda b, j: (0, j))],
        out_specs=pl.BlockSpec((1, heads, s, HEAD_W), lambda b, j: (b, j, 0, 0)),
        out_shape=jax.ShapeDtypeStruct((bsz, 3 * heads, s, HEAD_W), BF16),
        scratch_shapes=[pltpu.VMEM((s + 2 * SUBLANES, width), F32)],
        compiler_params=_cparams(2), name="gdn_prep",
    )(zz, conv_w)


def _gdn_gates_kernel(ba_ref, alog_ref, dtb_ref, o_ref, *, heads):
    s = ba_ref.shape[1]
    c = GDN_CHUNK
    ii = lax.broadcasted_iota(jnp.int32, (c, c), 0)
    jj = lax.broadcasted_iota(jnp.int32, (c, c), 1)
    tri_f = (ii >= jj).astype(F32)
    tri_b = (ii <= jj).astype(F32)
    lane = lax.broadcasted_iota(jnp.int32, (c, LANES), 1)
    for r0 in range(0, s, c):
        x = ba_ref[0, r0:r0 + c, :]
        beta = _sigmoid(x)
        z = x + dtb_ref[...]
        softplus = jnp.maximum(z, 0.0) + jnp.log1p(jnp.exp(-jnp.abs(z)))
        g = -jnp.exp(alog_ref[...]) * softplus
        gf = _dot(tri_f, g, precision=HIGHEST)
        gb = _dot(tri_b, g, precision=HIGHEST)
        gsum = jnp.where(lane >= 3 * heads, gb, gf)
        o_ref[0, r0:r0 + c, :] = jnp.where(lane < 2 * heads, beta, gsum)


def _gdn_gates(ba, a_log, dt_bias, heads):
    bsz, s, _ = ba.shape
    pad = lambda v: jnp.zeros((1, LANES), F32).at[0, 2 * heads:4 * heads].set(v.reshape(-1).astype(F32))
    return pl.pallas_call(
        functools.partial(_gdn_gates_kernel, heads=heads),
        grid=(bsz,),
        in_specs=[pl.BlockSpec((1, s, LANES), lambda b: (b, 0, 0)),
                  pl.BlockSpec((1, LANES), lambda b: (0, 0)),
                  pl.BlockSpec((1, LANES), lambda b: (0, 0))],
        out_specs=pl.BlockSpec((1, s, LANES), lambda b: (b, 0, 0)),
        out_shape=jax.ShapeDtypeStruct((bsz, s, LANES), F32),
        compiler_params=_cparams(1), name="gdn_gates",
    )(ba, pad(a_log), pad(dt_bias))


def _inv_dot(a, b):
    return _dot(a.astype(BF16), b.astype(BF16))


def _gdn_chunk_kernel(qkv_f, qkv_b, gc_f, gc_b, gr_f, gr_b, s0_ref, of_ref, ob_ref, sout_ref,
                      s_ref, u_ref, wq_ref, kt_ref, qkd_ref, cd_ref, *, heads, nsub):
    t = pl.program_id(1)
    nt = pl.num_programs(1)
    c = GDN_CHUNK
    nb = s0_ref.shape[0]
    nch = 2 * heads

    @pl.when(t == 0)
    def _():
        s_ref[...] = s0_ref[...]

    ii = lax.broadcasted_iota(jnp.int32, (c, 2 * c), 0)
    lane = lax.broadcasted_iota(jnp.int32, (c, 2 * c), 1)
    jj = lane % c
    left = lane < c
    eye_right = (ii == jj) & (lane >= c)
    dirs = ((qkv_f, gc_f, gr_f, ii >= jj, ii > jj, c - 1), (qkv_b, gc_b, gr_b, ii <= jj, ii < jj, 0))

    def local(ci, carry):
        rows = pl.ds(pl.multiple_of(ci * c, c), c)
        chains = []
        for bi in range(nb):
            for di, (qkv, gcr, grr, incl, strict, last) in enumerate(dirs):
                gc = gcr[bi, rows, :]
                gr = grr[bi, ci]
                for h in range(heads):
                    ch = di * heads + h
                    chains.append(dict(
                        bi=bi, ch=ch, incl=incl, strict=strict, last=last,
                        q=qkv[bi, h, rows, :], k=qkv[bi, heads + h, rows, :], v=qkv[bi, 2 * heads + h, rows, :],
                        beta=gc[:, ch:ch + 1],
                        gcol=gc[:, 2 * heads + ch:2 * heads + ch + 1],
                        grow=gr[2 * heads + ch:2 * heads + ch + 1, :]))
        kk = [_dot_nt(a["k"], jnp.concatenate([a["k"], a["k"]], axis=0)) for a in chains]
        qk = [_dot_nt(a["q"], a["k"]) for a in chains]
        dec = [jnp.where(a["incl"], jnp.exp(jnp.where(a["incl"], a["gcol"] - a["grow"], 0.0)), 0.0)
               for a in chains]
        zs = [jnp.where(left, -jnp.where(a["strict"], a["beta"] * kki * dci, 0.0), jnp.where(eye_right, 1.0, 0.0))
              for a, kki, dci in zip(chains, kk, dec)]
        for _ in range(int(math.log2(c))):
            zs = [_inv_dot(z[:, 0:c], z) + jnp.where(left, 0.0, z) for z in zs]
        egs = [jnp.exp(a["gcol"]) for a in chains]
        k32 = [a["k"].astype(F32) for a in chains]
        rhs = [jnp.concatenate([a["v"].astype(F32) * a["beta"], kf * (a["beta"] * eg)], axis=1)
               for a, kf, eg in zip(chains, k32, egs)]
        sols = [_inv_dot(z[:, c:2 * c], r) for z, r in zip(zs, rhs)]
        for a, sol, kf, eg, qki, dci in zip(chains, sols, k32, egs, qk, dec):
            bi, ch = a["bi"], a["ch"]
            glast = a["gcol"][a["last"]:a["last"] + 1, :]
            u_ref[bi, ch, ci] = sol[:, :HEAD_W]
            wq_ref[bi, ch, ci, 0:c, :] = sol[:, HEAD_W:].astype(BF16)
            wq_ref[bi, ch, ci, c:2 * c, :] = (a["q"].astype(F32) * eg).astype(BF16)
            kt_ref[bi, ch, ci] = (kf * jnp.exp(glast - a["gcol"])).astype(BF16)
            qkd_ref[bi, ch, ci] = (qki * dci[:, 0:c]).astype(BF16)
            cd_ref[bi, ch, ci] = jnp.broadcast_to(jnp.exp(glast), (1, LANES))
        return carry

    lax.fori_loop(0, nsub, local, 0)

    def step(si, carry):
        chains = []
        for bi in range(nb):
            for di, o_ref in enumerate((of_ref, ob_ref)):
                ci = si if di == 0 else nsub - 1 - si
                rows = pl.ds(pl.multiple_of(ci * c, c), c)
                chains += [(bi, di * heads + h, h, ci, rows, o_ref) for h in range(heads)]
        ss = [s_ref[bi, ch] for bi, ch, _, _, _, _ in chains]
        ws = [_dot(wq_ref[bi, ch, ci], s.astype(BF16))
              for (bi, ch, _, ci, _, _), s in zip(chains, ss)]
        vn = [(u_ref[bi, ch, ci] - w[0:c]).astype(BF16) for (bi, ch, _, ci, _, _), w in zip(chains, ws)]
        outs = [w[c:2 * c] + _dot(qkd_ref[bi, ch, ci], v) for (bi, ch, _, ci, _, _), w, v in zip(chains, ws, vn)]
        sn = [cd_ref[bi, ch, ci] * s + _dot_tn(kt_ref[bi, ch, ci], v)
              for (bi, ch, _, ci, _, _), s, v in zip(chains, ss, vn)]
        for (bi, ch, h, ci, rows, o_ref), o, s in zip(chains, outs, sn):
            o_ref[bi, h, rows, :] = o.astype(o_ref.dtype)
            s_ref[bi, ch] = s
        return carry

    lax.fori_loop(0, nsub, step, 0)

    @pl.when(t == nt - 1)
    def _():
        sout_ref[...] = s_ref[...]


def _gdn_chunked(qkv_hm, gcol, s0, heads):
    bsz, _, s, _ = qkv_hm.shape
    c = GDN_CHUNK
    nsub = min(GDN_CHUNKS_PER_STEP, s // c)
    tb = nsub * c
    nt = s // tb
    grow = gcol[:, :, :4 * heads].reshape(bsz, s // c, c, 4 * heads).transpose(0, 1, 3, 2)
    grow = jnp.concatenate([grow, grow], axis=-1)
    nb = 2 if bsz % 2 == 0 else 1
    fwd = lambda b, t: (b, 0, t, 0)
    bwd = lambda b, t: (b, 0, nt - 1 - t, 0)
    nch = 2 * heads
    o_sds = jax.ShapeDtypeStruct((bsz, heads, s, HEAD_W), BF16)
    return pl.pallas_call(
        functools.partial(_gdn_chunk_kernel, heads=heads, nsub=nsub),
        grid=(bsz // nb, nt),
        in_specs=[pl.BlockSpec((nb, 3 * heads, tb, HEAD_W), fwd),
                  pl.BlockSpec((nb, 3 * heads, tb, HEAD_W), bwd),
                  pl.BlockSpec((nb, tb, LANES), lambda b, t: (b, t, 0)),
                  pl.BlockSpec((nb, tb, LANES), lambda b, t: (b, nt - 1 - t, 0)),
                  pl.BlockSpec((nb, nsub, 4 * heads, 2 * c), lambda b, t: (b, t, 0, 0)),
                  pl.BlockSpec((nb, nsub, 4 * heads, 2 * c), lambda b, t: (b, nt - 1 - t, 0, 0)),
                  pl.BlockSpec((nb, nch, HEAD_W, HEAD_W), lambda b, t: (b, 0, 0, 0))],
        out_specs=(pl.BlockSpec((nb, heads, tb, HEAD_W), fwd),
                   pl.BlockSpec((nb, heads, tb, HEAD_W), bwd),
                   pl.BlockSpec((nb, nch, HEAD_W, HEAD_W), lambda b, t: (b, 0, 0, 0))),
        out_shape=(o_sds, o_sds, jax.ShapeDtypeStruct(s0.shape, F32)),
        scratch_shapes=[pltpu.VMEM((nb, nch, HEAD_W, HEAD_W), F32),
                        pltpu.VMEM((nb, nch, nsub, c, HEAD_W), F32),
                        pltpu.VMEM((nb, nch, nsub, 2 * c, HEAD_W), BF16),
                        pltpu.VMEM((nb, nch, nsub, c, HEAD_W), BF16),
                        pltpu.VMEM((nb, nch, nsub, c, c), BF16),
                        pltpu.VMEM((nb, nch, nsub, 1, LANES), F32)],
        compiler_params=_cparams(2), name="gdn_chunk",
    )(qkv_hm, qkv_hm, gcol, gcol, grow, grow, s0)


def _route_rows(logits_t, rb):
    s = _sigmoid(logits_t)
    sel = s + rb
    rows = [sel[i:i + 1, :] for i in range(N_EXPERTS)]
    srows = [s[i:i + 1, :] for i in range(N_EXPERTS)]
    gscore = []
    for g in range(N_GROUPS):
        a, b, c, d = rows[GROUP_SIZE * g:GROUP_SIZE * (g + 1)]
        hi_ab, lo_ab = jnp.maximum(a, b), jnp.minimum(a, b)
        hi_cd, lo_cd = jnp.maximum(c, d), jnp.minimum(c, d)
        m1 = jnp.maximum(hi_ab, hi_cd)
        m2 = jnp.maximum(jnp.maximum(lo_ab, lo_cd), jnp.minimum(hi_ab, hi_cd))
        gscore.append(m1 + m2)
    best = gscore[0]
    bg = jnp.zeros_like(best)
    for g in range(1, N_GROUPS):
        upd = gscore[g] > best
        bg = jnp.where(upd, float(g), bg)
        best = jnp.where(upd, gscore[g], best)

    def pick(rws, j):
        out = rws[(N_GROUPS - 1) * GROUP_SIZE + j]
        for g in range(N_GROUPS - 2, -1, -1):
            out = jnp.where(bg == float(g), rws[g * GROUP_SIZE + j], out)
        return out

    v = [pick(rows, j) for j in range(GROUP_SIZE)]
    sv = [pick(srows, j) for j in range(GROUP_SIZE)]
    i1, b1, w1 = jnp.zeros_like(best), v[0], sv[0]
    for j in range(1, GROUP_SIZE):
        upd = v[j] > b1
        i1 = jnp.where(upd, float(j), i1)
        b1 = jnp.where(upd, v[j], b1)
        w1 = jnp.where(upd, sv[j], w1)
    i2 = b2 = w2 = None
    for j in range(GROUP_SIZE):
        ok = i1 != float(j)
        if b2 is None:
            i2 = jnp.where(ok, 0.0, 1.0)
            b2 = jnp.where(ok, v[0], v[1])
            w2 = jnp.where(ok, sv[0], sv[1])
            continue
        upd = ok & (v[j] > b2)
        i2 = jnp.where(upd, float(j), i2)
        b2 = jnp.where(upd, v[j], b2)
        w2 = jnp.where(upd, sv[j], w2)
    tot = w1 + w2
    return bg * GROUP_SIZE + i1, bg * GROUP_SIZE + i2, w1 / tot, w2 / tot


def _slot_ranks(e1, e2, base):
    tm = e1.shape[1]
    eid = lax.broadcasted_iota(jnp.int32, (N_EXPERTS, tm), 0).astype(F32)
    oh1 = jnp.where(eid == e1, 1.0, 0.0)
    oh2 = jnp.where(eid == e2, 1.0, 0.0)
    before = (lax.broadcasted_iota(jnp.int32, (tm, tm), 0) < lax.broadcasted_iota(jnp.int32, (tm, tm), 1))
    before = jnp.where(before, 1.0, 0.0).astype(BF16)
    pre = _dot(jnp.concatenate([oh1, oh2], axis=0).astype(BF16), before)
    n1 = jnp.sum(oh1, axis=1, keepdims=True)
    n2 = jnp.sum(oh2, axis=1, keepdims=True)
    r1 = jnp.sum(oh1 * (pre[0:N_EXPERTS] + base), axis=0, keepdims=True)
    r2 = jnp.sum(oh2 * (pre[N_EXPERTS:] + (base + n1)), axis=0, keepdims=True)
    return r1, r2, base + (n1 + n2)


def _out_proj_kernel(*refs, pro, heads):
    if pro == "gdn":
        of_ref, ob_ref, z_ref, gn_ref = refs[:4]
        rest = refs[4:]
        parts = []
        for h in range(heads):
            o = of_ref[0, h].astype(F32) + ob_ref[0, h].astype(F32)
            parts.append(_rms(o, gn_ref[...]))
        a = (jnp.concatenate(parts, axis=1) * _silu(z_ref[0].astype(F32))).astype(BF16)
    else:
        a = refs[0][0]
        rest = refs[1:]
    x_ref, mod_ref, w_ref, b_ref, g_ref, rw_ref, rb_ref, cnt0_ref, x1_ref, h2_ref, rt_ref, cnt_ref = rest

    @pl.when((pl.program_id(0) == 0) & (pl.program_id(1) == 0))
    def _():
        cnt_ref[...] = cnt0_ref[...]

    y = _dot(a, w_ref[...]) + b_ref[...]
    x1 = x_ref[0] + mod_ref[0, 2:3, :] * y
    x1_ref[0] = x1
    h2 = _norm_mod(x1, g_ref[...], mod_ref[0, 3:4, :], mod_ref[0, 4:5, :]).astype(BF16)
    h2_ref[0] = h2
    e1, e2, w1, w2 = _route_rows(_dot_nt(rw_ref[...], h2), rb_ref[...])
    r1, r2, cnt = _slot_ranks(e1, e2, cnt_ref[:, 0:1])
    zero = jnp.zeros_like(e1)
    rt_ref[0] = jnp.concatenate([e1, e2, w1, w2, r1, r2, zero, zero], axis=0)
    cnt_ref[...] = jnp.broadcast_to(cnt, cnt_ref.shape)


def _out_proj(pro, acts, x, mod, w, bias, norm_g, router_wt, router_b, cnt0, *, tm=512):
    g_, s, d = x.shape
    tm = _row_tile(s, tm)
    heads = d // HEAD_W
    full2 = lambda v: pl.BlockSpec(v.shape, lambda b, t: (0, 0))
    row3 = lambda width: pl.BlockSpec((1, tm, width), lambda b, t: (b, t, 0))
    if pro == "gdn":
        hm = pl.BlockSpec((1, heads, tm, HEAD_W), lambda b, t: (b, 0, t, 0))
        zcol = 3
        act_specs = [hm, hm, pl.BlockSpec((1, tm, d), lambda b, t: (b, t, zcol)), full2(acts[3])]
    else:
        act_specs = [row3(d)]
    in_specs = act_specs + [row3(d), pl.BlockSpec((1, 6, d), lambda b, t: (b, 0, 0)),
                            full2(w), full2(bias), full2(norm_g), full2(router_wt), full2(router_b), full2(cnt0)]
    return pl.pallas_call(
        functools.partial(_out_proj_kernel, pro=pro, heads=heads),
        grid=(g_, s // tm),
        in_specs=in_specs,
        out_specs=(row3(d), row3(d), pl.BlockSpec((1, 8, tm), lambda b, t: (b, 0, t)), full2(cnt0)),
        out_shape=(jax.ShapeDtypeStruct((g_, s, d), F32), jax.ShapeDtypeStruct((g_, s, d), BF16),
                   jax.ShapeDtypeStruct((g_, 8, s), F32), jax.ShapeDtypeStruct(cnt0.shape, F32)),
        compiler_params=_cparams(2), name="out_proj_" + pro,
    )(*acts, x, mod, w, bias, norm_g, router_wt, router_b, cnt0)


def _moe_ffn_kernel(te_ref, nv_ref, x_ref, wg_ref, wu_ref, wd_ref, o_ref):
    i = pl.program_id(0)

    @pl.when(i < nv_ref[0])
    def _():
        x = x_ref[...]
        a = (_silu(_dot(x, wg_ref[0, 0].astype(BF16))) * _dot(x, wu_ref[0, 0].astype(BF16))).astype(BF16)
        o_ref[...] = _dot(a, wd_ref[0, 0].astype(BF16)).astype(o_ref.dtype)

    @pl.when(i >= nv_ref[0])
    def _():
        o_ref[...] = jnp.zeros_like(o_ref)


def _moe_ffn(xs, tile_expert, n_valid, wg, wu, wd, layer, *, tm):
    r, d = xs.shape
    f = wg.shape[-1]
    return pl.pallas_call(
        _moe_ffn_kernel,
        grid_spec=pltpu.PrefetchScalarGridSpec(
            num_scalar_prefetch=2, grid=(r // tm,),
            in_specs=[pl.BlockSpec((tm, d), lambda i, te, nv: (i, 0)),
                      pl.BlockSpec((1, 1, d, f), lambda i, te, nv: (layer, te[i], 0, 0)),
                      pl.BlockSpec((1, 1, d, f), lambda i, te, nv: (layer, te[i], 0, 0)),
                      pl.BlockSpec((1, 1, f, d), lambda i, te, nv: (layer, te[i], 0, 0))],
            out_specs=pl.BlockSpec((tm, d), lambda i, te, nv: (i, 0))),
        out_shape=jax.ShapeDtypeStruct((r, d), BF16),
        compiler_params=_cparams(1), name="moe_ffn",
    )(tile_expert, n_valid, xs, wg, wu, wd)


def _combine_kernel(x_ref, y0_ref, y1_ref, rt_ref, mod_ref, g_ref, o_ref, *, final):
    rt = rt_ref[0]
    y = rt[:, 2:3] * y0_ref[0].astype(F32) + rt[:, 3:4] * y1_ref[0].astype(F32)
    x2 = x_ref[0] + mod_ref[0, 5:6, :] * y
    o_ref[0] = _rms(x2, g_ref[...]) if final else x2


def _combine(x1, y0, y1, route_t, mod, final_g, *, final, tm=512):
    g_, s, d = x1.shape
    tm = _row_tile(s, tm)
    row3 = pl.BlockSpec((1, tm, d), lambda b, t: (b, t, 0))
    return pl.pallas_call(
        functools.partial(_combine_kernel, final=final),
        grid=(g_, s // tm),
        in_specs=[row3, row3, row3,
                  pl.BlockSpec((1, tm, 8), lambda b, t: (b, t, 0)),
                  pl.BlockSpec((1, 6, d), lambda b, t: (b, 0, 0)),
                  pl.BlockSpec(final_g.shape, lambda b, t: (0, 0))],
        out_specs=row3,
        out_shape=jax.ShapeDtypeStruct((g_, s, d), F32),
        compiler_params=_cparams(2), name="moe_combine",
    )(x1, y0, y1, route_t, mod, final_g)


def _moe(h2_list, route_list, counts, wg, wu, wd, layer, *, tm=512):
    d = h2_list[0].shape[-1]
    h2 = jnp.concatenate([h.reshape(-1, d) for h in h2_list], axis=0)
    n = h2.shape[0]
    cnt = counts[:, 0].astype(jnp.int32)
    padded = ((cnt + tm - 1) // tm) * tm
    ends = jnp.cumsum(padded)
    starts = (ends - padded).astype(F32)
    pos_list = []
    for r in route_list:
        e, rank = r[:, 0:2, :], r[:, 4:6, :]
        first = jnp.zeros_like(e)
        for k in range(N_EXPERTS):
            first = jnp.where(e == float(k), starts[k], first)
        pos_list.append((first + rank).astype(jnp.int32))
    pos_all = jnp.concatenate([p.transpose(1, 0, 2).reshape(2, -1) for p in pos_list], axis=1)
    n_tiles = (2 * n) // tm + N_EXPERTS
    tok = jnp.arange(n, dtype=jnp.int32)
    src = jnp.zeros((n_tiles * tm,), jnp.int32).at[pos_all.reshape(-1)].set(jnp.concatenate([tok, tok]))
    n_valid = (ends[-1] // tm).astype(jnp.int32)
    tile_start = jnp.minimum(jnp.arange(n_tiles, dtype=jnp.int32), n_valid - 1) * tm
    tile_expert = jnp.sum((tile_start[:, None] >= ends[None, :]).astype(jnp.int32), axis=1)
    tile_expert = jnp.minimum(tile_expert, N_EXPERTS - 1)
    xs = jnp.take(h2, src, axis=0)
    ys = _moe_ffn(xs, tile_expert, n_valid.reshape(1), wg, wu, wd, layer, tm=tm)
    outs = []
    for h, r, p in zip(h2_list, route_list, pos_list):
        y0 = jnp.take(ys, p[:, 0, :].reshape(-1), axis=0).reshape(h.shape)
        y1 = jnp.take(ys, p[:, 1, :].reshape(-1), axis=0).reshape(h.shape)
        outs.append((y0, y1, r.transpose(0, 2, 1)))
    return outs


def _rope_tables(seqlen):
    pos = jnp.arange(seqlen, dtype=jnp.int32)
    row = (pos // GRID_W).astype(F32)
    col = (pos % GRID_W).astype(F32)
    inv_freq = ROPE_THETA ** (-jnp.arange(ROPE_FREQS, dtype=F32) / ROPE_FREQS)
    lane = jnp.arange(LANES)
    within = lane % DA_HEAD_DIM
    axis = within // (2 * ROPE_FREQS)
    half = (within % (2 * ROPE_FREQS)) // ROPE_FREQS
    ang = jnp.where(axis[None, :] == 0, row[:, None], col[:, None]) * inv_freq[within % ROPE_FREQS][None, :]
    sign = jnp.where(half == 0, -1.0, 1.0).astype(F32)
    return jnp.cos(ang), jnp.sin(ang) * sign[None, :]


def kernel(x, c, ctx, c_ctx, ada_w, ada_b, norm_mix_g, norm_ffn_g, final_norm_g, conv_pw1_w, conv_pw1_b, conv_dw_w, conv_dw_b, conv_ln_g, conv_ln_b, conv_pw2_w, conv_pw2_b, diff_w_qkv, diff_lambda, diff_subln_g, diff_w_o, gdn_w_in, gdn_conv_w, gdn_a_log, gdn_dt_bias, gdn_norm_g, gdn_w_o, router_w, router_b, moe_w_gate, moe_w_up, moe_w_down):
    bsz, seqlen, d = x.shape
    n_ctx = ctx.shape[1]
    depth = ada_w.shape[0]
    heads = d // HEAD_W
    row = lambda v: v.reshape(1, -1).astype(F32)

    mods = _ada_all(jnp.concatenate([c, c_ctx[None, :]], axis=0), ada_w, ada_b)
    mods = mods.reshape(depth, bsz + 1, 6, d)
    router_wt = router_w.T.astype(BF16)
    router_bc = router_b.reshape(N_EXPERTS, 1).astype(F32)
    cos_t, sin_t = _rope_tables(seqlen)
    zero_bias = jnp.zeros((1, d), F32)

    def layer(i, lat, cx, mods_i, w):
        gb = lat.shape[0]
        last = i == depth - 1
        kind, j = i % N_MIXERS, i // N_MIXERS
        need_ctx = (not last) or kind != 0
        m_lat, m_ctx = mods_i
        gmix = row(norm_mix_g[i])
        streams = [(lat, m_lat)] + ([(cx, m_ctx)] if need_ctx else [])
        acts = []
        if kind == 0:
            for xs, ms in streams:
                a = _in_proj("glu", xs, ms, gmix, [w["w1"], row(conv_pw1_b[j])])
                shp = a.shape
                a = a.reshape(gb, -1, d)
                a = _dwconv_ln_silu(a, conv_dw_w[j].astype(F32), row(conv_dw_b[j]),
                                    row(conv_ln_g[j]), row(conv_ln_b[j]))
                acts.append(("plain", [a.reshape(shp)]))
        elif kind == 1:
            lam_init = 0.8 - 0.6 * math.exp(-0.3 * i)
            qscale = DA_HEAD_DIM ** -0.5 * math.log2(math.e)
            qkv_l = _in_proj("qkv", lat, m_lat, gmix, [w["wq"]], (cos_t, sin_t), rope=True, qscale=qscale)
            qkv_c = _in_proj("qkv", cx, m_ctx, gmix, [w["wq"]], qscale=qscale).reshape(gb, n_ctx, 3 * d)
            lam_v, sub_g = diff_lambda[j].astype(F32), row(diff_subln_g[j])
            o_l = _diff_attn_core(qkv_l, [qkv_c, qkv_l], lam_v, sub_g, lam_init)
            acts.append(("plain", [o_l]))
            if not last:
                o_c = _diff_attn_core(qkv_c, [qkv_c], lam_v, sub_g, lam_init)
                acts.append(("plain", [o_c.reshape(cx.shape)]))
        else:
            conv_w = gdn_conv_w[j].astype(F32)
            zz_l, ba_l = _in_proj("gdn", lat, m_lat, gmix, [w["w_main"], w["w_ba"]])
            zz_c, ba_c = _in_proj("gdn", cx, m_ctx, gmix, [w["w_main"], w["w_ba"]])
            zz_c = zz_c.reshape(gb, n_ctx, -1)
            ba_c = ba_c.reshape(gb, n_ctx, LANES)
            s0 = jnp.zeros((gb, 2 * heads, HEAD_W, HEAD_W), F32)
            ocf, ocb, s1 = _gdn_chunked(_gdn_prep(zz_c, conv_w),
                                        _gdn_gates(ba_c, gdn_a_log[j], gdn_dt_bias[j], heads), s0, heads)
            olf, olb, _ = _gdn_chunked(_gdn_prep(zz_l, conv_w),
                                       _gdn_gates(ba_l, gdn_a_log[j], gdn_dt_bias[j], heads), s1, heads)
            gn = row(gdn_norm_g[j])
            acts.append(("gdn", [olf, olb, zz_l, gn]))
            if not last:
                acts.append(("gdn", [ocf, ocb, zz_c, gn]))

        gffn = row(norm_ffn_g[i])
        x1s, h2s, routes = [], [], []
        counts = jnp.zeros((N_EXPERTS, LANES), F32)
        for (xs, ms), (pro, a) in zip(streams, acts):
            if pro == "gdn" and xs.shape[0] != a[0].shape[0]:
                xin = xs.reshape(gb, -1, d)
                msb = jnp.broadcast_to(ms, (gb, 6, d))
                x1, h2, rt, counts = _out_proj(pro, a, xin, msb, w["w_out"], w["b_out"], gffn,
                                               router_wt, router_bc, counts)
                x1, h2 = x1.reshape(xs.shape), h2.reshape(xs.shape)
                rt = rt.transpose(1, 0, 2).reshape(1, 8, -1)
            else:
                x1, h2, rt, counts = _out_proj(pro, a, xs, ms, w["w_out"], w["b_out"], gffn,
                                               router_wt, router_bc, counts)
            x1s.append(x1)
            h2s.append(h2)
            routes.append(rt)
        moe_out = _moe(h2s, routes, counts, moe_w_gate, moe_w_up, moe_w_down, i)
        fg = row(final_norm_g)
        lat = _combine(x1s[0], *moe_out[0], m_lat, fg, final=last)
        if not last:
            cx = _combine(x1s[1], *moe_out[1], m_ctx, fg, final=False)
        return lat, cx

    def layer_weights(i):
        kind, j = i % N_MIXERS, i // N_MIXERS
        if kind == 0:
            return dict(w1=conv_pw1_w[j].astype(BF16), w_out=conv_pw2_w[j].astype(BF16), b_out=row(conv_pw2_b[j]))
        if kind == 1:
            return dict(wq=diff_w_qkv[j].astype(BF16), w_out=diff_w_o[j].astype(BF16), b_out=zero_bias)
        cch = gdn_conv_w.shape[-1]
        w_in = gdn_w_in[j]
        w_ba = jnp.zeros((d, LANES), BF16).at[:, :4 * heads].set(w_in[:, cch + d:].astype(BF16))
        return dict(w_main=w_in[:, :cch + d].astype(BF16), w_ba=w_ba,
                    w_out=gdn_w_o[j].astype(BF16), b_out=zero_bias)

    groups = BATCH_GROUPS if bsz % BATCH_GROUPS == 0 else 1
    gb = bsz // groups
    state = [(x[g * gb:(g + 1) * gb], ctx[g * gb:(g + 1) * gb].reshape(1, gb * n_ctx, d)) for g in range(groups)]
    for i in range(depth):
        w = layer_weights(i)
        state = [layer(i, lat, cx, (mods[i, g * gb:(g + 1) * gb], mods[i, bsz:]), w)
                 for g, (lat, cx) in enumerate(state)]
    return jnp.concatenate([lat for lat, _ in state], axis=0) if groups > 1 else state[0][0]
```

```python
import functools
import math

import jax
import jax.numpy as jnp
from jax import lax
from jax.experimental import pallas as pl
from jax.experimental.pallas import tpu as pltpu

F32 = jnp.float32
BF16 = jnp.bfloat16
HIGHEST = lax.Precision.HIGHEST

NORM_EPS = 1e-6
N_MIXERS = 3
GRID_W = 64
ROPE_THETA = 10000.0
DA_HEAD_DIM = 64
ROPE_FREQS = DA_HEAD_DIM // 4
HEAD_W = 128
GDN_CHUNK = 64
GDN_CHUNKS_PER_STEP = 4
N_EXPERTS = 16
N_GROUPS = 4
GROUP_SIZE = N_EXPERTS // N_GROUPS
LANES = 128
SUBLANES = 8
CONV_HALO = 16
VMEM_LIMIT = 56 * 1024 * 1024


def _cparams(n_axes):
    return pltpu.CompilerParams(dimension_semantics=("arbitrary",) * n_axes,
                                vmem_limit_bytes=VMEM_LIMIT)


def _dot(a, b, **kw):
    return jnp.dot(a, b, preferred_element_type=F32, **kw)


def _dot_nt(a, b, **kw):
    return lax.dot_general(a, b, (((1,), (1,)), ((), ())), preferred_element_type=F32, **kw)


def _dot_tn(a, b, **kw):
    return lax.dot_general(a, b, (((0,), (0,)), ((), ())), preferred_element_type=F32, **kw)


def _sigmoid(x):
    return 1.0 / (1.0 + jnp.exp(-x))


def _silu(x):
    return x * _sigmoid(x)


def _rms(x, g):
    return x * lax.rsqrt(jnp.mean(x * x, axis=-1, keepdims=True) + NORM_EPS) * g


def _norm_mod(x, g, shift, scale):
    return _rms(x, g) * (1.0 + scale) + shift


def _row_tile(n, want):
    t = min(n, want)
    assert n % t == 0, (n, t)
    return t


def _ada_kernel(c_ref, w_ref, b_ref, o_ref):
    s = _silu(c_ref[...]).astype(BF16)
    o_ref[0] = _dot(s, w_ref[0].astype(BF16)) + b_ref[0]


def _ada_all(cvec, ada_w, ada_b):
    depth, d, n = ada_w.shape
    r = cvec.shape[0]
    tn = d
    return pl.pallas_call(
        _ada_kernel,
        grid=(depth, n // tn),
        in_specs=[pl.BlockSpec((r, d), lambda i, j: (0, 0)),
                  pl.BlockSpec((1, d, tn), lambda i, j: (i, 0, j)),
                  pl.BlockSpec((1, 1, tn), lambda i, j: (i, 0, j))],
        out_specs=pl.BlockSpec((1, r, tn), lambda i, j: (i, 0, j)),
        out_shape=jax.ShapeDtypeStruct((depth, r, n), F32),
        compiler_params=_cparams(2),
        name="adaln",
    )(cvec, ada_w, ada_b.reshape(depth, 1, n))


def _glu_kernel(x_ref, mod_ref, g_ref, w_ref, b_ref, o_ref, *, tn):
    h = _norm_mod(x_ref[0], g_ref[...], mod_ref[0, 0:1, :], mod_ref[0, 1:2, :]).astype(BF16)
    d = o_ref.shape[-1]
    for j in range(0, d, tn):
        a = _dot(h, w_ref[:, j:j + tn]) + b_ref[:, j:j + tn]
        gt = _dot(h, w_ref[:, d + j:d + j + tn]) + b_ref[:, d + j:d + j + tn]
        o_ref[0, :, j:j + tn] = (a * _sigmoid(gt)).astype(o_ref.dtype)


def _rope_lanes(t, cos, sin_signed, first_half):
    swapped = jnp.where(first_half, pltpu.roll(t, LANES - ROPE_FREQS, 1), pltpu.roll(t, ROPE_FREQS, 1))
    return t * cos + swapped * sin_signed


def _qkv_kernel(*refs, tn, rope, qscale):
    if rope:
        x_ref, mod_ref, g_ref, w_ref, cos_ref, sin_ref, o_ref = refs
    else:
        x_ref, mod_ref, g_ref, w_ref, o_ref = refs
    h = _norm_mod(x_ref[0], g_ref[...], mod_ref[0, 0:1, :], mod_ref[0, 1:2, :]).astype(BF16)
    d = x_ref.shape[-1]
    if rope:
        cos = cos_ref[...]
        sin = sin_ref[...]
        lane = lax.broadcasted_iota(jnp.int32, cos.shape, 1)
        first_half = (lane % (2 * ROPE_FREQS)) < ROPE_FREQS
    for j in range(0, 3 * d, tn):
        y = _dot(h, w_ref[:, j:j + tn])
        if j < 2 * d and rope:
            y = jnp.concatenate(
                [_rope_lanes(y[:, s:s + LANES], cos, sin, first_half) for s in range(0, tn, LANES)], axis=1)
        if j < d:
            y = y * qscale
        o_ref[0, :, j:j + tn] = y.astype(o_ref.dtype)


def _gdn_in_kernel(x_ref, mod_ref, g_ref, w_ref, wba_ref, zz_ref, ba_ref, *, tn):
    h = _norm_mod(x_ref[0], g_ref[...], mod_ref[0, 0:1, :], mod_ref[0, 1:2, :]).astype(BF16)
    for j in range(0, zz_ref.shape[-1], tn):
        zz_ref[0, :, j:j + tn] = _dot(h, w_ref[:, j:j + tn]).astype(zz_ref.dtype)
    ba_ref[0] = _dot(h, wba_ref[...])


def _in_proj(kind, x, mod, norm_g, weights, extra=(), *, tm=512, tn=512, rope=False, qscale=1.0):
    g_, s, d = x.shape
    tm = _row_tile(s, tm)
    grid = (g_, s // tm)
    full2 = lambda a: pl.BlockSpec(a.shape, lambda b, t: (0, 0))
    in_specs = [pl.BlockSpec((1, tm, d), lambda b, t: (b, t, 0)),
                pl.BlockSpec((1, 6, d), lambda b, t: (b, 0, 0)),
                full2(norm_g)] + [full2(w) for w in weights]
    args = [x, mod, norm_g] + list(weights)
    if kind == "glu":
        body = functools.partial(_glu_kernel, tn=tn)
        out_shape = jax.ShapeDtypeStruct((g_, s, d), BF16)
        out_specs = pl.BlockSpec((1, tm, d), lambda b, t: (b, t, 0))
    elif kind == "qkv":
        body = functools.partial(_qkv_kernel, tn=tn, rope=rope, qscale=qscale)
        if rope:
            in_specs += [pl.BlockSpec((tm, LANES), lambda b, t: (t, 0))] * 2
            args += list(extra)
        out_shape = jax.ShapeDtypeStruct((g_, s, 3 * d), BF16)
        out_specs = pl.BlockSpec((1, tm, 3 * d), lambda b, t: (b, t, 0))
    else:
        body = functools.partial(_gdn_in_kernel, tn=tn)
        nz = weights[0].shape[1]
        out_shape = (jax.ShapeDtypeStruct((g_, s, nz), BF16), jax.ShapeDtypeStruct((g_, s, LANES), F32))
        out_specs = (pl.BlockSpec((1, tm, nz), lambda b, t: (b, t, 0)),
                     pl.BlockSpec((1, tm, LANES), lambda b, t: (b, t, 0)))
    return pl.pallas_call(body, grid=grid, in_specs=in_specs, out_specs=out_specs, out_shape=out_shape,
                          compiler_params=_cparams(2), name="in_proj_" + kind)(*args)


def _dwconv_kernel(a_ref, prev_ref, next_ref, w_ref, b_ref, lg_ref, lb_ref, o_ref, buf_ref, acc_ref,
                   *, kw, rows, cols):
    t = pl.program_id(1)
    nt = pl.num_programs(1)
    tm, d = a_ref.shape[1], a_ref.shape[2]
    halo = CONV_HALO
    span = tm + 2 * halo
    buf_ref[0, 0:halo, :] = jnp.where(t > 0, prev_ref[0].astype(F32), 0.0)
    buf_ref[0, halo:halo + tm, :] = a_ref[0].astype(F32)
    buf_ref[0, halo + tm:span, :] = jnp.where(t < nt - 1, next_ref[0].astype(F32), 0.0)
    buf_ref[0, span:, :] = jnp.zeros((SUBLANES, d), F32)
    for s in range(1, SUBLANES):
        buf_ref[s, 0:span, :] = buf_ref[0, s:s + span, :]
    base = halo - kw // 2
    for r0 in range(0, tm, rows):
        for c0 in range(0, d, cols):
            acc = jnp.zeros((rows, cols), F32)
            for k in range(kw):
                off = base + k + r0
                s = off % SUBLANES
                acc = acc + w_ref[k:k + 1, c0:c0 + cols] * buf_ref[s, off - s:off - s + rows, c0:c0 + cols]
            acc_ref[r0:r0 + rows, c0:c0 + cols] = acc
    y = acc_ref[...] + b_ref[...]
    yc = y - jnp.mean(y, axis=-1, keepdims=True)
    yn = yc * lax.rsqrt(jnp.mean(yc * yc, axis=-1, keepdims=True) + NORM_EPS) * lg_ref[...] + lb_ref[...]
    o_ref[0] = _silu(yn).astype(o_ref.dtype)


def _dwconv_ln_silu(a, dw_w, dw_b, ln_g, ln_b, *, tm=256):
    bsz, s, d = a.shape
    kw = dw_w.shape[0]
    assert kw // 2 < CONV_HALO
    tm = _row_tile(s, tm)
    hb = tm // CONV_HALO
    nhb = s // CONV_HALO
    full2 = lambda v: pl.BlockSpec(v.shape, lambda b, t: (0, 0))
    return pl.pallas_call(
        functools.partial(_dwconv_kernel, kw=kw, rows=64, cols=256),
        grid=(bsz, s // tm),
        in_specs=[pl.BlockSpec((1, tm, d), lambda b, t: (b, t, 0)),
                  pl.BlockSpec((1, CONV_HALO, d), lambda b, t: (b, jnp.maximum(t * hb - 1, 0), 0)),
                  pl.BlockSpec((1, CONV_HALO, d), lambda b, t: (b, jnp.minimum((t + 1) * hb, nhb - 1), 0)),
                  full2(dw_w), full2(dw_b), full2(ln_g), full2(ln_b)],
        out_specs=pl.BlockSpec((1, tm, d), lambda b, t: (b, t, 0)),
        out_shape=jax.ShapeDtypeStruct((bsz, s, d), BF16),
        scratch_shapes=[pltpu.VMEM((SUBLANES, tm + 2 * CONV_HALO + SUBLANES, d), F32), pltpu.VMEM((tm, d), F32)],
        compiler_params=_cparams(2), name="dwconv_ln_silu",
    )(a, a, a, dw_w, dw_b, ln_g, ln_b)


def _attn_kernel(*refs, n_kv, lam_init, hps):
    lam_ref, sg_ref, q_ref = refs[:3]
    k_refs = refs[3:3 + n_kv]
    v_refs = refs[3 + n_kv:3 + 2 * n_kv]
    o_ref = refs[3 + 2 * n_kv]
    lv = lam_ref[...]
    lam = (jnp.exp(jnp.sum(lv[0:1] * lv[1:2], axis=-1, keepdims=True))
           - jnp.exp(jnp.sum(lv[2:3] * lv[3:4], axis=-1, keepdims=True)) + lam_init)
    lane = lax.broadcasted_iota(jnp.int32, (q_ref.shape[1], HEAD_W), 1)
    scores = []
    for h in range(hps):
        hl = slice(h * HEAD_W, (h + 1) * HEAD_W)
        q = q_ref[0, :, hl]
        for c in range(2):
            in_map = (lane < DA_HEAD_DIM) if c == 0 else (lane >= DA_HEAD_DIM)
            qc = jnp.where(in_map, q, jnp.zeros_like(q))
            scores.append([_dot_nt(qc, k_ref[0, :, hl]) for k_ref in k_refs])
    outs = []
    for i, ss in enumerate(scores):
        hl = slice((i // 2) * HEAD_W, (i // 2 + 1) * HEAD_W)
        m = ss[0].max(axis=-1, keepdims=True)
        for s_ in ss[1:]:
            m = jnp.maximum(m, s_.max(axis=-1, keepdims=True))
        l = None
        acc = None
        for s_, v_ref in zip(ss, v_refs):
            p = jnp.exp2(s_ - m)
            ls = jnp.sum(p, axis=-1, keepdims=True)
            pv = _dot(p.astype(BF16), v_ref[0, :, hl])
            l = ls if l is None else l + ls
            acc = pv if acc is None else acc + pv
        outs.append(acc / l)
    for h in range(hps):
        o = outs[2 * h] - lam * outs[2 * h + 1]
        o = _rms(o, sg_ref[...]) * (1.0 - lam_init)
        o_ref[0, :, h * HEAD_W:(h + 1) * HEAD_W] = o.astype(o_ref.dtype)


def _diff_attn_core(q_src, kv_srcs, lam_vecs, subln_g, lam_init, *, tq=512, hps=2):
    bsz, sq, d3 = q_src.shape
    d = d3 // 3
    nh = d // HEAD_W
    assert nh % hps == 0
    nhb = nh // hps
    wb = hps * HEAD_W
    tq = _row_tile(sq, tq)
    n_kv = len(kv_srcs)
    in_specs = [pl.BlockSpec(lam_vecs.shape, lambda b, h, t: (0, 0)),
                pl.BlockSpec(subln_g.shape, lambda b, h, t: (0, 0)),
                pl.BlockSpec((1, tq, wb), lambda b, h, t: (b, t, h))]
    in_specs += [pl.BlockSpec((1, s.shape[1], wb), lambda b, h, t: (b, 0, nhb + h)) for s in kv_srcs]
    in_specs += [pl.BlockSpec((1, s.shape[1], wb), lambda b, h, t: (b, 0, 2 * nhb + h)) for s in kv_srcs]
    return pl.pallas_call(
        functools.partial(_attn_kernel, n_kv=n_kv, lam_init=lam_init, hps=hps),
        grid=(bsz, nhb, sq // tq),
        in_specs=in_specs,
        out_specs=pl.BlockSpec((1, tq, wb), lambda b, h, t: (b, t, h)),
        out_shape=jax.ShapeDtypeStruct((bsz, sq, d), BF16),
        compiler_params=_cparams(3), name="diff_attn",
    )(lam_vecs, subln_g, q_src, *kv_srcs, *kv_srcs)


def _gdn_prep_kernel(zz_ref, w_ref, o_ref, buf_ref, *, kw, heads, qk_scale):
    s = zz_ref.shape[1]
    width = heads * HEAD_W
    pad = SUBLANES
    rows = min(s, 256)
    buf_ref[0:pad, :] = jnp.zeros((pad, width), F32)
    buf_ref[pad:pad + s, :] = zz_ref[0].astype(F32)
    buf_ref[pad + s:, :] = jnp.zeros((pad, width), F32)
    kind = pl.program_id(1)
    for h in range(heads):
        lanes = slice(h * HEAD_W, (h + 1) * HEAD_W)
        for r0 in range(0, s, rows):
            acc = jnp.zeros((rows, HEAD_W), F32)
            for k in range(kw):
                off = pad + k - kw // 2 + r0
                acc = acc + w_ref[k:k + 1, lanes] * buf_ref[off:off + rows, lanes]
            y = _silu(acc)
            nrm = lax.rsqrt(jnp.sum(y * y, axis=-1, keepdims=True) + NORM_EPS)
            fac = jnp.where(kind == 0, nrm * qk_scale, jnp.where(kind == 1, nrm, jnp.ones_like(nrm)))
            o_ref[0, h, r0:r0 + rows, :] = (y * fac).astype(o_ref.dtype)


def _gdn_prep(zz, conv_w):
    bsz, s, _ = zz.shape
    kw, cch = conv_w.shape
    heads = cch // (3 * HEAD_W)
    width = heads * HEAD_W
    return pl.pallas_call(
        functools.partial(_gdn_prep_kernel, kw=kw, heads=heads, qk_scale=HEAD_W ** -0.5),
        grid=(bsz, 3),
        in_specs=[pl.BlockSpec((1, s, width), lambda b, j: (b, 0, j)),
                  pl.BlockSpec((kw, width), lambda b, j: (0, j))],
        out_specs=pl.BlockSpec((1, heads, s, HEAD_W), lambda b, j: (b, j, 0, 0)),
        out_shape=jax.ShapeDtypeStruct((bsz, 3 * heads, s, HEAD_W), BF16),
        scratch_shapes=[pltpu.VMEM((s + 2 * SUBLANES, width), F32)],
        compiler_params=_cparams(2), name="gdn_prep",
    )(zz, conv_w)


def _gdn_gates_kernel(ba_ref, alog_ref, dtb_ref, o_ref, *, heads):
    s = ba_ref.shape[1]
    c = GDN_CHUNK
    ii = lax.broadcasted_iota(jnp.int32, (c, c), 0)
    jj = lax.broadcasted_iota(jnp.int32, (c, c), 1)
    tri_f = (ii >= jj).astype(F32)
    tri_b = (ii <= jj).astype(F32)
    lane = lax.broadcasted_iota(jnp.int32, (c, LANES), 1)
    for r0 in range(0, s, c):
        x = ba_ref[0, r0:r0 + c, :]
        beta = _sigmoid(x)
        z = x + dtb_ref[...]
        softplus = jnp.maximum(z, 0.0) + jnp.log1p(jnp.exp(-jnp.abs(z)))
        g = -jnp.exp(alog_ref[...]) * softplus
        gf = _dot(tri_f, g, precision=HIGHEST)
        gb = _dot(tri_b, g, precision=HIGHEST)
        gsum = jnp.where(lane >= 3 * heads, gb, gf)
        o_ref[0, r0:r0 + c, :] = jnp.where(lane < 2 * heads, beta, gsum)


def _gdn_gates(ba, a_log, dt_bias, heads):
    bsz, s, _ = ba.shape
    pad = lambda v: jnp.zeros((1, LANES), F32).at[0, 2 * heads:4 * heads].set(v.reshape(-1).astype(F32))
    return pl.pallas_call(
        functools.partial(_gdn_gates_kernel, heads=heads),
        grid=(bsz,),
        in_specs=[pl.BlockSpec((1, s, LANES), lambda b: (b, 0, 0)),
                  pl.BlockSpec((1, LANES), lambda b: (0, 0)),
                  pl.BlockSpec((1, LANES), lambda b: (0, 0))],
        out_specs=pl.BlockSpec((1, s, LANES), lambda b: (b, 0, 0)),
        out_shape=jax.ShapeDtypeStruct((bsz, s, LANES), F32),
        compiler_params=_cparams(1), name="gdn_gates",
    )(ba, pad(a_log), pad(dt_bias))


def _inv_dot(a, b):
    return _dot(a.astype(BF16), b.astype(BF16))


def _gdn_chunk_kernel(qkv_f, qkv_b, gc_f, gc_b, gr_f, gr_b, s0_ref, of_ref, ob_ref, sout_ref,
                      s_ref, u_ref, wq_ref, kt_ref, qkd_ref, cd_ref, *, heads, nsub):
    t = pl.program_id(1)
    nt = pl.num_programs(1)
    c = GDN_CHUNK
    nb = s0_ref.shape[0]
    nch = 2 * heads

    @pl.when(t == 0)
    def _():
        s_ref[...] = s0_ref[...]

    ii = lax.broadcasted_iota(jnp.int32, (c, 2 * c), 0)
    lane = lax.broadcasted_iota(jnp.int32, (c, 2 * c), 1)
    jj = lane % c
    left = lane < c
    eye_right = (ii == jj) & (lane >= c)
    dirs = ((qkv_f, gc_f, gr_f, ii >= jj, ii > jj, c - 1), (qkv_b, gc_b, gr_b, ii <= jj, ii < jj, 0))

    def local(ci, carry):
        rows = pl.ds(pl.multiple_of(ci * c, c), c)
        chains = []
        for bi in range(nb):
            for di, (qkv, gcr, grr, incl, strict, last) in enumerate(dirs):
                gc = gcr[bi, rows, :]
                gr = grr[bi, ci]
                for h in range(heads):
                    ch = di * heads + h
                    chains.append(dict(
                        bi=bi, ch=ch, incl=incl, strict=strict, last=last,
                        q=qkv[bi, h, rows, :], k=qkv[bi, heads + h, rows, :], v=qkv[bi, 2 * heads + h, rows, :],
                        beta=gc[:, ch:ch + 1],
                        gcol=gc[:, 2 * heads + ch:2 * heads + ch + 1],
                        grow=gr[2 * heads + ch:2 * heads + ch + 1, :]))
        kk = [_dot_nt(a["k"], jnp.concatenate([a["k"], a["k"]], axis=0)) for a in chains]
        qk = [_dot_nt(a["q"], a["k"]) for a in chains]
        dec = [jnp.where(a["incl"], jnp.exp(jnp.where(a["incl"], a["gcol"] - a["grow"], 0.0)), 0.0)
               for a in chains]
        zs = [jnp.where(left, -jnp.where(a["strict"], a["beta"] * kki * dci, 0.0), jnp.where(eye_right, 1.0, 0.0))
              for a, kki, dci in zip(chains, kk, dec)]
        for _ in range(int(math.log2(c))):
            zs = [_inv_dot(z[:, 0:c], z) + jnp.where(left, 0.0, z) for z in zs]
        egs = [jnp.exp(a["gcol"]) for a in chains]
        k32 = [a["k"].astype(F32) for a in chains]
        rhs = [jnp.concatenate([a["v"].astype(F32) * a["beta"], kf * (a["beta"] * eg)], axis=1)
               for a, kf, eg in zip(chains, k32, egs)]
        sols = [_inv_dot(z[:, c:2 * c], r) for z, r in zip(zs, rhs)]
        for a, sol, kf, eg, qki, dci in zip(chains, sols, k32, egs, qk, dec):
            bi, ch = a["bi"], a["ch"]
            glast = a["gcol"][a["last"]:a["last"] + 1, :]
            u_ref[bi, ch, ci] = sol[:, :HEAD_W]
            wq_ref[bi, ch, ci, 0:c, :] = sol[:, HEAD_W:].astype(BF16)
            wq_ref[bi, ch, ci, c:2 * c, :] = (a["q"].astype(F32) * eg).astype(BF16)
            kt_ref[bi, ch, ci] = (kf * jnp.exp(glast - a["gcol"])).astype(BF16)
            qkd_ref[bi, ch, ci] = (qki * dci[:, 0:c]).astype(BF16)
            cd_ref[bi, ch, ci] = jnp.broadcast_to(jnp.exp(glast), (1, LANES))
        return carry

    lax.fori_loop(0, nsub, local, 0)

    def step(si, carry):
        chains = []
        for bi in range(nb):
            for di, o_ref in enumerate((of_ref, ob_ref)):
                ci = si if di == 0 else nsub - 1 - si
                rows = pl.ds(pl.multiple_of(ci * c, c), c)
                chains += [(bi, di * heads + h, h, ci, rows, o_ref) for h in range(heads)]
        ss = [s_ref[bi, ch] for bi, ch, _, _, _, _ in chains]
        ws = [_dot(wq_ref[bi, ch, ci], s.astype(BF16))
              for (bi, ch, _, ci, _, _), s in zip(chains, ss)]
        vn = [(u_ref[bi, ch, ci] - w[0:c]).astype(BF16) for (bi, ch, _, ci, _, _), w in zip(chains, ws)]
        outs = [w[c:2 * c] + _dot(qkd_ref[bi, ch, ci], v) for (bi, ch, _, ci, _, _), w, v in zip(chains, ws, vn)]
        sn = [cd_ref[bi, ch, ci] * s + _dot_tn(kt_ref[bi, ch, ci], v)
              for (bi, ch, _, ci, _, _), s, v in zip(chains, ss, vn)]
        for (bi, ch, h, ci, rows, o_ref), o, s in zip(chains, outs, sn):
            o_ref[bi, h, rows, :] = o.astype(o_ref.dtype)
            s_ref[bi, ch] = s
        return carry

    lax.fori_loop(0, nsub, step, 0)

    @pl.when(t == nt - 1)
    def _():
        sout_ref[...] = s_ref[...]


def _gdn_chunked(qkv_hm, gcol, s0, heads):
    bsz, _, s, _ = qkv_hm.shape
    c = GDN_CHUNK
    nsub = min(GDN_CHUNKS_PER_STEP, s // c)
    tb = nsub * c
    nt = s // tb
    grow = gcol[:, :, :4 * heads].reshape(bsz, s // c, c, 4 * heads).transpose(0, 1, 3, 2)
    grow = jnp.concatenate([grow, grow], axis=-1)
    nb = 2 if bsz % 2 == 0 else 1
    fwd = lambda b, t: (b, 0, t, 0)
    bwd = lambda b, t: (b, 0, nt - 1 - t, 0)
    nch = 2 * heads
    o_sds = jax.ShapeDtypeStruct((bsz, heads, s, HEAD_W), BF16)
    return pl.pallas_call(
        functools.partial(_gdn_chunk_kernel, heads=heads, nsub=nsub),
        grid=(bsz // nb, nt),
        in_specs=[pl.BlockSpec((nb, 3 * heads, tb, HEAD_W), fwd),
                  pl.BlockSpec((nb, 3 * heads, tb, HEAD_W), bwd),
                  pl.BlockSpec((nb, tb, LANES), lambda b, t: (b, t, 0)),
                  pl.BlockSpec((nb, tb, LANES), lambda b, t: (b, nt - 1 - t, 0)),
                  pl.BlockSpec((nb, nsub, 4 * heads, 2 * c), lambda b, t: (b, t, 0, 0)),
                  pl.BlockSpec((nb, nsub, 4 * heads, 2 * c), lambda b, t: (b, nt - 1 - t, 0, 0)),
                  pl.BlockSpec((nb, nch, HEAD_W, HEAD_W), lambda b, t: (b, 0, 0, 0))],
        out_specs=(pl.BlockSpec((nb, heads, tb, HEAD_W), fwd),
                   pl.BlockSpec((nb, heads, tb, HEAD_W), bwd),
                   pl.BlockSpec((nb, nch, HEAD_W, HEAD_W), lambda b, t: (b, 0, 0, 0))),
        out_shape=(o_sds, o_sds, jax.ShapeDtypeStruct(s0.shape, F32)),
        scratch_shapes=[pltpu.VMEM((nb, nch, HEAD_W, HEAD_W), F32),
                        pltpu.VMEM((nb, nch, nsub, c, HEAD_W), F32),
                        pltpu.VMEM((nb, nch, nsub, 2 * c, HEAD_W), BF16),
                        pltpu.VMEM((nb, nch, nsub, c, HEAD_W), BF16),
                        pltpu.VMEM((nb, nch, nsub, c, c), BF16),
                        pltpu.VMEM((nb, nch, nsub, 1, LANES), F32)],
        compiler_params=_cparams(2), name="gdn_chunk",
    )(qkv_hm, qkv_hm, gcol, gcol, grow, grow, s0)


def _route_rows(logits_t, rb):
    s = _sigmoid(logits_t)
    sel = s + rb
    rows = [sel[i:i + 1, :] for i in range(N_EXPERTS)]
    srows = [s[i:i + 1, :] for i in range(N_EXPERTS)]
    gscore = []
    for g in range(N_GROUPS):
        a, b, c, d = rows[GROUP_SIZE * g:GROUP_SIZE * (g + 1)]
        hi_ab, lo_ab = jnp.maximum(a, b), jnp.minimum(a, b)
        hi_cd, lo_cd = jnp.maximum(c, d), jnp.minimum(c, d)
        m1 = jnp.maximum(hi_ab, hi_cd)
        m2 = jnp.maximum(jnp.maximum(lo_ab, lo_cd), jnp.minimum(hi_ab, hi_cd))
        gscore.append(m1 + m2)
    best = gscore[0]
    bg = jnp.zeros_like(best)
    for g in range(1, N_GROUPS):
        upd = gscore[g] > best
        bg = jnp.where(upd, float(g), bg)
        best = jnp.where(upd, gscore[g], best)

    def pick(rws, j):
        out = rws[(N_GROUPS - 1) * GROUP_SIZE + j]
        for g in range(N_GROUPS - 2, -1, -1):
            out = jnp.where(bg == float(g), rws[g * GROUP_SIZE + j], out)
        return out

    v = [pick(rows, j) for j in range(GROUP_SIZE)]
    sv = [pick(srows, j) for j in range(GROUP_SIZE)]
    i1, b1, w1 = jnp.zeros_like(best), v[0], sv[0]
    for j in range(1, GROUP_SIZE):
        upd = v[j] > b1
        i1 = jnp.where(upd, float(j), i1)
        b1 = jnp.where(upd, v[j], b1)
        w1 = jnp.where(upd, sv[j], w1)
    i2 = b2 = w2 = None
    for j in range(GROUP_SIZE):
        ok = i1 != float(j)
        if b2 is None:
            i2 = jnp.where(ok, 0.0, 1.0)
            b2 = jnp.where(ok, v[0], v[1])
            w2 = jnp.where(ok, sv[0], sv[1])
            continue
        upd = ok & (v[j] > b2)
        i2 = jnp.where(upd, float(j), i2)
        b2 = jnp.where(upd, v[j], b2)
        w2 = jnp.where(upd, sv[j], w2)
    tot = w1 + w2
    return bg * GROUP_SIZE + i1, bg * GROUP_SIZE + i2, w1 / tot, w2 / tot


def _slot_ranks(e1, e2, base):
    tm = e1.shape[1]
    eid = lax.broadcasted_iota(jnp.int32, (N_EXPERTS, tm), 0).astype(F32)
    oh1 = jnp.where(eid == e1, 1.0, 0.0)
    oh2 = jnp.where(eid == e2, 1.0, 0.0)
    before = (lax.broadcasted_iota(jnp.int32, (tm, tm), 0) < lax.broadcasted_iota(jnp.int32, (tm, tm), 1))
    before = jnp.where(before, 1.0, 0.0).astype(BF16)
    pre = _dot(jnp.concatenate([oh1, oh2], axis=0).astype(BF16), before)
    n1 = jnp.sum(oh1, axis=1, keepdims=True)
    n2 = jnp.sum(oh2, axis=1, keepdims=True)
    r1 = jnp.sum(oh1 * (pre[0:N_EXPERTS] + base), axis=0, keepdims=True)
    r2 = jnp.sum(oh2 * (pre[N_EXPERTS:] + (base + n1)), axis=0, keepdims=True)
    return r1, r2, base + (n1 + n2)


def _out_proj_kernel(*refs, pro, heads):
    if pro == "gdn":
        of_ref, ob_ref, z_ref, gn_ref = refs[:4]
        rest = refs[4:]
        parts = []
        for h in range(heads):
            o = of_ref[0, h].astype(F32) + ob_ref[0, h].astype(F32)
            parts.append(_rms(o, gn_ref[...]))
        a = (jnp.concatenate(parts, axis=1) * _silu(z_ref[0].astype(F32))).astype(BF16)
    else:
        a = refs[0][0]
        rest = refs[1:]
    x_ref, mod_ref, w_ref, b_ref, g_ref, rw_ref, rb_ref, cnt0_ref, x1_ref, h2_ref, rt_ref, cnt_ref = rest

    @pl.when((pl.program_id(0) == 0) & (pl.program_id(1) == 0))
    def _():
        cnt_ref[...] = cnt0_ref[...]

    y = _dot(a, w_ref[...]) + b_ref[...]
    x1 = x_ref[0] + mod_ref[0, 2:3, :] * y
    x1_ref[0] = x1
    h2 = _norm_mod(x1, g_ref[...], mod_ref[0, 3:4, :], mod_ref[0, 4:5, :]).astype(BF16)
    h2_ref[0] = h2
    e1, e2, w1, w2 = _route_rows(_dot_nt(rw_ref[...], h2), rb_ref[...])
    r1, r2, cnt = _slot_ranks(e1, e2, cnt_ref[:, 0:1])
    zero = jnp.zeros_like(e1)
    rt_ref[0] = jnp.concatenate([e1, e2, w1, w2, r1, r2, zero, zero], axis=0)
    cnt_ref[...] = jnp.broadcast_to(cnt, cnt_ref.shape)


def _out_proj(pro, acts, x, mod, w, bias, norm_g, router_wt, router_b, cnt0, *, tm=512):
    g_, s, d = x.shape
    tm = _row_tile(s, tm)
    heads = d // HEAD_W
    full2 = lambda v: pl.BlockSpec(v.shape, lambda b, t: (0, 0))
    row3 = lambda width: pl.BlockSpec((1, tm, width), lambda b, t: (b, t, 0))
    if pro == "gdn":
        hm = pl.BlockSpec((1, heads, tm, HEAD_W), lambda b, t: (b, 0, t, 0))
        zcol = 3
        act_specs = [hm, hm, pl.BlockSpec((1, tm, d), lambda b, t: (b, t, zcol)), full2(acts[3])]
    else:
        act_specs = [row3(d)]
    in_specs = act_specs + [row3(d), pl.BlockSpec((1, 6, d), lambda b, t: (b, 0, 0)),
                            full2(w), full2(bias), full2(norm_g), full2(router_wt), full2(router_b), full2(cnt0)]
    return pl.pallas_call(
        functools.partial(_out_proj_kernel, pro=pro, heads=heads),
        grid=(g_, s // tm),
        in_specs=in_specs,
        out_specs=(row3(d), row3(d), pl.BlockSpec((1, 8, tm), lambda b, t: (b, 0, t)), full2(cnt0)),
        out_shape=(jax.ShapeDtypeStruct((g_, s, d), F32), jax.ShapeDtypeStruct((g_, s, d), BF16),
                   jax.ShapeDtypeStruct((g_, 8, s), F32), jax.ShapeDtypeStruct(cnt0.shape, F32)),
        compiler_params=_cparams(2), name="out_proj_" + pro,
    )(*acts, x, mod, w, bias, norm_g, router_wt, router_b, cnt0)


def _moe_ffn_kernel(te_ref, nv_ref, x_ref, wg_ref, wu_ref, wd_ref, o_ref):
    i = pl.program_id(0)

    @pl.when(i < nv_ref[0])
    def _():
        x = x_ref[...]
        a = (_silu(_dot(x, wg_ref[0, 0].astype(BF16))) * _dot(x, wu_ref[0, 0].astype(BF16))).astype(BF16)
        o_ref[...] = _dot(a, wd_ref[0, 0].astype(BF16)).astype(o_ref.dtype)

    @pl.when(i >= nv_ref[0])
    def _():
        o_ref[...] = jnp.zeros_like(o_ref)


def _moe_ffn(xs, tile_expert, n_valid, wg, wu, wd, layer, *, tm):
    r, d = xs.shape
    f = wg.shape[-1]
    return pl.pallas_call(
        _moe_ffn_kernel,
        grid_spec=pltpu.PrefetchScalarGridSpec(
            num_scalar_prefetch=2, grid=(r // tm,),
            in_specs=[pl.BlockSpec((tm, d), lambda i, te, nv: (i, 0)),
                      pl.BlockSpec((1, 1, d, f), lambda i, te, nv: (layer, te[i], 0, 0)),
                      pl.BlockSpec((1, 1, d, f), lambda i, te, nv: (layer, te[i], 0, 0)),
                      pl.BlockSpec((1, 1, f, d), lambda i, te, nv: (layer, te[i], 0, 0))],
            out_specs=pl.BlockSpec((tm, d), lambda i, te, nv: (i, 0))),
        out_shape=jax.ShapeDtypeStruct((r, d), BF16),
        compiler_params=_cparams(1), name="moe_ffn",
    )(tile_expert, n_valid, xs, wg, wu, wd)


def _combine_kernel(x_ref, y0_ref, y1_ref, rt_ref, mod_ref, g_ref, o_ref, *, final):
    rt = rt_ref[0]
    y = rt[:, 2:3] * y0_ref[0].astype(F32) + rt[:, 3:4] * y1_ref[0].astype(F32)
    x2 = x_ref[0] + mod_ref[0, 5:6, :] * y
    o_ref[0] = _rms(x2, g_ref[...]) if final else x2


def _combine(x1, y0, y1, route_t, mod, final_g, *, final, tm=512):
    g_, s, d = x1.shape
    tm = _row_tile(s, tm)
    row3 = pl.BlockSpec((1, tm, d), lambda b, t: (b, t, 0))
    return pl.pallas_call(
        functools.partial(_combine_kernel, final=final),
        grid=(g_, s // tm),
        in_specs=[row3, row3, row3,
                  pl.BlockSpec((1, tm, 8), lambda b, t: (b, t, 0)),
                  pl.BlockSpec((1, 6, d), lambda b, t: (b, 0, 0)),
                  pl.BlockSpec(final_g.shape, lambda b, t: (0, 0))],
        out_specs=row3,
        out_shape=jax.ShapeDtypeStruct((g_, s, d), F32),
        compiler_params=_cparams(2), name="moe_combine",
    )(x1, y0, y1, route_t, mod, final_g)


def _moe(h2_list, route_list, counts, wg, wu, wd, layer, *, tm=512):
    d = h2_list[0].shape[-1]
    h2 = jnp.concatenate([h.reshape(-1, d) for h in h2_list], axis=0)
    n = h2.shape[0]
    cnt = counts[:, 0].astype(jnp.int32)
    padded = ((cnt + tm - 1) // tm) * tm
    ends = jnp.cumsum(padded)
    starts = (ends - padded).astype(F32)
    pos_list = []
    for r in route_list:
        e, rank = r[:, 0:2, :], r[:, 4:6, :]
        first = jnp.zeros_like(e)
        for k in range(N_EXPERTS):
            first = jnp.where(e == float(k), starts[k], first)
        pos_list.append((first + rank).astype(jnp.int32))
    pos_all = jnp.concatenate([p.transpose(1, 0, 2).reshape(2, -1) for p in pos_list], axis=1)
    n_tiles = (2 * n) // tm + N_EXPERTS
    tok = jnp.arange(n, dtype=jnp.int32)
    filler = jnp.arange(n_tiles * tm, dtype=jnp.int32) % n
    src = filler.at[pos_all.reshape(-1)].set(jnp.concatenate([tok, tok]))
    n_valid = (ends[-1] // tm).astype(jnp.int32)
    tile_start = jnp.minimum(jnp.arange(n_tiles, dtype=jnp.int32), n_valid - 1) * tm
    tile_expert = jnp.sum((tile_start[:, None] >= ends[None, :]).astype(jnp.int32), axis=1)
    tile_expert = jnp.minimum(tile_expert, N_EXPERTS - 1)
    xs = jnp.take(h2, src, axis=0)
    ys = _moe_ffn(xs, tile_expert, n_valid.reshape(1), wg, wu, wd, layer, tm=tm)
    outs = []
    for h, r, p in zip(h2_list, route_list, pos_list):
        y0 = jnp.take(ys, p[:, 0, :].reshape(-1), axis=0).reshape(h.shape)
        y1 = jnp.take(ys, p[:, 1, :].reshape(-1), axis=0).reshape(h.shape)
        outs.append((y0, y1, r.transpose(0, 2, 1)))
    return outs


def _rope_tables(seqlen):
    pos = jnp.arange(seqlen, dtype=jnp.int32)
    row = (pos // GRID_W).astype(F32)
    col = (pos % GRID_W).astype(F32)
    inv_freq = ROPE_THETA ** (-jnp.arange(ROPE_FREQS, dtype=F32) / ROPE_FREQS)
    lane = jnp.arange(LANES)
    within = lane % DA_HEAD_DIM
    axis = within // (2 * ROPE_FREQS)
    half = (within % (2 * ROPE_FREQS)) // ROPE_FREQS
    ang = jnp.where(axis[None, :] == 0, row[:, None], col[:, None]) * inv_freq[within % ROPE_FREQS][None, :]
    sign = jnp.where(half == 0, -1.0, 1.0).astype(F32)
    return jnp.cos(ang), jnp.sin(ang) * sign[None, :]


def kernel(x, c, ctx, c_ctx, ada_w, ada_b, norm_mix_g, norm_ffn_g, final_norm_g, conv_pw1_w, conv_pw1_b, conv_dw_w, conv_dw_b, conv_ln_g, conv_ln_b, conv_pw2_w, conv_pw2_b, diff_w_qkv, diff_lambda, diff_subln_g, diff_w_o, gdn_w_in, gdn_conv_w, gdn_a_log, gdn_dt_bias, gdn_norm_g, gdn_w_o, router_w, router_b, moe_w_gate, moe_w_up, moe_w_down):
    bsz, seqlen, d = x.shape
    n_ctx = ctx.shape[1]
    depth = ada_w.shape[0]
    heads = d // HEAD_W
    row = lambda v: v.reshape(1, -1).astype(F32)

    mods = _ada_all(jnp.concatenate([c, c_ctx[None, :]], axis=0), ada_w, ada_b)
    mods = mods.reshape(depth, bsz + 1, 6, d)
    router_wt = router_w.T.astype(BF16)
    router_bc = router_b.reshape(N_EXPERTS, 1).astype(F32)
    cos_t, sin_t = _rope_tables(seqlen)
    zero_bias = jnp.zeros((1, d), F32)

    def layer(i, lat, cx, mods_i, w):
        gb = lat.shape[0]
        last = i == depth - 1
        kind, j = i % N_MIXERS, i // N_MIXERS
        need_ctx = (not last) or kind != 0
        m_lat, m_ctx = mods_i
        gmix = row(norm_mix_g[i])
        streams = [(lat, m_lat)] + ([(cx, m_ctx)] if need_ctx else [])
        acts = []
        if kind == 0:
            for xs, ms in streams:
                a = _in_proj("glu", xs, ms, gmix, [w["w1"], row(conv_pw1_b[j])])
                shp = a.shape
                a = a.reshape(gb, -1, d)
                a = _dwconv_ln_silu(a, conv_dw_w[j].astype(F32), row(conv_dw_b[j]),
                                    row(conv_ln_g[j]), row(conv_ln_b[j]))
                acts.append(("plain", [a.reshape(shp)]))
        elif kind == 1:
            lam_init = 0.8 - 0.6 * math.exp(-0.3 * i)
            qscale = DA_HEAD_DIM ** -0.5 * math.log2(math.e)
            qkv_l = _in_proj("qkv", lat, m_lat, gmix, [w["wq"]], (cos_t, sin_t), rope=True, qscale=qscale)
            qkv_c = _in_proj("qkv", cx, m_ctx, gmix, [w["wq"]], qscale=qscale).reshape(gb, n_ctx, 3 * d)
            lam_v, sub_g = diff_lambda[j].astype(F32), row(diff_subln_g[j])
            o_l = _diff_attn_core(qkv_l, [qkv_c, qkv_l], lam_v, sub_g, lam_init)
            acts.append(("plain", [o_l]))
            if not last:
                o_c = _diff_attn_core(qkv_c, [qkv_c], lam_v, sub_g, lam_init)
                acts.append(("plain", [o_c.reshape(cx.shape)]))
        else:
            conv_w = gdn_conv_w[j].astype(F32)
            zz_l, ba_l = _in_proj("gdn", lat, m_lat, gmix, [w["w_main"], w["w_ba"]])
            zz_c, ba_c = _in_proj("gdn", cx, m_ctx, gmix, [w["w_main"], w["w_ba"]])
            zz_c = zz_c.reshape(gb, n_ctx, -1)
            ba_c = ba_c.reshape(gb, n_ctx, LANES)
            s0 = jnp.zeros((gb, 2 * heads, HEAD_W, HEAD_W), F32)
            ocf, ocb, s1 = _gdn_chunked(_gdn_prep(zz_c, conv_w),
                                        _gdn_gates(ba_c, gdn_a_log[j], gdn_dt_bias[j], heads), s0, heads)
            olf, olb, _ = _gdn_chunked(_gdn_prep(zz_l, conv_w),
                                       _gdn_gates(ba_l, gdn_a_log[j], gdn_dt_bias[j], heads), s1, heads)
            gn = row(gdn_norm_g[j])
            acts.append(("gdn", [olf, olb, zz_l, gn]))
            if not last:
                acts.append(("gdn", [ocf, ocb, zz_c, gn]))

        gffn = row(norm_ffn_g[i])
        x1s, h2s, routes = [], [], []
        counts = jnp.zeros((N_EXPERTS, LANES), F32)
        for (xs, ms), (pro, a) in zip(streams, acts):
            if pro == "gdn" and xs.shape[0] != a[0].shape[0]:
                xin = xs.reshape(gb, -1, d)
                msb = jnp.broadcast_to(ms, (gb, 6, d))
                x1, h2, rt, counts = _out_proj(pro, a, xin, msb, w["w_out"], w["b_out"], gffn,
                                               router_wt, router_bc, counts)
                x1, h2 = x1.reshape(xs.shape), h2.reshape(xs.shape)
                rt = rt.transpose(1, 0, 2).reshape(1, 8, -1)
            else:
                x1, h2, rt, counts = _out_proj(pro, a, xs, ms, w["w_out"], w["b_out"], gffn,
                                               router_wt, router_bc, counts)
            x1s.append(x1)
            h2s.append(h2)
            routes.append(rt)
        moe_out = _moe(h2s, routes, counts, moe_w_gate, moe_w_up, moe_w_down, i)
        fg = row(final_norm_g)
        lat = _combine(x1s[0], *moe_out[0], m_lat, fg, final=last)
        if not last:
            cx = _combine(x1s[1], *moe_out[1], m_ctx, fg, final=False)
        return lat, cx

    def layer_weights(i):
        kind, j = i % N_MIXERS, i // N_MIXERS
        if kind == 0:
            return dict(w1=conv_pw1_w[j].astype(BF16), w_out=conv_pw2_w[j].astype(BF16), b_out=row(conv_pw2_b[j]))
        if kind == 1:
            return dict(wq=diff_w_qkv[j].astype(BF16), w_out=diff_w_o[j].astype(BF16), b_out=zero_bias)
        cch = gdn_conv_w.shape[-1]
        w_in = gdn_w_in[j]
        w_ba = jnp.zeros((d, LANES), BF16).at[:, :4 * heads].set(w_in[:, cch + d:].astype(BF16))
        return dict(w_main=w_in[:, :cch + d].astype(BF16), w_ba=w_ba,
                    w_out=gdn_w_o[j].astype(BF16), b_out=zero_bias)

    lat = x
    cx = ctx.reshape(1, bsz * n_ctx, d)
    for i in range(depth):
        lat, cx = layer(i, lat, cx, (mods[i, :bsz], mods[i, bsz:]), layer_weights(i))
    return lat
```

```python
import functools
import math

import jax
import jax.numpy as jnp
from jax import lax
from jax.experimental import pallas as pl
from jax.experimental.pallas import tpu as pltpu

F32 = jnp.float32
BF16 = jnp.bfloat16
HIGHEST = lax.Precision.HIGHEST

NORM_EPS = 1e-6
N_MIXERS = 3
GRID_W = 64
ROPE_THETA = 10000.0
DA_HEAD_DIM = 64
ROPE_FREQS = DA_HEAD_DIM // 4
HEAD_W = 128
GDN_CHUNK = 64
GDN_CHUNKS_PER_STEP = 4
N_EXPERTS = 16
N_GROUPS = 4
GROUP_SIZE = N_EXPERTS // N_GROUPS
LANES = 128
SUBLANES = 8
CONV_HALO = 16
VMEM_LIMIT = 56 * 1024 * 1024


def _cparams(n_axes):
    return pltpu.CompilerParams(dimension_semantics=("arbitrary",) * n_axes,
                                vmem_limit_bytes=VMEM_LIMIT)


def _dot(a, b, **kw):
    return jnp.dot(a, b, preferred_element_type=F32, **kw)


def _dot_nt(a, b, **kw):
    return lax.dot_general(a, b, (((1,), (1,)), ((), ())), preferred_element_type=F32, **kw)


def _dot_tn(a, b, **kw):
    return lax.dot_general(a, b, (((0,), (0,)), ((), ())), preferred_element_type=F32, **kw)


def _sigmoid(x):
    return 1.0 / (1.0 + jnp.exp(-x))


def _silu(x):
    return x * _sigmoid(x)


def _rms(x, g):
    return x * lax.rsqrt(jnp.mean(x * x, axis=-1, keepdims=True) + NORM_EPS) * g


def _norm_mod(x, g, shift, scale):
    return _rms(x, g) * (1.0 + scale) + shift


def _row_tile(n, want):
    t = min(n, want)
    assert n % t == 0, (n, t)
    return t


def _ada_kernel(c_ref, w_ref, b_ref, o_ref):
    s = _silu(c_ref[...]).astype(BF16)
    o_ref[0] = _dot(s, w_ref[0].astype(BF16)) + b_ref[0]


def _ada_all(cvec, ada_w, ada_b):
    depth, d, n = ada_w.shape
    r = cvec.shape[0]
    tn = d
    return pl.pallas_call(
        _ada_kernel,
        grid=(depth, n // tn),
        in_specs=[pl.BlockSpec((r, d), lambda i, j: (0, 0)),
                  pl.BlockSpec((1, d, tn), lambda i, j: (i, 0, j)),
                  pl.BlockSpec((1, 1, tn), lambda i, j: (i, 0, j))],
        out_specs=pl.BlockSpec((1, r, tn), lambda i, j: (i, 0, j)),
        out_shape=jax.ShapeDtypeStruct((depth, r, n), F32),
        compiler_params=_cparams(2),
        name="adaln",
    )(cvec, ada_w, ada_b.reshape(depth, 1, n))


def _glu_kernel(x_ref, mod_ref, g_ref, w_ref, b_ref, o_ref, *, tn):
    h = _norm_mod(x_ref[0], g_ref[...], mod_ref[0, 0:1, :], mod_ref[0, 1:2, :]).astype(BF16)
    d = o_ref.shape[-1]
    for j in range(0, d, tn):
        a = _dot(h, w_ref[:, j:j + tn]) + b_ref[:, j:j + tn]
        gt = _dot(h, w_ref[:, d + j:d + j + tn]) + b_ref[:, d + j:d + j + tn]
        o_ref[0, :, j:j + tn] = (a * _sigmoid(gt)).astype(o_ref.dtype)


def _rope_lanes(t, cos, sin_signed, first_half):
    swapped = jnp.where(first_half, pltpu.roll(t, LANES - ROPE_FREQS, 1), pltpu.roll(t, ROPE_FREQS, 1))
    return t * cos + swapped * sin_signed


def _qkv_kernel(*refs, tn, rope, qscale):
    if rope:
        x_ref, mod_ref, g_ref, w_ref, cos_ref, sin_ref, o_ref = refs
    else:
        x_ref, mod_ref, g_ref, w_ref, o_ref = refs
    h = _norm_mod(x_ref[0], g_ref[...], mod_ref[0, 0:1, :], mod_ref[0, 1:2, :]).astype(BF16)
    d = x_ref.shape[-1]
    if rope:
        cos = cos_ref[...]
        sin = sin_ref[...]
        lane = lax.broadcasted_iota(jnp.int32, cos.shape, 1)
        first_half = (lane % (2 * ROPE_FREQS)) < ROPE_FREQS
    for j in range(0, 3 * d, tn):
        y = _dot(h, w_ref[:, j:j + tn])
        if j < 2 * d and rope:
            y = jnp.concatenate(
                [_rope_lanes(y[:, s:s + LANES], cos, sin, first_half) for s in range(0, tn, LANES)], axis=1)
        if j < d:
            y = y * qscale
        o_ref[0, :, j:j + tn] = y.astype(o_ref.dtype)


def _gdn_in_kernel(x_ref, mod_ref, g_ref, w_ref, wba_ref, zz_ref, ba_ref, *, tn):
    h = _norm_mod(x_ref[0], g_ref[...], mod_ref[0, 0:1, :], mod_ref[0, 1:2, :]).astype(BF16)
    for j in range(0, zz_ref.shape[-1], tn):
        zz_ref[0, :, j:j + tn] = _dot(h, w_ref[:, j:j + tn]).astype(zz_ref.dtype)
    ba_ref[0] = _dot(h, wba_ref[...])


def _in_proj(kind, x, mod, norm_g, weights, extra=(), *, tm=512, tn=512, rope=False, qscale=1.0):
    g_, s, d = x.shape
    tm = _row_tile(s, tm)
    grid = (g_, s // tm)
    full2 = lambda a: pl.BlockSpec(a.shape, lambda b, t: (0, 0))
    in_specs = [pl.BlockSpec((1, tm, d), lambda b, t: (b, t, 0)),
                pl.BlockSpec((1, 6, d), lambda b, t: (b, 0, 0)),
                full2(norm_g)] + [full2(w) for w in weights]
    args = [x, mod, norm_g] + list(weights)
    if kind == "glu":
        body = functools.partial(_glu_kernel, tn=tn)
        out_shape = jax.ShapeDtypeStruct((g_, s, d), BF16)
        out_specs = pl.BlockSpec((1, tm, d), lambda b, t: (b, t, 0))
    elif kind == "qkv":
        body = functools.partial(_qkv_kernel, tn=tn, rope=rope, qscale=qscale)
        if rope:
            in_specs += [pl.BlockSpec((tm, LANES), lambda b, t: (t, 0))] * 2
            args += list(extra)
        out_shape = jax.ShapeDtypeStruct((g_, s, 3 * d), BF16)
        out_specs = pl.BlockSpec((1, tm, 3 * d), lambda b, t: (b, t, 0))
    else:
        body = functools.partial(_gdn_in_kernel, tn=tn)
        nz = weights[0].shape[1]
        out_shape = (jax.ShapeDtypeStruct((g_, s, nz), BF16), jax.ShapeDtypeStruct((g_, s, LANES), F32))
        out_specs = (pl.BlockSpec((1, tm, nz), lambda b, t: (b, t, 0)),
                     pl.BlockSpec((1, tm, LANES), lambda b, t: (b, t, 0)))
    return pl.pallas_call(body, grid=grid, in_specs=in_specs, out_specs=out_specs, out_shape=out_shape,
                          compiler_params=_cparams(2), name="in_proj_" + kind)(*args)


def _dwconv_kernel(a_ref, prev_ref, next_ref, w_ref, b_ref, lg_ref, lb_ref, o_ref, buf_ref, acc_ref,
                   *, kw, rows, cols):
    t = pl.program_id(1)
    nt = pl.num_programs(1)
    tm, d = a_ref.shape[1], a_ref.shape[2]
    halo = CONV_HALO
    span = tm + 2 * halo
    buf_ref[0, 0:halo, :] = jnp.where(t > 0, prev_ref[0].astype(F32), 0.0)
    buf_ref[0, halo:halo + tm, :] = a_ref[0].astype(F32)
    buf_ref[0, halo + tm:span, :] = jnp.where(t < nt - 1, next_ref[0].astype(F32), 0.0)
    buf_ref[0, span:, :] = jnp.zeros((SUBLANES, d), F32)
    for s in range(1, SUBLANES):
        buf_ref[s, 0:span, :] = buf_ref[0, s:s + span, :]
    base = halo - kw // 2
    for r0 in range(0, tm, rows):
        for c0 in range(0, d, cols):
            acc = jnp.zeros((rows, cols), F32)
            for k in range(kw):
                off = base + k + r0
                s = off % SUBLANES
                acc = acc + w_ref[k:k + 1, c0:c0 + cols] * buf_ref[s, off - s:off - s + rows, c0:c0 + cols]
            acc_ref[r0:r0 + rows, c0:c0 + cols] = acc
    y = acc_ref[...] + b_ref[...]
    yc = y - jnp.mean(y, axis=-1, keepdims=True)
    yn = yc * lax.rsqrt(jnp.mean(yc * yc, axis=-1, keepdims=True) + NORM_EPS) * lg_ref[...] + lb_ref[...]
    o_ref[0] = _silu(yn).astype(o_ref.dtype)


def _dwconv_ln_silu(a, dw_w, dw_b, ln_g, ln_b, *, tm=256):
    bsz, s, d = a.shape
    kw = dw_w.shape[0]
    assert kw // 2 < CONV_HALO
    tm = _row_tile(s, tm)
    hb = tm // CONV_HALO
    nhb = s // CONV_HALO
    full2 = lambda v: pl.BlockSpec(v.shape, lambda b, t: (0, 0))
    return pl.pallas_call(
        functools.partial(_dwconv_kernel, kw=kw, rows=64, cols=256),
        grid=(bsz, s // tm),
        in_specs=[pl.BlockSpec((1, tm, d), lambda b, t: (b, t, 0)),
                  pl.BlockSpec((1, CONV_HALO, d), lambda b, t: (b, jnp.maximum(t * hb - 1, 0), 0)),
                  pl.BlockSpec((1, CONV_HALO, d), lambda b, t: (b, jnp.minimum((t + 1) * hb, nhb - 1), 0)),
                  full2(dw_w), full2(dw_b), full2(ln_g), full2(ln_b)],
        out_specs=pl.BlockSpec((1, tm, d), lambda b, t: (b, t, 0)),
        out_shape=jax.ShapeDtypeStruct((bsz, s, d), BF16),
        scratch_shapes=[pltpu.VMEM((SUBLANES, tm + 2 * CONV_HALO + SUBLANES, d), F32), pltpu.VMEM((tm, d), F32)],
        compiler_params=_cparams(2), name="dwconv_ln_silu",
    )(a, a, a, dw_w, dw_b, ln_g, ln_b)


def _attn_kernel(*refs, n_kv, lam_init, hps):
    lam_ref, sg_ref, q_ref = refs[:3]
    k_refs = refs[3:3 + n_kv]
    v_refs = refs[3 + n_kv:3 + 2 * n_kv]
    o_ref = refs[3 + 2 * n_kv]
    lv = lam_ref[...]
    lam = (jnp.exp(jnp.sum(lv[0:1] * lv[1:2], axis=-1, keepdims=True))
           - jnp.exp(jnp.sum(lv[2:3] * lv[3:4], axis=-1, keepdims=True)) + lam_init)
    lane = lax.broadcasted_iota(jnp.int32, (q_ref.shape[1], HEAD_W), 1)
    scores = []
    for h in range(hps):
        hl = slice(h * HEAD_W, (h + 1) * HEAD_W)
        q = q_ref[0, :, hl]
        for c in range(2):
            in_map = (lane < DA_HEAD_DIM) if c == 0 else (lane >= DA_HEAD_DIM)
            qc = jnp.where(in_map, q, jnp.zeros_like(q))
            scores.append([_dot_nt(qc, k_ref[0, :, hl]) for k_ref in k_refs])
    for h in range(hps):
        hl = slice(h * HEAD_W, (h + 1) * HEAD_W)
        probs, coef = [], []
        for c in range(2):
            ss = scores[2 * h + c]
            m = ss[0].max(axis=-1, keepdims=True)
            for s_ in ss[1:]:
                m = jnp.maximum(m, s_.max(axis=-1, keepdims=True))
            ps = [jnp.exp2(s_ - m) for s_ in ss]
            l = ps[0].sum(axis=-1, keepdims=True)
            for p in ps[1:]:
                l = l + p.sum(axis=-1, keepdims=True)
            probs.append(ps)
            coef.append((1.0 if c == 0 else lam) / l)
        o = None
        for p0, p1, v_ref in zip(probs[0], probs[1], v_refs):
            pv = _dot((p0 * coef[0] - p1 * coef[1]).astype(BF16), v_ref[0, :, hl])
            o = pv if o is None else o + pv
        o = _rms(o, sg_ref[...]) * (1.0 - lam_init)
        o_ref[0, :, hl] = o.astype(o_ref.dtype)


def _diff_attn_core(q_src, kv_srcs, lam_vecs, subln_g, lam_init, *, tq=512, hps=2):
    bsz, sq, d3 = q_src.shape
    d = d3 // 3
    nh = d // HEAD_W
    assert nh % hps == 0
    nhb = nh // hps
    wb = hps * HEAD_W
    tq = _row_tile(sq, tq)
    n_kv = len(kv_srcs)
    in_specs = [pl.BlockSpec(lam_vecs.shape, lambda b, h, t: (0, 0)),
                pl.BlockSpec(subln_g.shape, lambda b, h, t: (0, 0)),
                pl.BlockSpec((1, tq, wb), lambda b, h, t: (b, t, h))]
    in_specs += [pl.BlockSpec((1, s.shape[1], wb), lambda b, h, t: (b, 0, nhb + h)) for s in kv_srcs]
    in_specs += [pl.BlockSpec((1, s.shape[1], wb), lambda b, h, t: (b, 0, 2 * nhb + h)) for s in kv_srcs]
    return pl.pallas_call(
        functools.partial(_attn_kernel, n_kv=n_kv, lam_init=lam_init, hps=hps),
        grid=(bsz, nhb, sq // tq),
        in_specs=in_specs,
        out_specs=pl.BlockSpec((1, tq, wb), lambda b, h, t: (b, t, h)),
        out_shape=jax.ShapeDtypeStruct((bsz, sq, d), BF16),
        compiler_params=_cparams(3), name="diff_attn",
    )(lam_vecs, subln_g, q_src, *kv_srcs, *kv_srcs)


def _gdn_prep_kernel(zz_ref, w_ref, o_ref, buf_ref, *, kw, heads, qk_scale):
    s = zz_ref.shape[1]
    width = heads * HEAD_W
    pad = SUBLANES
    rows = min(s, 256)
    buf_ref[0:pad, :] = jnp.zeros((pad, width), F32)
    buf_ref[pad:pad + s, :] = zz_ref[0].astype(F32)
    buf_ref[pad + s:, :] = jnp.zeros((pad, width), F32)
    kind = pl.program_id(1)
    for h in range(heads):
        lanes = slice(h * HEAD_W, (h + 1) * HEAD_W)
        for r0 in range(0, s, rows):
            acc = jnp.zeros((rows, HEAD_W), F32)
            for k in range(kw):
                off = pad + k - kw // 2 + r0
                acc = acc + w_ref[k:k + 1, lanes] * buf_ref[off:off + rows, lanes]
            y = _silu(acc)
            nrm = lax.rsqrt(jnp.sum(y * y, axis=-1, keepdims=True) + NORM_EPS)
            fac = jnp.where(kind == 0, nrm * qk_scale, jnp.where(kind == 1, nrm, jnp.ones_like(nrm)))
            o_ref[0, h, r0:r0 + rows, :] = (y * fac).astype(o_ref.dtype)


def _gdn_prep(zz, conv_w):
    bsz, s, _ = zz.shape
    kw, cch = conv_w.shape
    heads = cch // (3 * HEAD_W)
    width = heads * HEAD_W
    return pl.pallas_call(
        functools.partial(_gdn_prep_kernel, kw=kw, heads=heads, qk_scale=HEAD_W ** -0.5),
        grid=(bsz, 3),
        in_specs=[pl.BlockSpec((1, s, width), lambda b, j: (b, 0, j)),
                  pl.BlockSpec((kw, width), lambda b, j: (0, j))],
        out_specs=pl.BlockSpec((1, heads, s, HEAD_W), lambda b, j: (b, j, 0, 0)),
        out_shape=jax.ShapeDtypeStruct((bsz, 3 * heads, s, HEAD_W), BF16),
        scratch_shapes=[pltpu.VMEM((s + 2 * SUBLANES, width), F32)],
        compiler_params=_cparams(2), name="gdn_prep",
    )(zz, conv_w)


def _gdn_gates_kernel(ba_ref, alog_ref, dtb_ref, o_ref, *, heads):
    s = ba_ref.shape[1]
    c = GDN_CHUNK
    ii = lax.broadcasted_iota(jnp.int32, (c, c), 0)
    jj = lax.broadcasted_iota(jnp.int32, (c, c), 1)
    tri_f = (ii >= jj).astype(F32)
    tri_b = (ii <= jj).astype(F32)
    lane = lax.broadcasted_iota(jnp.int32, (c, LANES), 1)
    for r0 in range(0, s, c):
        x = ba_ref[0, r0:r0 + c, :]
        beta = _sigmoid(x)
        z = x + dtb_ref[...]
        softplus = jnp.maximum(z, 0.0) + jnp.log1p(jnp.exp(-jnp.abs(z)))
        g = -jnp.exp(alog_ref[...]) * softplus
        gf = _dot(tri_f, g, precision=HIGHEST)
        gb = _dot(tri_b, g, precision=HIGHEST)
        gsum = jnp.where(lane >= 3 * heads, gb, gf)
        o_ref[0, r0:r0 + c, :] = jnp.where(lane < 2 * heads, beta, gsum)


def _gdn_gates(ba, a_log, dt_bias, heads):
    bsz, s, _ = ba.shape
    pad = lambda v: jnp.zeros((1, LANES), F32).at[0, 2 * heads:4 * heads].set(v.reshape(-1).astype(F32))
    return pl.pallas_call(
        functools.partial(_gdn_gates_kernel, heads=heads),
        grid=(bsz,),
        in_specs=[pl.BlockSpec((1, s, LANES), lambda b: (b, 0, 0)),
                  pl.BlockSpec((1, LANES), lambda b: (0, 0)),
                  pl.BlockSpec((1, LANES), lambda b: (0, 0))],
        out_specs=pl.BlockSpec((1, s, LANES), lambda b: (b, 0, 0)),
        out_shape=jax.ShapeDtypeStruct((bsz, s, LANES), F32),
        compiler_params=_cparams(1), name="gdn_gates",
    )(ba, pad(a_log), pad(dt_bias))


def _inv_dot(a, b):
    return _dot(a.astype(BF16), b.astype(BF16))


def _gdn_chunk_kernel(qkv_f, qkv_b, gc_f, gc_b, gr_f, gr_b, s0_ref, of_ref, ob_ref, sout_ref,
                      s_ref, u_ref, wq_ref, kt_ref, qkd_ref, cd_ref, *, heads, nsub):
    t = pl.program_id(1)
    nt = pl.num_programs(1)
    c = GDN_CHUNK
    nb = s0_ref.shape[0]
    nch = 2 * heads

    @pl.when(t == 0)
    def _():
        s_ref[...] = s0_ref[...]

    ii = lax.broadcasted_iota(jnp.int32, (c, 2 * c), 0)
    lane = lax.broadcasted_iota(jnp.int32, (c, 2 * c), 1)
    jj = lane % c
    left = lane < c
    eye_right = (ii == jj) & (lane >= c)
    dirs = ((qkv_f, gc_f, gr_f, ii >= jj, ii > jj, c - 1), (qkv_b, gc_b, gr_b, ii <= jj, ii < jj, 0))

    def local(ci, carry):
        rows = pl.ds(pl.multiple_of(ci * c, c), c)
        chains = []
        for bi in range(nb):
            for di, (qkv, gcr, grr, incl, strict, last) in enumerate(dirs):
                gc = gcr[bi, rows, :]
                gr = grr[bi, ci]
                for h in range(heads):
                    ch = di * heads + h
                    chains.append(dict(
                        bi=bi, ch=ch, incl=incl, strict=strict, last=last,
                        q=qkv[bi, h, rows, :], k=qkv[bi, heads + h, rows, :], v=qkv[bi, 2 * heads + h, rows, :],
                        beta=gc[:, ch:ch + 1],
                        gcol=gc[:, 2 * heads + ch:2 * heads + ch + 1],
                        grow=gr[2 * heads + ch:2 * heads + ch + 1, :]))
        kk = [_dot_nt(a["k"], jnp.concatenate([a["k"], a["k"]], axis=0)) for a in chains]
        qk = [_dot_nt(a["q"], a["k"]) for a in chains]
        dec = [jnp.where(a["incl"], jnp.exp(jnp.where(a["incl"], a["gcol"] - a["grow"], 0.0)), 0.0)
               for a in chains]
        zs = [jnp.where(left, -jnp.where(a["strict"], a["beta"] * kki * dci, 0.0), jnp.where(eye_right, 1.0, 0.0))
              for a, kki, dci in zip(chains, kk, dec)]
        for _ in range(int(math.log2(c))):
            zs = [_inv_dot(z[:, 0:c], z) + jnp.where(left, 0.0, z) for z in zs]
        egs = [jnp.exp(a["gcol"]) for a in chains]
        k32 = [a["k"].astype(F32) for a in chains]
        rhs = [jnp.concatenate([a["v"].astype(F32) * a["beta"], kf * (a["beta"] * eg)], axis=1)
               for a, kf, eg in zip(chains, k32, egs)]
        sols = [_inv_dot(z[:, c:2 * c], r) for z, r in zip(zs, rhs)]
        for a, sol, kf, eg, qki, dci in zip(chains, sols, k32, egs, qk, dec):
            bi, ch = a["bi"], a["ch"]
            glast = a["gcol"][a["last"]:a["last"] + 1, :]
            u_ref[bi, ch, ci] = sol[:, :HEAD_W]
            wq_ref[bi, ch, ci, 0:c, :] = sol[:, HEAD_W:].astype(BF16)
            wq_ref[bi, ch, ci, c:2 * c, :] = (a["q"].astype(F32) * eg).astype(BF16)
            kt_ref[bi, ch, ci] = (kf * jnp.exp(glast - a["gcol"])).astype(BF16)
            qkd_ref[bi, ch, ci] = (qki * dci[:, 0:c]).astype(BF16)
            cd_ref[bi, ch, ci] = jnp.broadcast_to(jnp.exp(glast), (1, LANES))
        return carry

    lax.fori_loop(0, nsub, local, 0)

    def step(si, carry):
        chains = []
        for bi in range(nb):
            for di, o_ref in enumerate((of_ref, ob_ref)):
                ci = si if di == 0 else nsub - 1 - si
                rows = pl.ds(pl.multiple_of(ci * c, c), c)
                chains += [(bi, di * heads + h, h, ci, rows, o_ref) for h in range(heads)]
        ss = [s_ref[bi, ch] for bi, ch, _, _, _, _ in chains]
        ws = [_dot(wq_ref[bi, ch, ci], s.astype(BF16))
              for (bi, ch, _, ci, _, _), s in zip(chains, ss)]
        vn = [(u_ref[bi, ch, ci] - w[0:c]).astype(BF16) for (bi, ch, _, ci, _, _), w in zip(chains, ws)]
        outs = [w[c:2 * c] + _dot(qkd_ref[bi, ch, ci], v) for (bi, ch, _, ci, _, _), w, v in zip(chains, ws, vn)]
        sn = [cd_ref[bi, ch, ci] * s + _dot_tn(kt_ref[bi, ch, ci], v)
              for (bi, ch, _, ci, _, _), s, v in zip(chains, ss, vn)]
        for (bi, ch, h, ci, rows, o_ref), o, s in zip(chains, outs, sn):
            o_ref[bi, h, rows, :] = o.astype(o_ref.dtype)
            s_ref[bi, ch] = s
        return carry

    lax.fori_loop(0, nsub, step, 0)

    @pl.when(t == nt - 1)
    def _():
        sout_ref[...] = s_ref[...]


def _gdn_chunked(qkv_hm, gcol, s0, heads):
    bsz, _, s, _ = qkv_hm.shape
    c = GDN_CHUNK
    nsub = min(GDN_CHUNKS_PER_STEP, s // c)
    tb = nsub * c
    nt = s // tb
    grow = gcol[:, :, :4 * heads].reshape(bsz, s // c, c, 4 * heads).transpose(0, 1, 3, 2)
    grow = jnp.concatenate([grow, grow], axis=-1)
    nb = 2 if bsz % 2 == 0 else 1
    fwd = lambda b, t: (b, 0, t, 0)
    bwd = lambda b, t: (b, 0, nt - 1 - t, 0)
    nch = 2 * heads
    o_sds = jax.ShapeDtypeStruct((bsz, heads, s, HEAD_W), BF16)
    return pl.pallas_call(
        functools.partial(_gdn_chunk_kernel, heads=heads, nsub=nsub),
        grid=(bsz // nb, nt),
        in_specs=[pl.BlockSpec((nb, 3 * heads, tb, HEAD_W), fwd),
                  pl.BlockSpec((nb, 3 * heads, tb, HEAD_W), bwd),
                  pl.BlockSpec((nb, tb, LANES), lambda b, t: (b, t, 0)),
                  pl.BlockSpec((nb, tb, LANES), lambda b, t: (b, nt - 1 - t, 0)),
                  pl.BlockSpec((nb, nsub, 4 * heads, 2 * c), lambda b, t: (b, t, 0, 0)),
                  pl.BlockSpec((nb, nsub, 4 * heads, 2 * c), lambda b, t: (b, nt - 1 - t, 0, 0)),
                  pl.BlockSpec((nb, nch, HEAD_W, HEAD_W), lambda b, t: (b, 0, 0, 0))],
        out_specs=(pl.BlockSpec((nb, heads, tb, HEAD_W), fwd),
                   pl.BlockSpec((nb, heads, tb, HEAD_W), bwd),
                   pl.BlockSpec((nb, nch, HEAD_W, HEAD_W), lambda b, t: (b, 0, 0, 0))),
        out_shape=(o_sds, o_sds, jax.ShapeDtypeStruct(s0.shape, F32)),
        scratch_shapes=[pltpu.VMEM((nb, nch, HEAD_W, HEAD_W), F32),
                        pltpu.VMEM((nb, nch, nsub, c, HEAD_W), F32),
                        pltpu.VMEM((nb, nch, nsub, 2 * c, HEAD_W), BF16),
                        pltpu.VMEM((nb, nch, nsub, c, HEAD_W), BF16),
                        pltpu.VMEM((nb, nch, nsub, c, c), BF16),
                        pltpu.VMEM((nb, nch, nsub, 1, LANES), F32)],
        compiler_params=_cparams(2), name="gdn_chunk",
    )(qkv_hm, qkv_hm, gcol, gcol, grow, grow, s0)


def _route_rows(logits_t, rb):
    s = _sigmoid(logits_t)
    sel = s + rb
    rows = [sel[i:i + 1, :] for i in range(N_EXPERTS)]
    srows = [s[i:i + 1, :] for i in range(N_EXPERTS)]
    gscore = []
    for g in range(N_GROUPS):
        a, b, c, d = rows[GROUP_SIZE * g:GROUP_SIZE * (g + 1)]
        hi_ab, lo_ab = jnp.maximum(a, b), jnp.minimum(a, b)
        hi_cd, lo_cd = jnp.maximum(c, d), jnp.minimum(c, d)
        m1 = jnp.maximum(hi_ab, hi_cd)
        m2 = jnp.maximum(jnp.maximum(lo_ab, lo_cd), jnp.minimum(hi_ab, hi_cd))
        gscore.append(m1 + m2)
    best = gscore[0]
    bg = jnp.zeros_like(best)
    for g in range(1, N_GROUPS):
        upd = gscore[g] > best
        bg = jnp.where(upd, float(g), bg)
        best = jnp.where(upd, gscore[g], best)

    def pick(rws, j):
        out = rws[(N_GROUPS - 1) * GROUP_SIZE + j]
        for g in range(N_GROUPS - 2, -1, -1):
            out = jnp.where(bg == float(g), rws[g * GROUP_SIZE + j], out)
        return out

    v = [pick(rows, j) for j in range(GROUP_SIZE)]
    sv = [pick(srows, j) for j in range(GROUP_SIZE)]
    i1, b1, w1 = jnp.zeros_like(best), v[0], sv[0]
    for j in range(1, GROUP_SIZE):
        upd = v[j] > b1
        i1 = jnp.where(upd, float(j), i1)
        b1 = jnp.where(upd, v[j], b1)
        w1 = jnp.where(upd, sv[j], w1)
    i2 = b2 = w2 = None
    for j in range(GROUP_SIZE):
        ok = i1 != float(j)
        if b2 is None:
            i2 = jnp.where(ok, 0.0, 1.0)
            b2 = jnp.where(ok, v[0], v[1])
            w2 = jnp.where(ok, sv[0], sv[1])
            continue
        upd = ok & (v[j] > b2)
        i2 = jnp.where(upd, float(j), i2)
        b2 = jnp.where(upd, v[j], b2)
        w2 = jnp.where(upd, sv[j], w2)
    tot = w1 + w2
    return bg * GROUP_SIZE + i1, bg * GROUP_SIZE + i2, w1 / tot, w2 / tot


def _slot_ranks(e1, e2, base):
    tm = e1.shape[1]
    eid = lax.broadcasted_iota(jnp.int32, (N_EXPERTS, tm), 0).astype(F32)
    oh1 = jnp.where(eid == e1, 1.0, 0.0)
    oh2 = jnp.where(eid == e2, 1.0, 0.0)
    before = (lax.broadcasted_iota(jnp.int32, (tm, tm), 0) < lax.broadcasted_iota(jnp.int32, (tm, tm), 1))
    before = jnp.where(before, 1.0, 0.0).astype(BF16)
    pre = _dot(jnp.concatenate([oh1, oh2], axis=0).astype(BF16), before)
    n1 = jnp.sum(oh1, axis=1, keepdims=True)
    n2 = jnp.sum(oh2, axis=1, keepdims=True)
    r1 = jnp.sum(oh1 * (pre[0:N_EXPERTS] + base), axis=0, keepdims=True)
    r2 = jnp.sum(oh2 * (pre[N_EXPERTS:] + (base + n1)), axis=0, keepdims=True)
    return r1, r2, base + (n1 + n2)


def _out_proj_kernel(*refs, pro, heads):
    if pro == "gdn":
        of_ref, ob_ref, z_ref, gn_ref = refs[:4]
        rest = refs[4:]
        parts = []
        for h in range(heads):
            o = of_ref[0, h].astype(F32) + ob_ref[0, h].astype(F32)
            parts.append(_rms(o, gn_ref[...]))
        a = (jnp.concatenate(parts, axis=1) * _silu(z_ref[0].astype(F32))).astype(BF16)
    else:
        a = refs[0][0]
        rest = refs[1:]
    x_ref, mod_ref, w_ref, b_ref, g_ref, rw_ref, rb_ref, cnt0_ref, x1_ref, h2_ref, rt_ref, cnt_ref = rest

    @pl.when((pl.program_id(0) == 0) & (pl.program_id(1) == 0))
    def _():
        cnt_ref[...] = cnt0_ref[...]

    y = _dot(a, w_ref[...]) + b_ref[...]
    x1 = x_ref[0] + mod_ref[0, 2:3, :] * y
    x1_ref[0] = x1
    h2 = _norm_mod(x1, g_ref[...], mod_ref[0, 3:4, :], mod_ref[0, 4:5, :]).astype(BF16)
    h2_ref[0] = h2
    e1, e2, w1, w2 = _route_rows(_dot_nt(rw_ref[...], h2), rb_ref[...])
    r1, r2, cnt = _slot_ranks(e1, e2, cnt_ref[:, 0:1])
    zero = jnp.zeros_like(e1)
    rt_ref[0] = jnp.concatenate([e1, e2, w1, w2, r1, r2, zero, zero], axis=0)
    cnt_ref[...] = jnp.broadcast_to(cnt, cnt_ref.shape)


def _out_proj(pro, acts, x, mod, w, bias, norm_g, router_wt, router_b, cnt0, *, tm=512):
    g_, s, d = x.shape
    tm = _row_tile(s, tm)
    heads = d // HEAD_W
    full2 = lambda v: pl.BlockSpec(v.shape, lambda b, t: (0, 0))
    row3 = lambda width: pl.BlockSpec((1, tm, width), lambda b, t: (b, t, 0))
    if pro == "gdn":
        hm = pl.BlockSpec((1, heads, tm, HEAD_W), lambda b, t: (b, 0, t, 0))
        zcol = 3
        act_specs = [hm, hm, pl.BlockSpec((1, tm, d), lambda b, t: (b, t, zcol)), full2(acts[3])]
    else:
        act_specs = [row3(d)]
    in_specs = act_specs + [row3(d), pl.BlockSpec((1, 6, d), lambda b, t: (b, 0, 0)),
                            full2(w), full2(bias), full2(norm_g), full2(router_wt), full2(router_b), full2(cnt0)]
    return pl.pallas_call(
        functools.partial(_out_proj_kernel, pro=pro, heads=heads),
        grid=(g_, s // tm),
        in_specs=in_specs,
        out_specs=(row3(d), row3(d), pl.BlockSpec((1, 8, tm), lambda b, t: (b, 0, t)), full2(cnt0)),
        out_shape=(jax.ShapeDtypeStruct((g_, s, d), F32), jax.ShapeDtypeStruct((g_, s, d), BF16),
                   jax.ShapeDtypeStruct((g_, 8, s), F32), jax.ShapeDtypeStruct(cnt0.shape, F32)),
        compiler_params=_cparams(2), name="out_proj_" + pro,
    )(*acts, x, mod, w, bias, norm_g, router_wt, router_b, cnt0)


def _moe_ffn_kernel(te_ref, nv_ref, x_ref, wg_ref, wu_ref, wd_ref, o_ref):
    i = pl.program_id(0)

    @pl.when(i < nv_ref[0])
    def _():
        x = x_ref[...]
        a = (_silu(_dot(x, wg_ref[0, 0].astype(BF16))) * _dot(x, wu_ref[0, 0].astype(BF16))).astype(BF16)
        o_ref[...] = _dot(a, wd_ref[0, 0].astype(BF16)).astype(o_ref.dtype)

    @pl.when(i >= nv_ref[0])
    def _():
        o_ref[...] = jnp.zeros_like(o_ref)


def _moe_ffn(xs, tile_expert, n_valid, wg, wu, wd, layer, *, tm):
    r, d = xs.shape
    f = wg.shape[-1]
    return pl.pallas_call(
        _moe_ffn_kernel,
        grid_spec=pltpu.PrefetchScalarGridSpec(
            num_scalar_prefetch=2, grid=(r // tm,),
            in_specs=[pl.BlockSpec((tm, d), lambda i, te, nv: (i, 0)),
                      pl.BlockSpec((1, 1, d, f), lambda i, te, nv: (layer, te[i], 0, 0)),
                      pl.BlockSpec((1, 1, d, f), lambda i, te, nv: (layer, te[i], 0, 0)),
                      pl.BlockSpec((1, 1, f, d), lambda i, te, nv: (layer, te[i], 0, 0))],
            out_specs=pl.BlockSpec((tm, d), lambda i, te, nv: (i, 0))),
        out_shape=jax.ShapeDtypeStruct((r, d), BF16),
        compiler_params=_cparams(1), name="moe_ffn",
    )(tile_expert, n_valid, xs, wg, wu, wd)


def _combine_kernel(x_ref, y0_ref, y1_ref, rt_ref, mod_ref, g_ref, o_ref, *, final):
    rt = rt_ref[0]
    y = rt[:, 2:3] * y0_ref[0].astype(F32) + rt[:, 3:4] * y1_ref[0].astype(F32)
    x2 = x_ref[0] + mod_ref[0, 5:6, :] * y
    o_ref[0] = _rms(x2, g_ref[...]) if final else x2


def _combine(x1, y0, y1, route_t, mod, final_g, *, final, tm=512):
    g_, s, d = x1.shape
    tm = _row_tile(s, tm)
    row3 = pl.BlockSpec((1, tm, d), lambda b, t: (b, t, 0))
    return pl.pallas_call(
        functools.partial(_combine_kernel, final=final),
        grid=(g_, s // tm),
        in_specs=[row3, row3, row3,
                  pl.BlockSpec((1, tm, 8), lambda b, t: (b, t, 0)),
                  pl.BlockSpec((1, 6, d), lambda b, t: (b, 0, 0)),
                  pl.BlockSpec(final_g.shape, lambda b, t: (0, 0))],
        out_specs=row3,
        out_shape=jax.ShapeDtypeStruct((g_, s, d), F32),
        compiler_params=_cparams(2), name="moe_combine",
    )(x1, y0, y1, route_t, mod, final_g)


def _moe(h2_list, route_list, counts, wg, wu, wd, layer, *, tm=512):
    d = h2_list[0].shape[-1]
    h2 = jnp.concatenate([h.reshape(-1, d) for h in h2_list], axis=0)
    n = h2.shape[0]
    cnt = counts[:, 0].astype(jnp.int32)
    padded = ((cnt + tm - 1) // tm) * tm
    ends = jnp.cumsum(padded)
    starts = (ends - padded).astype(F32)
    pos_list = []
    for r in route_list:
        e, rank = r[:, 0:2, :], r[:, 4:6, :]
        first = jnp.zeros_like(e)
        for k in range(N_EXPERTS):
            first = jnp.where(e == float(k), starts[k], first)
        pos_list.append((first + rank).astype(jnp.int32))
    pos_all = jnp.concatenate([p.transpose(1, 0, 2).reshape(2, -1) for p in pos_list], axis=1)
    n_tiles = (2 * n) // tm + N_EXPERTS
    tok = jnp.arange(n, dtype=jnp.int32)
    filler = jnp.arange(n_tiles * tm, dtype=jnp.int32) % n
    src = filler.at[pos_all.reshape(-1)].set(jnp.concatenate([tok, tok]), mode="promise_in_bounds",
                                             unique_indices=True)
    n_valid = (ends[-1] // tm).astype(jnp.int32)
    tile_start = jnp.minimum(jnp.arange(n_tiles, dtype=jnp.int32), n_valid - 1) * tm
    tile_expert = jnp.sum((tile_start[:, None] >= ends[None, :]).astype(jnp.int32), axis=1)
    tile_expert = jnp.minimum(tile_expert, N_EXPERTS - 1)
    xs = h2.at[src].get(mode="promise_in_bounds")
    ys = _moe_ffn(xs, tile_expert, n_valid.reshape(1), wg, wu, wd, layer, tm=tm)
    outs = []
    for h, r, p in zip(h2_list, route_list, pos_list):
        y0 = ys.at[p[:, 0, :].reshape(-1)].get(mode="promise_in_bounds").reshape(h.shape)
        y1 = ys.at[p[:, 1, :].reshape(-1)].get(mode="promise_in_bounds").reshape(h.shape)
        outs.append((y0, y1, r.transpose(0, 2, 1)))
    return outs


def _rope_tables(seqlen):
    pos = jnp.arange(seqlen, dtype=jnp.int32)
    row = (pos // GRID_W).astype(F32)
    col = (pos % GRID_W).astype(F32)
    inv_freq = ROPE_THETA ** (-jnp.arange(ROPE_FREQS, dtype=F32) / ROPE_FREQS)
    lane = jnp.arange(LANES)
    within = lane % DA_HEAD_DIM
    axis = within // (2 * ROPE_FREQS)
    half = (within % (2 * ROPE_FREQS)) // ROPE_FREQS
    ang = jnp.where(axis[None, :] == 0, row[:, None], col[:, None]) * inv_freq[within % ROPE_FREQS][None, :]
    sign = jnp.where(half == 0, -1.0, 1.0).astype(F32)
    return jnp.cos(ang), jnp.sin(ang) * sign[None, :]


def kernel(x, c, ctx, c_ctx, ada_w, ada_b, norm_mix_g, norm_ffn_g, final_norm_g, conv_pw1_w, conv_pw1_b, conv_dw_w, conv_dw_b, conv_ln_g, conv_ln_b, conv_pw2_w, conv_pw2_b, diff_w_qkv, diff_lambda, diff_subln_g, diff_w_o, gdn_w_in, gdn_conv_w, gdn_a_log, gdn_dt_bias, gdn_norm_g, gdn_w_o, router_w, router_b, moe_w_gate, moe_w_up, moe_w_down):
    bsz, seqlen, d = x.shape
    n_ctx = ctx.shape[1]
    depth = ada_w.shape[0]
    heads = d // HEAD_W
    row = lambda v: v.reshape(1, -1).astype(F32)

    mods = _ada_all(jnp.concatenate([c, c_ctx[None, :]], axis=0), ada_w, ada_b)
    mods = mods.reshape(depth, bsz + 1, 6, d)
    router_wt = router_w.T.astype(BF16)
    router_bc = router_b.reshape(N_EXPERTS, 1).astype(F32)
    cos_t, sin_t = _rope_tables(seqlen)
    zero_bias = jnp.zeros((1, d), F32)

    def layer(i, lat, cx, mods_i, w):
        gb = lat.shape[0]
        last = i == depth - 1
        kind, j = i % N_MIXERS, i // N_MIXERS
        need_ctx = (not last) or kind != 0
        m_lat, m_ctx = mods_i
        gmix = row(norm_mix_g[i])
        streams = [(lat, m_lat)] + ([(cx, m_ctx)] if need_ctx else [])
        acts = []
        if kind == 0:
            for xs, ms in streams:
                a = _in_proj("glu", xs, ms, gmix, [w["w1"], row(conv_pw1_b[j])])
                shp = a.shape
                a = a.reshape(gb, -1, d)
                a = _dwconv_ln_silu(a, conv_dw_w[j].astype(F32), row(conv_dw_b[j]),
                                    row(conv_ln_g[j]), row(conv_ln_b[j]))
                acts.append(("plain", [a.reshape(shp)]))
        elif kind == 1:
            lam_init = 0.8 - 0.6 * math.exp(-0.3 * i)
            qscale = DA_HEAD_DIM ** -0.5 * math.log2(math.e)
            qkv_l = _in_proj("qkv", lat, m_lat, gmix, [w["wq"]], (cos_t, sin_t), rope=True, qscale=qscale)
            qkv_c = _in_proj("qkv", cx, m_ctx, gmix, [w["wq"]], qscale=qscale).reshape(gb, n_ctx, 3 * d)
            lam_v, sub_g = diff_lambda[j].astype(F32), row(diff_subln_g[j])
            o_l = _diff_attn_core(qkv_l, [qkv_c, qkv_l], lam_v, sub_g, lam_init)
            acts.append(("plain", [o_l]))
            if not last:
                o_c = _diff_attn_core(qkv_c, [qkv_c], lam_v, sub_g, lam_init)
                acts.append(("plain", [o_c.reshape(cx.shape)]))
        else:
            conv_w = gdn_conv_w[j].astype(F32)
            zz_l, ba_l = _in_proj("gdn", lat, m_lat, gmix, [w["w_main"], w["w_ba"]])
            zz_c, ba_c = _in_proj("gdn", cx, m_ctx, gmix, [w["w_main"], w["w_ba"]])
            zz_c = zz_c.reshape(gb, n_ctx, -1)
            ba_c = ba_c.reshape(gb, n_ctx, LANES)
            s0 = jnp.zeros((gb, 2 * heads, HEAD_W, HEAD_W), F32)
            ocf, ocb, s1 = _gdn_chunked(_gdn_prep(zz_c, conv_w),
                                        _gdn_gates(ba_c, gdn_a_log[j], gdn_dt_bias[j], heads), s0, heads)
            olf, olb, _ = _gdn_chunked(_gdn_prep(zz_l, conv_w),
                                       _gdn_gates(ba_l, gdn_a_log[j], gdn_dt_bias[j], heads), s1, heads)
            gn = row(gdn_norm_g[j])
            acts.append(("gdn", [olf, olb, zz_l, gn]))
            if not last:
                acts.append(("gdn", [ocf, ocb, zz_c, gn]))

        gffn = row(norm_ffn_g[i])
        x1s, h2s, routes = [], [], []
        counts = jnp.zeros((N_EXPERTS, LANES), F32)
        for (xs, ms), (pro, a) in zip(streams, acts):
            if pro == "gdn" and xs.shape[0] != a[0].shape[0]:
                xin = xs.reshape(gb, -1, d)
                msb = jnp.broadcast_to(ms, (gb, 6, d))
                x1, h2, rt, counts = _out_proj(pro, a, xin, msb, w["w_out"], w["b_out"], gffn,
                                               router_wt, router_bc, counts)
                x1, h2 = x1.reshape(xs.shape), h2.reshape(xs.shape)
                rt = rt.transpose(1, 0, 2).reshape(1, 8, -1)
            else:
                x1, h2, rt, counts = _out_proj(pro, a, xs, ms, w["w_out"], w["b_out"], gffn,
                                               router_wt, router_bc, counts)
            x1s.append(x1)
            h2s.append(h2)
            routes.append(rt)
        moe_out = _moe(h2s, routes, counts, moe_w_gate, moe_w_up, moe_w_down, i)
        fg = row(final_norm_g)
        lat = _combine(x1s[0], *moe_out[0], m_lat, fg, final=last)
        if not last:
            cx = _combine(x1s[1], *moe_out[1], m_ctx, fg, final=False)
        return lat, cx

    def layer_weights(i):
        kind, j = i % N_MIXERS, i // N_MIXERS
        if kind == 0:
            return dict(w1=conv_pw1_w[j].astype(BF16), w_out=conv_pw2_w[j].astype(BF16), b_out=row(conv_pw2_b[j]))
        if kind == 1:
            return dict(wq=diff_w_qkv[j].astype(BF16), w_out=diff_w_o[j].astype(BF16), b_out=zero_bias)
        cch = gdn_conv_w.shape[-1]
        w_in = gdn_w_in[j]
        w_ba = jnp.zeros((d, LANES), BF16).at[:, :4 * heads].set(w_in[:, cch + d:].astype(BF16))
        return dict(w_main=w_in[:, :cch + d].astype(BF16), w_ba=w_ba,
                    w_out=gdn_w_o[j].astype(BF16), b_out=zero_bias)

    lat = x
    cx = ctx.reshape(1, bsz * n_ctx, d)
    for i in range(depth):
        lat, cx = layer(i, lat, cx, (mods[i, :bsz], mods[i, bsz:]), layer_weights(i))
    return lat
```

```python
import functools
import math

import jax
import jax.numpy as jnp
from jax import lax
from jax.experimental import pallas as pl
from jax.experimental.pallas import tpu as pltpu

F32 = jnp.float32
BF16 = jnp.bfloat16
HIGHEST = lax.Precision.HIGHEST

NORM_EPS = 1e-6
N_MIXERS = 3
GRID_W = 64
ROPE_THETA = 10000.0
DA_HEAD_DIM = 64
ROPE_FREQS = DA_HEAD_DIM // 4
HEAD_W = 128
GDN_CHUNK = 64
GDN_CHUNKS_PER_STEP = 4
N_EXPERTS = 16
N_GROUPS = 4
GROUP_SIZE = N_EXPERTS // N_GROUPS
LANES = 128
SUBLANES = 8
CONV_HALO = 16
VMEM_LIMIT = 56 * 1024 * 1024


def _cparams(n_axes):
    return pltpu.CompilerParams(dimension_semantics=("arbitrary",) * n_axes,
                                vmem_limit_bytes=VMEM_LIMIT)


def _dot(a, b, **kw):
    return jnp.dot(a, b, preferred_element_type=F32, **kw)


def _dot_nt(a, b, **kw):
    return lax.dot_general(a, b, (((1,), (1,)), ((), ())), preferred_element_type=F32, **kw)


def _dot_tn(a, b, **kw):
    return lax.dot_general(a, b, (((0,), (0,)), ((), ())), preferred_element_type=F32, **kw)


def _sigmoid(x):
    return 1.0 / (1.0 + jnp.exp(-x))


def _silu(x):
    return x * _sigmoid(x)


def _rms(x, g):
    return x * lax.rsqrt(jnp.mean(x * x, axis=-1, keepdims=True) + NORM_EPS) * g


def _norm_mod(x, g, shift, scale):
    return _rms(x, g) * (1.0 + scale) + shift


def _row_tile(n, want):
    t = min(n, want)
    assert n % t == 0, (n, t)
    return t


def _ada_kernel(c_ref, w_ref, b_ref, o_ref):
    s = _silu(c_ref[...]).astype(BF16)
    o_ref[0] = _dot(s, w_ref[0].astype(BF16)) + b_ref[0]


def _ada_all(cvec, ada_w, ada_b):
    depth, d, n = ada_w.shape
    r = cvec.shape[0]
    tn = d
    return pl.pallas_call(
        _ada_kernel,
        grid=(depth, n // tn),
        in_specs=[pl.BlockSpec((r, d), lambda i, j: (0, 0)),
                  pl.BlockSpec((1, d, tn), lambda i, j: (i, 0, j)),
                  pl.BlockSpec((1, 1, tn), lambda i, j: (i, 0, j))],
        out_specs=pl.BlockSpec((1, r, tn), lambda i, j: (i, 0, j)),
        out_shape=jax.ShapeDtypeStruct((depth, r, n), F32),
        compiler_params=_cparams(2),
        name="adaln",
    )(cvec, ada_w, ada_b.reshape(depth, 1, n))


def _glu_kernel(x_ref, mod_ref, g_ref, w_ref, b_ref, o_ref, *, tn):
    h = _norm_mod(x_ref[0], g_ref[...], mod_ref[0, 0:1, :], mod_ref[0, 1:2, :]).astype(BF16)
    d = o_ref.shape[-1]
    for j in range(0, d, tn):
        a = _dot(h, w_ref[:, j:j + tn]) + b_ref[:, j:j + tn]
        gt = _dot(h, w_ref[:, d + j:d + j + tn]) + b_ref[:, d + j:d + j + tn]
        o_ref[0, :, j:j + tn] = (a * _sigmoid(gt)).astype(o_ref.dtype)


def _rope_lanes(t, cos, sin_signed, first_half):
    swapped = jnp.where(first_half, pltpu.roll(t, LANES - ROPE_FREQS, 1), pltpu.roll(t, ROPE_FREQS, 1))
    return t * cos + swapped * sin_signed


def _qkv_kernel(*refs, tn, rope, qscale):
    if rope:
        x_ref, mod_ref, g_ref, w_ref, cos_ref, sin_ref, o_ref = refs
    else:
        x_ref, mod_ref, g_ref, w_ref, o_ref = refs
    h = _norm_mod(x_ref[0], g_ref[...], mod_ref[0, 0:1, :], mod_ref[0, 1:2, :]).astype(BF16)
    d = x_ref.shape[-1]
    if rope:
        cos = cos_ref[...]
        sin = sin_ref[...]
        lane = lax.broadcasted_iota(jnp.int32, cos.shape, 1)
        first_half = (lane % (2 * ROPE_FREQS)) < ROPE_FREQS
    for j in range(0, 3 * d, tn):
        y = _dot(h, w_ref[:, j:j + tn])
        if j < 2 * d and rope:
            y = jnp.concatenate(
                [_rope_lanes(y[:, s:s + LANES], cos, sin, first_half) for s in range(0, tn, LANES)], axis=1)
        if j < d:
            y = y * qscale
        o_ref[0, :, j:j + tn] = y.astype(o_ref.dtype)


def _gdn_in_kernel(x_ref, mod_ref, g_ref, w_ref, wba_ref, zz_ref, ba_ref, *, tn):
    h = _norm_mod(x_ref[0], g_ref[...], mod_ref[0, 0:1, :], mod_ref[0, 1:2, :]).astype(BF16)
    for j in range(0, zz_ref.shape[-1], tn):
        zz_ref[0, :, j:j + tn] = _dot(h, w_ref[:, j:j + tn]).astype(zz_ref.dtype)
    ba_ref[0] = _dot(h, wba_ref[...])


def _in_proj(kind, x, mod, norm_g, weights, extra=(), *, tm=512, tn=512, rope=False, qscale=1.0):
    g_, s, d = x.shape
    tm = _row_tile(s, tm)
    grid = (g_, s // tm)
    full2 = lambda a: pl.BlockSpec(a.shape, lambda b, t: (0, 0))
    in_specs = [pl.BlockSpec((1, tm, d), lambda b, t: (b, t, 0)),
                pl.BlockSpec((1, 6, d), lambda b, t: (b, 0, 0)),
                full2(norm_g)] + [full2(w) for w in weights]
    args = [x, mod, norm_g] + list(weights)
    if kind == "glu":
        body = functools.partial(_glu_kernel, tn=tn)
        out_shape = jax.ShapeDtypeStruct((g_, s, d), BF16)
        out_specs = pl.BlockSpec((1, tm, d), lambda b, t: (b, t, 0))
    elif kind == "qkv":
        body = functools.partial(_qkv_kernel, tn=tn, rope=rope, qscale=qscale)
        if rope:
            in_specs += [pl.BlockSpec((tm, LANES), lambda b, t: (t, 0))] * 2
            args += list(extra)
        out_shape = jax.ShapeDtypeStruct((g_, s, 3 * d), BF16)
        out_specs = pl.BlockSpec((1, tm, 3 * d), lambda b, t: (b, t, 0))
    else:
        body = functools.partial(_gdn_in_kernel, tn=tn)
        nz = weights[0].shape[1]
        out_shape = (jax.ShapeDtypeStruct((g_, s, nz), BF16), jax.ShapeDtypeStruct((g_, s, LANES), F32))
        out_specs = (pl.BlockSpec((1, tm, nz), lambda b, t: (b, t, 0)),
                     pl.BlockSpec((1, tm, LANES), lambda b, t: (b, t, 0)))
    return pl.pallas_call(body, grid=grid, in_specs=in_specs, out_specs=out_specs, out_shape=out_shape,
                          compiler_params=_cparams(2), name="in_proj_" + kind)(*args)


def _dwconv_kernel(a_ref, prev_ref, next_ref, w_ref, b_ref, lg_ref, lb_ref, o_ref, buf_ref, acc_ref,
                   *, kw, rows, cols):
    t = pl.program_id(1)
    nt = pl.num_programs(1)
    tm, d = a_ref.shape[1], a_ref.shape[2]
    halo = CONV_HALO
    span = tm + 2 * halo
    buf_ref[0, 0:halo, :] = jnp.where(t > 0, prev_ref[0].astype(F32), 0.0)
    buf_ref[0, halo:halo + tm, :] = a_ref[0].astype(F32)
    buf_ref[0, halo + tm:span, :] = jnp.where(t < nt - 1, next_ref[0].astype(F32), 0.0)
    buf_ref[0, span:, :] = jnp.zeros((SUBLANES, d), F32)
    for s in range(1, SUBLANES):
        buf_ref[s, 0:span, :] = buf_ref[0, s:s + span, :]
    base = halo - kw // 2
    for r0 in range(0, tm, rows):
        for c0 in range(0, d, cols):
            acc = jnp.zeros((rows, cols), F32)
            for k in range(kw):
                off = base + k + r0
                s = off % SUBLANES
                acc = acc + w_ref[k:k + 1, c0:c0 + cols] * buf_ref[s, off - s:off - s + rows, c0:c0 + cols]
            acc_ref[r0:r0 + rows, c0:c0 + cols] = acc
    y = acc_ref[...] + b_ref[...]
    yc = y - jnp.mean(y, axis=-1, keepdims=True)
    yn = yc * lax.rsqrt(jnp.mean(yc * yc, axis=-1, keepdims=True) + NORM_EPS) * lg_ref[...] + lb_ref[...]
    o_ref[0] = _silu(yn).astype(o_ref.dtype)


def _dwconv_ln_silu(a, dw_w, dw_b, ln_g, ln_b, *, tm=256):
    bsz, s, d = a.shape
    kw = dw_w.shape[0]
    assert kw // 2 < CONV_HALO
    tm = _row_tile(s, tm)
    hb = tm // CONV_HALO
    nhb = s // CONV_HALO
    full2 = lambda v: pl.BlockSpec(v.shape, lambda b, t: (0, 0))
    return pl.pallas_call(
        functools.partial(_dwconv_kernel, kw=kw, rows=64, cols=256),
        grid=(bsz, s // tm),
        in_specs=[pl.BlockSpec((1, tm, d), lambda b, t: (b, t, 0)),
                  pl.BlockSpec((1, CONV_HALO, d), lambda b, t: (b, jnp.maximum(t * hb - 1, 0), 0)),
                  pl.BlockSpec((1, CONV_HALO, d), lambda b, t: (b, jnp.minimum((t + 1) * hb, nhb - 1), 0)),
                  full2(dw_w), full2(dw_b), full2(ln_g), full2(ln_b)],
        out_specs=pl.BlockSpec((1, tm, d), lambda b, t: (b, t, 0)),
        out_shape=jax.ShapeDtypeStruct((bsz, s, d), BF16),
        scratch_shapes=[pltpu.VMEM((SUBLANES, tm + 2 * CONV_HALO + SUBLANES, d), F32), pltpu.VMEM((tm, d), F32)],
        compiler_params=_cparams(2), name="dwconv_ln_silu",
    )(a, a, a, dw_w, dw_b, ln_g, ln_b)


def _attn_kernel(*refs, n_kv, lam_init, hps):
    lam_ref, sg_ref, q_ref = refs[:3]
    k_refs = refs[3:3 + n_kv]
    v_refs = refs[3 + n_kv:3 + 2 * n_kv]
    o_ref = refs[3 + 2 * n_kv]
    lv = lam_ref[...]
    lam = (jnp.exp(jnp.sum(lv[0:1] * lv[1:2], axis=-1, keepdims=True))
           - jnp.exp(jnp.sum(lv[2:3] * lv[3:4], axis=-1, keepdims=True)) + lam_init)
    lane = lax.broadcasted_iota(jnp.int32, (q_ref.shape[1], HEAD_W), 1)
    scores = []
    for h in range(hps):
        hl = slice(h * HEAD_W, (h + 1) * HEAD_W)
        q = q_ref[0, :, hl]
        for c in range(2):
            in_map = (lane < DA_HEAD_DIM) if c == 0 else (lane >= DA_HEAD_DIM)
            qc = jnp.where(in_map, q, jnp.zeros_like(q))
            scores.append([_dot_nt(qc, k_ref[0, :, hl]) for k_ref in k_refs])
    for h in range(hps):
        hl = slice(h * HEAD_W, (h + 1) * HEAD_W)
        probs, coef = [], []
        for c in range(2):
            ss = scores[2 * h + c]
            m = ss[0].max(axis=-1, keepdims=True)
            for s_ in ss[1:]:
                m = jnp.maximum(m, s_.max(axis=-1, keepdims=True))
            ps = [jnp.exp2(s_ - m) for s_ in ss]
            l = ps[0].sum(axis=-1, keepdims=True)
            for p in ps[1:]:
                l = l + p.sum(axis=-1, keepdims=True)
            probs.append(ps)
            coef.append((1.0 if c == 0 else lam) / l)
        o = None
        for p0, p1, v_ref in zip(probs[0], probs[1], v_refs):
            pv = _dot((p0 * coef[0] - p1 * coef[1]).astype(BF16), v_ref[0, :, hl])
            o = pv if o is None else o + pv
        o = _rms(o, sg_ref[...]) * (1.0 - lam_init)
        o_ref[0, :, hl] = o.astype(o_ref.dtype)


def _diff_attn_core(q_src, kv_srcs, lam_vecs, subln_g, lam_init, *, tq=512, hps=2):
    bsz, sq, d3 = q_src.shape
    d = d3 // 3
    nh = d // HEAD_W
    assert nh % hps == 0
    nhb = nh // hps
    wb = hps * HEAD_W
    tq = _row_tile(sq, tq)
    n_kv = len(kv_srcs)
    in_specs = [pl.BlockSpec(lam_vecs.shape, lambda b, h, t: (0, 0)),
                pl.BlockSpec(subln_g.shape, lambda b, h, t: (0, 0)),
                pl.BlockSpec((1, tq, wb), lambda b, h, t: (b, t, h))]
    in_specs += [pl.BlockSpec((1, s.shape[1], wb), lambda b, h, t: (b, 0, nhb + h)) for s in kv_srcs]
    in_specs += [pl.BlockSpec((1, s.shape[1], wb), lambda b, h, t: (b, 0, 2 * nhb + h)) for s in kv_srcs]
    return pl.pallas_call(
        functools.partial(_attn_kernel, n_kv=n_kv, lam_init=lam_init, hps=hps),
        grid=(bsz, nhb, sq // tq),
        in_specs=in_specs,
        out_specs=pl.BlockSpec((1, tq, wb), lambda b, h, t: (b, t, h)),
        out_shape=jax.ShapeDtypeStruct((bsz, sq, d), BF16),
        compiler_params=_cparams(3), name="diff_attn",
    )(lam_vecs, subln_g, q_src, *kv_srcs, *kv_srcs)


def _gdn_prep_kernel(zz_ref, w_ref, o_ref, buf_ref, *, kw, heads, qk_scale):
    s = zz_ref.shape[1]
    width = heads * HEAD_W
    pad = SUBLANES
    rows = min(s, 256)
    buf_ref[0:pad, :] = jnp.zeros((pad, width), F32)
    buf_ref[pad:pad + s, :] = zz_ref[0].astype(F32)
    buf_ref[pad + s:, :] = jnp.zeros((pad, width), F32)
    kind = pl.program_id(1)
    for h in range(heads):
        lanes = slice(h * HEAD_W, (h + 1) * HEAD_W)
        for r0 in range(0, s, rows):
            acc = jnp.zeros((rows, HEAD_W), F32)
            for k in range(kw):
                off = pad + k - kw // 2 + r0
                acc = acc + w_ref[k:k + 1, lanes] * buf_ref[off:off + rows, lanes]
            y = _silu(acc)
            nrm = lax.rsqrt(jnp.sum(y * y, axis=-1, keepdims=True) + NORM_EPS)
            fac = jnp.where(kind == 0, nrm * qk_scale, jnp.where(kind == 1, nrm, jnp.ones_like(nrm)))
            o_ref[0, h, r0:r0 + rows, :] = (y * fac).astype(o_ref.dtype)


def _gdn_prep(zz, conv_w):
    bsz, s, _ = zz.shape
    kw, cch = conv_w.shape
    heads = cch // (3 * HEAD_W)
    width = heads * HEAD_W
    return pl.pallas_call(
        functools.partial(_gdn_prep_kernel, kw=kw, heads=heads, qk_scale=HEAD_W ** -0.5),
        grid=(bsz, 3),
        in_specs=[pl.BlockSpec((1, s, width), lambda b, j: (b, 0, j)),
                  pl.BlockSpec((kw, width), lambda b, j: (0, j))],
        out_specs=pl.BlockSpec((1, heads, s, HEAD_W), lambda b, j: (b, j, 0, 0)),
        out_shape=jax.ShapeDtypeStruct((bsz, 3 * heads, s, HEAD_W), BF16),
        scratch_shapes=[pltpu.VMEM((s + 2 * SUBLANES, width), F32)],
        compiler_params=_cparams(2), name="gdn_prep",
    )(zz, conv_w)


def _gdn_gates_kernel(ba_ref, alog_ref, dtb_ref, o_ref, *, heads):
    s = ba_ref.shape[1]
    c = GDN_CHUNK
    ii = lax.broadcasted_iota(jnp.int32, (c, c), 0)
    jj = lax.broadcasted_iota(jnp.int32, (c, c), 1)
    tri_f = (ii >= jj).astype(F32)
    tri_b = (ii <= jj).astype(F32)
    lane = lax.broadcasted_iota(jnp.int32, (c, LANES), 1)
    for r0 in range(0, s, c):
        x = ba_ref[0, r0:r0 + c, :]
        beta = _sigmoid(x)
        z = x + dtb_ref[...]
        softplus = jnp.maximum(z, 0.0) + jnp.log1p(jnp.exp(-jnp.abs(z)))
        g = -jnp.exp(alog_ref[...]) * softplus
        gf = _dot(tri_f, g, precision=HIGHEST)
        gb = _dot(tri_b, g, precision=HIGHEST)
        gsum = jnp.where(lane >= 3 * heads, gb, gf)
        o_ref[0, r0:r0 + c, :] = jnp.where(lane < 2 * heads, beta, gsum)


def _gdn_gates(ba, a_log, dt_bias, heads):
    bsz, s, _ = ba.shape
    pad = lambda v: jnp.zeros((1, LANES), F32).at[0, 2 * heads:4 * heads].set(v.reshape(-1).astype(F32))
    return pl.pallas_call(
        functools.partial(_gdn_gates_kernel, heads=heads),
        grid=(bsz,),
        in_specs=[pl.BlockSpec((1, s, LANES), lambda b: (b, 0, 0)),
                  pl.BlockSpec((1, LANES), lambda b: (0, 0)),
                  pl.BlockSpec((1, LANES), lambda b: (0, 0))],
        out_specs=pl.BlockSpec((1, s, LANES), lambda b: (b, 0, 0)),
        out_shape=jax.ShapeDtypeStruct((bsz, s, LANES), F32),
        compiler_params=_cparams(1), name="gdn_gates",
    )(ba, pad(a_log), pad(dt_bias))


def _inv_dot(a, b):
    return _dot(a.astype(BF16), b.astype(BF16))


def _gdn_chunk_kernel(qkv_f, qkv_b, gc_f, gc_b, gr_f, gr_b, s0_ref, of_ref, ob_ref, sout_ref,
                      s_ref, u_ref, wq_ref, kt_ref, qkd_ref, cd_ref, *, heads, nsub):
    t = pl.program_id(1)
    nt = pl.num_programs(1)
    c = GDN_CHUNK
    nb = s0_ref.shape[0]
    nch = 2 * heads

    @pl.when(t == 0)
    def _():
        s_ref[...] = s0_ref[...]

    ii = lax.broadcasted_iota(jnp.int32, (c, 2 * c), 0)
    lane = lax.broadcasted_iota(jnp.int32, (c, 2 * c), 1)
    jj = lane % c
    left = lane < c
    eye_right = (ii == jj) & (lane >= c)
    dirs = ((qkv_f, gc_f, gr_f, ii >= jj, ii > jj, c - 1), (qkv_b, gc_b, gr_b, ii <= jj, ii < jj, 0))

    def local(ci, carry):
        rows = pl.ds(pl.multiple_of(ci * c, c), c)
        chains = []
        for bi in range(nb):
            for di, (qkv, gcr, grr, incl, strict, last) in enumerate(dirs):
                gc = gcr[bi, rows, :]
                gr = grr[bi, ci]
                for h in range(heads):
                    ch = di * heads + h
                    chains.append(dict(
                        bi=bi, ch=ch, incl=incl, strict=strict, last=last,
                        q=qkv[bi, h, rows, :], k=qkv[bi, heads + h, rows, :], v=qkv[bi, 2 * heads + h, rows, :],
                        beta=gc[:, ch:ch + 1],
                        gcol=gc[:, 2 * heads + ch:2 * heads + ch + 1],
                        grow=gr[2 * heads + ch:2 * heads + ch + 1, :]))
        kk = [_dot_nt(a["k"], jnp.concatenate([a["k"], a["k"]], axis=0)) for a in chains]
        qk = [_dot_nt(a["q"], a["k"]) for a in chains]
        dec = [jnp.where(a["incl"], jnp.exp(jnp.where(a["incl"], a["gcol"] - a["grow"], 0.0)), 0.0)
               for a in chains]
        zs = [jnp.where(left, -jnp.where(a["strict"], a["beta"] * kki * dci, 0.0), jnp.where(eye_right, 1.0, 0.0))
              for a, kki, dci in zip(chains, kk, dec)]
        for _ in range(int(math.log2(c))):
            zs = [_inv_dot(z[:, 0:c], z) + jnp.where(left, 0.0, z) for z in zs]
        egs = [jnp.exp(a["gcol"]) for a in chains]
        k32 = [a["k"].astype(F32) for a in chains]
        rhs = [jnp.concatenate([a["v"].astype(F32) * a["beta"], kf * (a["beta"] * eg)], axis=1)
               for a, kf, eg in zip(chains, k32, egs)]
        sols = [_inv_dot(z[:, c:2 * c], r) for z, r in zip(zs, rhs)]
        for a, sol, kf, eg, qki, dci in zip(chains, sols, k32, egs, qk, dec):
            bi, ch = a["bi"], a["ch"]
            glast = a["gcol"][a["last"]:a["last"] + 1, :]
            u_ref[bi, ch, ci] = sol[:, :HEAD_W]
            wq_ref[bi, ch, ci, 0:c, :] = sol[:, HEAD_W:].astype(BF16)
            wq_ref[bi, ch, ci, c:2 * c, :] = (a["q"].astype(F32) * eg).astype(BF16)
            kt_ref[bi, ch, ci] = (kf * jnp.exp(glast - a["gcol"])).astype(BF16)
            qkd_ref[bi, ch, ci] = (qki * dci[:, 0:c]).astype(BF16)
            cd_ref[bi, ch, ci] = jnp.broadcast_to(jnp.exp(glast), (1, LANES))
        return carry

    lax.fori_loop(0, nsub, local, 0)

    def step(si, carry):
        chains = []
        for bi in range(nb):
            for di, o_ref in enumerate((of_ref, ob_ref)):
                ci = si if di == 0 else nsub - 1 - si
                rows = pl.ds(pl.multiple_of(ci * c, c), c)
                chains += [(bi, di * heads + h, h, ci, rows, o_ref) for h in range(heads)]
        ss = [s_ref[bi, ch] for bi, ch, _, _, _, _ in chains]
        ws = [_dot(wq_ref[bi, ch, ci], s.astype(BF16))
              for (bi, ch, _, ci, _, _), s in zip(chains, ss)]
        vn = [(u_ref[bi, ch, ci] - w[0:c]).astype(BF16) for (bi, ch, _, ci, _, _), w in zip(chains, ws)]
        outs = [w[c:2 * c] + _dot(qkd_ref[bi, ch, ci], v) for (bi, ch, _, ci, _, _), w, v in zip(chains, ws, vn)]
        sn = [cd_ref[bi, ch, ci] * s + _dot_tn(kt_ref[bi, ch, ci], v)
              for (bi, ch, _, ci, _, _), s, v in zip(chains, ss, vn)]
        for (bi, ch, h, ci, rows, o_ref), o, s in zip(chains, outs, sn):
            o_ref[bi, h, rows, :] = o.astype(o_ref.dtype)
            s_ref[bi, ch] = s
        return carry

    lax.fori_loop(0, nsub, step, 0)

    @pl.when(t == nt - 1)
    def _():
        sout_ref[...] = s_ref[...]


def _gdn_chunked(qkv_hm, gcol, s0, heads):
    bsz, _, s, _ = qkv_hm.shape
    c = GDN_CHUNK
    nsub = min(GDN_CHUNKS_PER_STEP, s // c)
    tb = nsub * c
    nt = s // tb
    grow = gcol[:, :, :4 * heads].reshape(bsz, s // c, c, 4 * heads).transpose(0, 1, 3, 2)
    grow = jnp.concatenate([grow, grow], axis=-1)
    nb = 2 if bsz % 2 == 0 else 1
    fwd = lambda b, t: (b, 0, t, 0)
    bwd = lambda b, t: (b, 0, nt - 1 - t, 0)
    nch = 2 * heads
    o_sds = jax.ShapeDtypeStruct((bsz, heads, s, HEAD_W), BF16)
    return pl.pallas_call(
        functools.partial(_gdn_chunk_kernel, heads=heads, nsub=nsub),
        grid=(bsz // nb, nt),
        in_specs=[pl.BlockSpec((nb, 3 * heads, tb, HEAD_W), fwd),
                  pl.BlockSpec((nb, 3 * heads, tb, HEAD_W), bwd),
                  pl.BlockSpec((nb, tb, LANES), lambda b, t: (b, t, 0)),
                  pl.BlockSpec((nb, tb, LANES), lambda b, t: (b, nt - 1 - t, 0)),
                  pl.BlockSpec((nb, nsub, 4 * heads, 2 * c), lambda b, t: (b, t, 0, 0)),
                  pl.BlockSpec((nb, nsub, 4 * heads, 2 * c), lambda b, t: (b, nt - 1 - t, 0, 0)),
                  pl.BlockSpec((nb, nch, HEAD_W, HEAD_W), lambda b, t: (b, 0, 0, 0))],
        out_specs=(pl.BlockSpec((nb, heads, tb, HEAD_W), fwd),
                   pl.BlockSpec((nb, heads, tb, HEAD_W), bwd),
                   pl.BlockSpec((nb, nch, HEAD_W, HEAD_W), lambda b, t: (b, 0, 0, 0))),
        out_shape=(o_sds, o_sds, jax.ShapeDtypeStruct(s0.shape, F32)),
        scratch_shapes=[pltpu.VMEM((nb, nch, HEAD_W, HEAD_W), F32),
                        pltpu.VMEM((nb, nch, nsub, c, HEAD_W), F32),
                        pltpu.VMEM((nb, nch, nsub, 2 * c, HEAD_W), BF16),
                        pltpu.VMEM((nb, nch, nsub, c, HEAD_W), BF16),
                        pltpu.VMEM((nb, nch, nsub, c, c), BF16),
                        pltpu.VMEM((nb, nch, nsub, 1, LANES), F32)],
        compiler_params=_cparams(2), name="gdn_chunk",
    )(qkv_hm, qkv_hm, gcol, gcol, grow, grow, s0)


def _route_rows(logits_t, rb):
    s = _sigmoid(logits_t)
    sel = s + rb
    rows = [sel[i:i + 1, :] for i in range(N_EXPERTS)]
    srows = [s[i:i + 1, :] for i in range(N_EXPERTS)]
    gscore = []
    for g in range(N_GROUPS):
        a, b, c, d = rows[GROUP_SIZE * g:GROUP_SIZE * (g + 1)]
        hi_ab, lo_ab = jnp.maximum(a, b), jnp.minimum(a, b)
        hi_cd, lo_cd = jnp.maximum(c, d), jnp.minimum(c, d)
        m1 = jnp.maximum(hi_ab, hi_cd)
        m2 = jnp.maximum(jnp.maximum(lo_ab, lo_cd), jnp.minimum(hi_ab, hi_cd))
        gscore.append(m1 + m2)
    best = gscore[0]
    bg = jnp.zeros_like(best)
    for g in range(1, N_GROUPS):
        upd = gscore[g] > best
        bg = jnp.where(upd, float(g), bg)
        best = jnp.where(upd, gscore[g], best)

    def pick(rws, j):
        out = rws[(N_GROUPS - 1) * GROUP_SIZE + j]
        for g in range(N_GROUPS - 2, -1, -1):
            out = jnp.where(bg == float(g), rws[g * GROUP_SIZE + j], out)
        return out

    v = [pick(rows, j) for j in range(GROUP_SIZE)]
    sv = [pick(srows, j) for j in range(GROUP_SIZE)]
    i1, b1, w1 = jnp.zeros_like(best), v[0], sv[0]
    for j in range(1, GROUP_SIZE):
        upd = v[j] > b1
        i1 = jnp.where(upd, float(j), i1)
        b1 = jnp.where(upd, v[j], b1)
        w1 = jnp.where(upd, sv[j], w1)
    i2 = b2 = w2 = None
    for j in range(GROUP_SIZE):
        ok = i1 != float(j)
        if b2 is None:
            i2 = jnp.where(ok, 0.0, 1.0)
            b2 = jnp.where(ok, v[0], v[1])
            w2 = jnp.where(ok, sv[0], sv[1])
            continue
        upd = ok & (v[j] > b2)
        i2 = jnp.where(upd, float(j), i2)
        b2 = jnp.where(upd, v[j], b2)
        w2 = jnp.where(upd, sv[j], w2)
    tot = w1 + w2
    return bg * GROUP_SIZE + i1, bg * GROUP_SIZE + i2, w1 / tot, w2 / tot


def _slot_ranks(e1, e2, base):
    tm = e1.shape[1]
    eid = lax.broadcasted_iota(jnp.int32, (N_EXPERTS, tm), 0).astype(F32)
    oh1 = jnp.where(eid == e1, 1.0, 0.0)
    oh2 = jnp.where(eid == e2, 1.0, 0.0)
    before = (lax.broadcasted_iota(jnp.int32, (tm, tm), 0) < lax.broadcasted_iota(jnp.int32, (tm, tm), 1))
    before = jnp.where(before, 1.0, 0.0).astype(BF16)
    pre = _dot(jnp.concatenate([oh1, oh2], axis=0).astype(BF16), before)
    n1 = jnp.sum(oh1, axis=1, keepdims=True)
    n2 = jnp.sum(oh2, axis=1, keepdims=True)
    r1 = jnp.sum(oh1 * (pre[0:N_EXPERTS] + base), axis=0, keepdims=True)
    r2 = jnp.sum(oh2 * (pre[N_EXPERTS:] + (base + n1)), axis=0, keepdims=True)
    return r1, r2, base + (n1 + n2)


def _out_proj_kernel(*refs, pro, heads):
    rest = refs[4:] if pro == "gdn" else refs[1:]
    x_ref, mod_ref, w_ref, b_ref, g_ref, rw_ref, rb_ref, cnt0_ref, x1_ref, h2_ref, rt_ref, cnt_ref = rest
    tm = x_ref.shape[1]
    sub = tm // 2 if tm % (2 * LANES) == 0 else tm
    blocks = [slice(r, r + sub) for r in range(0, tm, sub)]

    @pl.when((pl.program_id(0) == 0) & (pl.program_id(1) == 0))
    def _():
        cnt_ref[...] = cnt0_ref[...]

    ys = []
    for blk in blocks:
        if pro == "gdn":
            of_ref, ob_ref, z_ref, gn_ref = refs[:4]
            parts = [_rms(of_ref[0, h, blk, :].astype(F32) + ob_ref[0, h, blk, :].astype(F32), gn_ref[...])
                     for h in range(heads)]
            a = (jnp.concatenate(parts, axis=1) * _silu(z_ref[0, blk, :].astype(F32))).astype(BF16)
        else:
            a = refs[0][0, blk, :]
        ys.append(_dot(a, w_ref[...]) + b_ref[...])
    cnt = cnt_ref[:, 0:1]
    for blk, y in zip(blocks, ys):
        x1 = x_ref[0, blk, :] + mod_ref[0, 2:3, :] * y
        x1_ref[0, blk, :] = x1
        h2 = _norm_mod(x1, g_ref[...], mod_ref[0, 3:4, :], mod_ref[0, 4:5, :]).astype(BF16)
        h2_ref[0, blk, :] = h2
        e1, e2, w1, w2 = _route_rows(_dot_nt(rw_ref[...], h2), rb_ref[...])
        r1, r2, cnt = _slot_ranks(e1, e2, cnt)
        zero = jnp.zeros_like(e1)
        rt_ref[0, :, blk] = jnp.concatenate([e1, e2, w1, w2, r1, r2, zero, zero], axis=0)
    cnt_ref[...] = jnp.broadcast_to(cnt, cnt_ref.shape)


def _out_proj(pro, acts, x, mod, w, bias, norm_g, router_wt, router_b, cnt0, *, tm=512):
    g_, s, d = x.shape
    tm = _row_tile(s, tm)
    heads = d // HEAD_W
    full2 = lambda v: pl.BlockSpec(v.shape, lambda b, t: (0, 0))
    row3 = lambda width: pl.BlockSpec((1, tm, width), lambda b, t: (b, t, 0))
    if pro == "gdn":
        hm = pl.BlockSpec((1, heads, tm, HEAD_W), lambda b, t: (b, 0, t, 0))
        zcol = 3
        act_specs = [hm, hm, pl.BlockSpec((1, tm, d), lambda b, t: (b, t, zcol)), full2(acts[3])]
    else:
        act_specs = [row3(d)]
    in_specs = act_specs + [row3(d), pl.BlockSpec((1, 6, d), lambda b, t: (b, 0, 0)),
                            full2(w), full2(bias), full2(norm_g), full2(router_wt), full2(router_b), full2(cnt0)]
    return pl.pallas_call(
        functools.partial(_out_proj_kernel, pro=pro, heads=heads),
        grid=(g_, s // tm),
        in_specs=in_specs,
        out_specs=(row3(d), row3(d), pl.BlockSpec((1, 8, tm), lambda b, t: (b, 0, t)), full2(cnt0)),
        out_shape=(jax.ShapeDtypeStruct((g_, s, d), F32), jax.ShapeDtypeStruct((g_, s, d), BF16),
                   jax.ShapeDtypeStruct((g_, 8, s), F32), jax.ShapeDtypeStruct(cnt0.shape, F32)),
        compiler_params=_cparams(2), name="out_proj_" + pro,
    )(*acts, x, mod, w, bias, norm_g, router_wt, router_b, cnt0)


def _moe_ffn_kernel(te_ref, nv_ref, x_ref, wg_ref, wu_ref, wd_ref, o_ref):
    i = pl.program_id(0)

    @pl.when(i < nv_ref[0])
    def _():
        x = x_ref[...]
        a = (_silu(_dot(x, wg_ref[0, 0].astype(BF16))) * _dot(x, wu_ref[0, 0].astype(BF16))).astype(BF16)
        o_ref[...] = _dot(a, wd_ref[0, 0].astype(BF16)).astype(o_ref.dtype)

    @pl.when(i >= nv_ref[0])
    def _():
        o_ref[...] = jnp.zeros_like(o_ref)


def _moe_ffn(xs, tile_expert, n_valid, wg, wu, wd, layer, *, tm):
    r, d = xs.shape
    f = wg.shape[-1]
    return pl.pallas_call(
        _moe_ffn_kernel,
        grid_spec=pltpu.PrefetchScalarGridSpec(
            num_scalar_prefetch=2, grid=(r // tm,),
            in_specs=[pl.BlockSpec((tm, d), lambda i, te, nv: (i, 0)),
                      pl.BlockSpec((1, 1, d, f), lambda i, te, nv: (layer, te[i], 0, 0)),
                      pl.BlockSpec((1, 1, d, f), lambda i, te, nv: (layer, te[i], 0, 0)),
                      pl.BlockSpec((1, 1, f, d), lambda i, te, nv: (layer, te[i], 0, 0))],
            out_specs=pl.BlockSpec((tm, d), lambda i, te, nv: (i, 0))),
        out_shape=jax.ShapeDtypeStruct((r, d), BF16),
        compiler_params=_cparams(1), name="moe_ffn",
    )(tile_expert, n_valid, xs, wg, wu, wd)


def _combine_kernel(x_ref, y0_ref, y1_ref, rt_ref, mod_ref, g_ref, o_ref, *, final):
    rt = rt_ref[0]
    y = rt[:, 2:3] * y0_ref[0].astype(F32) + rt[:, 3:4] * y1_ref[0].astype(F32)
    x2 = x_ref[0] + mod_ref[0, 5:6, :] * y
    o_ref[0] = _rms(x2, g_ref[...]) if final else x2


def _combine(x1, y0, y1, route_t, mod, final_g, *, final, tm=512):
    g_, s, d = x1.shape
    tm = _row_tile(s, tm)
    row3 = pl.BlockSpec((1, tm, d), lambda b, t: (b, t, 0))
    return pl.pallas_call(
        functools.partial(_combine_kernel, final=final),
        grid=(g_, s // tm),
        in_specs=[row3, row3, row3,
                  pl.BlockSpec((1, tm, 8), lambda b, t: (b, t, 0)),
                  pl.BlockSpec((1, 6, d), lambda b, t: (b, 0, 0)),
                  pl.BlockSpec(final_g.shape, lambda b, t: (0, 0))],
        out_specs=row3,
        out_shape=jax.ShapeDtypeStruct((g_, s, d), F32),
        compiler_params=_cparams(2), name="moe_combine",
    )(x1, y0, y1, route_t, mod, final_g)


def _invert_slots_kernel(seg_ref, pos_ref, src_ref, *, n, n_slots, n_rows):
    def fill(lo, hi):
        def body(r, carry):
            src_ref[r] = lax.rem(r, n)
            return carry
        lax.fori_loop(lo, hi, body, 0)

    end = 0
    for e in range(N_EXPERTS):
        first, used, reserved = seg_ref[e], seg_ref[N_EXPERTS + e], seg_ref[2 * N_EXPERTS + e]
        fill(first + used, first + reserved)
        end = first + reserved
    fill(end, n_rows)

    rows_per_choice = n // LANES
    for k in range(n_slots // n):
        def row(j, carry, k=k):
            for c in range(LANES):
                src_ref[pos_ref[k * rows_per_choice + j, c]] = j * LANES + c
            return carry
        lax.fori_loop(0, rows_per_choice, row, 0)


def _invert_slots(pos_all, seg, n, n_rows):
    n_slots = pos_all.size
    assert n % LANES == 0 and n_slots % n == 0
    smem = pl.BlockSpec(memory_space=pltpu.SMEM)
    return pl.pallas_call(
        functools.partial(_invert_slots_kernel, n=n, n_slots=n_slots, n_rows=n_rows),
        in_specs=[smem, smem], out_specs=smem,
        out_shape=jax.ShapeDtypeStruct((n_rows,), jnp.int32),
        name="invert_slots",
    )(seg, pos_all.reshape(n_slots // LANES, LANES))


def _moe(h2_list, route_list, counts, wg, wu, wd, layer, *, tm=512):
    d = h2_list[0].shape[-1]
    h2 = jnp.concatenate([h.reshape(-1, d) for h in h2_list], axis=0)
    n = h2.shape[0]
    cnt = counts[:, 0].astype(jnp.int32)
    padded = ((cnt + tm - 1) // tm) * tm
    ends = jnp.cumsum(padded)
    starts = (ends - padded).astype(F32)
    pos_list = []
    for r in route_list:
        e, rank = r[:, 0:2, :], r[:, 4:6, :]
        first = jnp.zeros_like(e)
        for k in range(N_EXPERTS):
            first = jnp.where(e == float(k), starts[k], first)
        pos_list.append((first + rank).astype(jnp.int32))
    pos_all = jnp.concatenate([p.transpose(1, 0, 2).reshape(2, -1) for p in pos_list], axis=1)
    n_tiles = (2 * n) // tm + N_EXPERTS
    seg = jnp.concatenate([(ends - padded), cnt, padded]).astype(jnp.int32)
    src = _invert_slots(pos_all, seg, n, n_tiles * tm)
    n_valid = (ends[-1] // tm).astype(jnp.int32)
    tile_start = jnp.minimum(jnp.arange(n_tiles, dtype=jnp.int32), n_valid - 1) * tm
    tile_expert = jnp.sum((tile_start[:, None] >= ends[None, :]).astype(jnp.int32), axis=1)
    tile_expert = jnp.minimum(tile_expert, N_EXPERTS - 1)
    xs = h2.at[src].get(mode="promise_in_bounds")
    ys = _moe_ffn(xs, tile_expert, n_valid.reshape(1), wg, wu, wd, layer, tm=tm)
    outs = []
    for h, r, p in zip(h2_list, route_list, pos_list):
        y0 = ys.at[p[:, 0, :].reshape(-1)].get(mode="promise_in_bounds").reshape(h.shape)
        y1 = ys.at[p[:, 1, :].reshape(-1)].get(mode="promise_in_bounds").reshape(h.shape)
        outs.append((y0, y1, r.transpose(0, 2, 1)))
    return outs


def _rope_tables(seqlen):
    pos = jnp.arange(seqlen, dtype=jnp.int32)
    row = (pos // GRID_W).astype(F32)
    col = (pos % GRID_W).astype(F32)
    inv_freq = ROPE_THETA ** (-jnp.arange(ROPE_FREQS, dtype=F32) / ROPE_FREQS)
    lane = jnp.arange(LANES)
    within = lane % DA_HEAD_DIM
    axis = within // (2 * ROPE_FREQS)
    half = (within % (2 * ROPE_FREQS)) // ROPE_FREQS
    ang = jnp.where(axis[None, :] == 0, row[:, None], col[:, None]) * inv_freq[within % ROPE_FREQS][None, :]
    sign = jnp.where(half == 0, -1.0, 1.0).astype(F32)
    return jnp.cos(ang), jnp.sin(ang) * sign[None, :]


def kernel(x, c, ctx, c_ctx, ada_w, ada_b, norm_mix_g, norm_ffn_g, final_norm_g, conv_pw1_w, conv_pw1_b, conv_dw_w, conv_dw_b, conv_ln_g, conv_ln_b, conv_pw2_w, conv_pw2_b, diff_w_qkv, diff_lambda, diff_subln_g, diff_w_o, gdn_w_in, gdn_conv_w, gdn_a_log, gdn_dt_bias, gdn_norm_g, gdn_w_o, router_w, router_b, moe_w_gate, moe_w_up, moe_w_down):
    bsz, seqlen, d = x.shape
    n_ctx = ctx.shape[1]
    depth = ada_w.shape[0]
    heads = d // HEAD_W
    row = lambda v: v.reshape(1, -1).astype(F32)

    mods = _ada_all(jnp.concatenate([c, c_ctx[None, :]], axis=0), ada_w, ada_b)
    mods = mods.reshape(depth, bsz + 1, 6, d)
    router_wt = router_w.T.astype(BF16)
    router_bc = router_b.reshape(N_EXPERTS, 1).astype(F32)
    cos_t, sin_t = _rope_tables(seqlen)
    zero_bias = jnp.zeros((1, d), F32)

    def layer(i, lat, cx, mods_i, w):
        gb = lat.shape[0]
        last = i == depth - 1
        kind, j = i % N_MIXERS, i // N_MIXERS
        need_ctx = (not last) or kind != 0
        m_lat, m_ctx = mods_i
        gmix = row(norm_mix_g[i])
        streams = [(lat, m_lat)] + ([(cx, m_ctx)] if need_ctx else [])
        acts = []
        if kind == 0:
            for xs, ms in streams:
                a = _in_proj("glu", xs, ms, gmix, [w["w1"], row(conv_pw1_b[j])])
                shp = a.shape
                a = a.reshape(gb, -1, d)
                a = _dwconv_ln_silu(a, conv_dw_w[j].astype(F32), row(conv_dw_b[j]),
                                    row(conv_ln_g[j]), row(conv_ln_b[j]))
                acts.append(("plain", [a.reshape(shp)]))
        elif kind == 1:
            lam_init = 0.8 - 0.6 * math.exp(-0.3 * i)
            qscale = DA_HEAD_DIM ** -0.5 * math.log2(math.e)
            qkv_l = _in_proj("qkv", lat, m_lat, gmix, [w["wq"]], (cos_t, sin_t), rope=True, qscale=qscale)
            qkv_c = _in_proj("qkv", cx, m_ctx, gmix, [w["wq"]], qscale=qscale).reshape(gb, n_ctx, 3 * d)
            lam_v, sub_g = diff_lambda[j].astype(F32), row(diff_subln_g[j])
            o_l = _diff_attn_core(qkv_l, [qkv_c, qkv_l], lam_v, sub_g, lam_init)
            acts.append(("plain", [o_l]))
            if not last:
                o_c = _diff_attn_core(qkv_c, [qkv_c], lam_v, sub_g, lam_init)
                acts.append(("plain", [o_c.reshape(cx.shape)]))
        else:
            conv_w = gdn_conv_w[j].astype(F32)
            zz_l, ba_l = _in_proj("gdn", lat, m_lat, gmix, [w["w_main"], w["w_ba"]])
            zz_c, ba_c = _in_proj("gdn", cx, m_ctx, gmix, [w["w_main"], w["w_ba"]])
            zz_c = zz_c.reshape(gb, n_ctx, -1)
            ba_c = ba_c.reshape(gb, n_ctx, LANES)
            s0 = jnp.zeros((gb, 2 * heads, HEAD_W, HEAD_W), F32)
            ocf, ocb, s1 = _gdn_chunked(_gdn_prep(zz_c, conv_w),
                                        _gdn_gates(ba_c, gdn_a_log[j], gdn_dt_bias[j], heads), s0, heads)
            olf, olb, _ = _gdn_chunked(_gdn_prep(zz_l, conv_w),
                                       _gdn_gates(ba_l, gdn_a_log[j], gdn_dt_bias[j], heads), s1, heads)
            gn = row(gdn_norm_g[j])
            acts.append(("gdn", [olf, olb, zz_l, gn]))
            if not last:
                acts.append(("gdn", [ocf, ocb, zz_c, gn]))

        gffn = row(norm_ffn_g[i])
        x1s, h2s, routes = [], [], []
        counts = jnp.zeros((N_EXPERTS, LANES), F32)
        for (xs, ms), (pro, a) in zip(streams, acts):
            if pro == "gdn" and xs.shape[0] != a[0].shape[0]:
                xin = xs.reshape(gb, -1, d)
                msb = jnp.broadcast_to(ms, (gb, 6, d))
                x1, h2, rt, counts = _out_proj(pro, a, xin, msb, w["w_out"], w["b_out"], gffn,
                                               router_wt, router_bc, counts)
                x1, h2 = x1.reshape(xs.shape), h2.reshape(xs.shape)
                rt = rt.transpose(1, 0, 2).reshape(1, 8, -1)
            else:
                x1, h2, rt, counts = _out_proj(pro, a, xs, ms, w["w_out"], w["b_out"], gffn,
                                               router_wt, router_bc, counts)
            x1s.append(x1)
            h2s.append(h2)
            routes.append(rt)
        moe_out = _moe(h2s, routes, counts, moe_w_gate, moe_w_up, moe_w_down, i)
        fg = row(final_norm_g)
        lat = _combine(x1s[0], *moe_out[0], m_lat, fg, final=last)
        if not last:
            cx = _combine(x1s[1], *moe_out[1], m_ctx, fg, final=False)
        return lat, cx

    def layer_weights(i):
        kind, j = i % N_MIXERS, i // N_MIXERS
        if kind == 0:
            return dict(w1=conv_pw1_w[j].astype(BF16), w_out=conv_pw2_w[j].astype(BF16), b_out=row(conv_pw2_b[j]))
        if kind == 1:
            return dict(wq=diff_w_qkv[j].astype(BF16), w_out=diff_w_o[j].astype(BF16), b_out=zero_bias)
        cch = gdn_conv_w.shape[-1]
        w_in = gdn_w_in[j]
        w_ba = jnp.zeros((d, LANES), BF16).at[:, :4 * heads].set(w_in[:, cch + d:].astype(BF16))
        return dict(w_main=w_in[:, :cch + d].astype(BF16), w_ba=w_ba,
                    w_out=gdn_w_o[j].astype(BF16), b_out=zero_bias)

    lat = x
    cx = ctx.reshape(1, bsz * n_ctx, d)
    for i in range(depth):
        lat, cx = layer(i, lat, cx, (mods[i, :bsz], mods[i, bsz:]), layer_weights(i))
    return lat
```

```python
import functools
import math

import jax
import jax.numpy as jnp
from jax import lax
from jax.experimental import pallas as pl
from jax.experimental.pallas import tpu as pltpu

F32 = jnp.float32
BF16 = jnp.bfloat16
HIGHEST = lax.Precision.HIGHEST

NORM_EPS = 1e-6
N_MIXERS = 3
GRID_W = 64
ROPE_THETA = 10000.0
DA_HEAD_DIM = 64
ROPE_FREQS = DA_HEAD_DIM // 4
HEAD_W = 128
GDN_CHUNK = 64
GDN_CHUNKS_PER_STEP = 4
N_EXPERTS = 16
N_GROUPS = 4
GROUP_SIZE = N_EXPERTS // N_GROUPS
LANES = 128
SUBLANES = 8
CONV_HALO = 16
VMEM_LIMIT = 56 * 1024 * 1024


def _cparams(n_axes):
    return pltpu.CompilerParams(dimension_semantics=("arbitrary",) * n_axes,
                                vmem_limit_bytes=VMEM_LIMIT)


def _dot(a, b, **kw):
    return jnp.dot(a, b, preferred_element_type=F32, **kw)


def _dot_nt(a, b, **kw):
    return lax.dot_general(a, b, (((1,), (1,)), ((), ())), preferred_element_type=F32, **kw)


def _dot_tn(a, b, **kw):
    return lax.dot_general(a, b, (((0,), (0,)), ((), ())), preferred_element_type=F32, **kw)


def _sigmoid(x):
    return 1.0 / (1.0 + jnp.exp(-x))


def _silu(x):
    return x * _sigmoid(x)


def _rms(x, g):
    return x * lax.rsqrt(jnp.mean(x * x, axis=-1, keepdims=True) + NORM_EPS) * g


def _norm_mod(x, g, shift, scale):
    return _rms(x, g) * (1.0 + scale) + shift


def _row_tile(n, want):
    t = min(n, want)
    assert n % t == 0, (n, t)
    return t


def _ada_kernel(c_ref, w_ref, b_ref, o_ref):
    s = _silu(c_ref[...]).astype(BF16)
    o_ref[0] = _dot(s, w_ref[0].astype(BF16)) + b_ref[0]


def _ada_all(cvec, ada_w, ada_b):
    depth, d, n = ada_w.shape
    r = cvec.shape[0]
    tn = d
    return pl.pallas_call(
        _ada_kernel,
        grid=(depth, n // tn),
        in_specs=[pl.BlockSpec((r, d), lambda i, j: (0, 0)),
                  pl.BlockSpec((1, d, tn), lambda i, j: (i, 0, j)),
                  pl.BlockSpec((1, 1, tn), lambda i, j: (i, 0, j))],
        out_specs=pl.BlockSpec((1, r, tn), lambda i, j: (i, 0, j)),
        out_shape=jax.ShapeDtypeStruct((depth, r, n), F32),
        compiler_params=_cparams(2),
        name="adaln",
    )(cvec, ada_w, ada_b.reshape(depth, 1, n))


def _glu_kernel(x_ref, mod_ref, g_ref, w_ref, b_ref, o_ref, *, tn):
    h = _norm_mod(x_ref[0], g_ref[...], mod_ref[0, 0:1, :], mod_ref[0, 1:2, :]).astype(BF16)
    d = o_ref.shape[-1]
    for j in range(0, d, tn):
        a = _dot(h, w_ref[:, j:j + tn]) + b_ref[:, j:j + tn]
        gt = _dot(h, w_ref[:, d + j:d + j + tn]) + b_ref[:, d + j:d + j + tn]
        o_ref[0, :, j:j + tn] = (a * _sigmoid(gt)).astype(o_ref.dtype)


def _rope_lanes(t, cos, sin_signed, first_half):
    swapped = jnp.where(first_half, pltpu.roll(t, LANES - ROPE_FREQS, 1), pltpu.roll(t, ROPE_FREQS, 1))
    return t * cos + swapped * sin_signed


def _qkv_kernel(*refs, tn, rope, qscale):
    if rope:
        x_ref, mod_ref, g_ref, w_ref, cos_ref, sin_ref, o_ref = refs
    else:
        x_ref, mod_ref, g_ref, w_ref, o_ref = refs
    h = _norm_mod(x_ref[0], g_ref[...], mod_ref[0, 0:1, :], mod_ref[0, 1:2, :]).astype(BF16)
    d = x_ref.shape[-1]
    if rope:
        cos = cos_ref[...]
        sin = sin_ref[...]
        lane = lax.broadcasted_iota(jnp.int32, cos.shape, 1)
        first_half = (lane % (2 * ROPE_FREQS)) < ROPE_FREQS
    for j in range(0, 3 * d, tn):
        y = _dot(h, w_ref[:, j:j + tn])
        if j < 2 * d and rope:
            y = jnp.concatenate(
                [_rope_lanes(y[:, s:s + LANES], cos, sin, first_half) for s in range(0, tn, LANES)], axis=1)
        if j < d:
            y = y * qscale
        o_ref[0, :, j:j + tn] = y.astype(o_ref.dtype)


def _gdn_in_kernel(x_ref, mod_ref, g_ref, w_ref, wba_ref, zz_ref, ba_ref, *, tn):
    h = _norm_mod(x_ref[0], g_ref[...], mod_ref[0, 0:1, :], mod_ref[0, 1:2, :]).astype(BF16)
    for j in range(0, zz_ref.shape[-1], tn):
        zz_ref[0, :, j:j + tn] = _dot(h, w_ref[:, j:j + tn]).astype(zz_ref.dtype)
    ba_ref[0] = _dot(h, wba_ref[...])


def _in_proj(kind, x, mod, norm_g, weights, extra=(), *, tm=512, tn=512, rope=False, qscale=1.0):
    g_, s, d = x.shape
    tm = _row_tile(s, tm)
    grid = (g_, s // tm)
    full2 = lambda a: pl.BlockSpec(a.shape, lambda b, t: (0, 0))
    in_specs = [pl.BlockSpec((1, tm, d), lambda b, t: (b, t, 0)),
                pl.BlockSpec((1, 6, d), lambda b, t: (b, 0, 0)),
                full2(norm_g)] + [full2(w) for w in weights]
    args = [x, mod, norm_g] + list(weights)
    if kind == "glu":
        body = functools.partial(_glu_kernel, tn=tn)
        out_shape = jax.ShapeDtypeStruct((g_, s, d), BF16)
        out_specs = pl.BlockSpec((1, tm, d), lambda b, t: (b, t, 0))
    elif kind == "qkv":
        body = functools.partial(_qkv_kernel, tn=tn, rope=rope, qscale=qscale)
        if rope:
            in_specs += [pl.BlockSpec((tm, LANES), lambda b, t: (t, 0))] * 2
            args += list(extra)
        out_shape = jax.ShapeDtypeStruct((g_, s, 3 * d), BF16)
        out_specs = pl.BlockSpec((1, tm, 3 * d), lambda b, t: (b, t, 0))
    else:
        body = functools.partial(_gdn_in_kernel, tn=tn)
        nz = weights[0].shape[1]
        out_shape = (jax.ShapeDtypeStruct((g_, s, nz), BF16), jax.ShapeDtypeStruct((g_, s, LANES), F32))
        out_specs = (pl.BlockSpec((1, tm, nz), lambda b, t: (b, t, 0)),
                     pl.BlockSpec((1, tm, LANES), lambda b, t: (b, t, 0)))
    return pl.pallas_call(body, grid=grid, in_specs=in_specs, out_specs=out_specs, out_shape=out_shape,
                          compiler_params=_cparams(2), name="in_proj_" + kind)(*args)


def _dwconv_kernel(a_ref, prev_ref, next_ref, w_ref, b_ref, lg_ref, lb_ref, o_ref, buf_ref, acc_ref,
                   *, kw, rows, cols):
    t = pl.program_id(1)
    nt = pl.num_programs(1)
    tm, d = a_ref.shape[1], a_ref.shape[2]
    halo = CONV_HALO
    span = tm + 2 * halo
    buf_ref[0, 0:halo, :] = jnp.where(t > 0, prev_ref[0].astype(F32), 0.0)
    buf_ref[0, halo:halo + tm, :] = a_ref[0].astype(F32)
    buf_ref[0, halo + tm:span, :] = jnp.where(t < nt - 1, next_ref[0].astype(F32), 0.0)
    buf_ref[0, span:, :] = jnp.zeros((SUBLANES, d), F32)
    for s in range(1, SUBLANES):
        buf_ref[s, 0:span, :] = buf_ref[0, s:s + span, :]
    base = halo - kw // 2
    for r0 in range(0, tm, rows):
        for c0 in range(0, d, cols):
            acc = jnp.zeros((rows, cols), F32)
            for k in range(kw):
                off = base + k + r0
                s = off % SUBLANES
                acc = acc + w_ref[k:k + 1, c0:c0 + cols] * buf_ref[s, off - s:off - s + rows, c0:c0 + cols]
            acc_ref[r0:r0 + rows, c0:c0 + cols] = acc
    y = acc_ref[...] + b_ref[...]
    yc = y - jnp.mean(y, axis=-1, keepdims=True)
    yn = yc * lax.rsqrt(jnp.mean(yc * yc, axis=-1, keepdims=True) + NORM_EPS) * lg_ref[...] + lb_ref[...]
    o_ref[0] = _silu(yn).astype(o_ref.dtype)


def _dwconv_ln_silu(a, dw_w, dw_b, ln_g, ln_b, *, tm=256):
    bsz, s, d = a.shape
    kw = dw_w.shape[0]
    assert kw // 2 < CONV_HALO
    tm = _row_tile(s, tm)
    hb = tm // CONV_HALO
    nhb = s // CONV_HALO
    full2 = lambda v: pl.BlockSpec(v.shape, lambda b, t: (0, 0))
    return pl.pallas_call(
        functools.partial(_dwconv_kernel, kw=kw, rows=64, cols=256),
        grid=(bsz, s // tm),
        in_specs=[pl.BlockSpec((1, tm, d), lambda b, t: (b, t, 0)),
                  pl.BlockSpec((1, CONV_HALO, d), lambda b, t: (b, jnp.maximum(t * hb - 1, 0), 0)),
                  pl.BlockSpec((1, CONV_HALO, d), lambda b, t: (b, jnp.minimum((t + 1) * hb, nhb - 1), 0)),
                  full2(dw_w), full2(dw_b), full2(ln_g), full2(ln_b)],
        out_specs=pl.BlockSpec((1, tm, d), lambda b, t: (b, t, 0)),
        out_shape=jax.ShapeDtypeStruct((bsz, s, d), BF16),
        scratch_shapes=[pltpu.VMEM((SUBLANES, tm + 2 * CONV_HALO + SUBLANES, d), F32), pltpu.VMEM((tm, d), F32)],
        compiler_params=_cparams(2), name="dwconv_ln_silu",
    )(a, a, a, dw_w, dw_b, ln_g, ln_b)


def _attn_kernel(*refs, n_kv, lam_init, hps):
    lam_ref, sg_ref, q_ref = refs[:3]
    k_refs = refs[3:3 + n_kv]
    v_refs = refs[3 + n_kv:3 + 2 * n_kv]
    o_ref = refs[3 + 2 * n_kv]
    lv = lam_ref[...]
    lam = (jnp.exp(jnp.sum(lv[0:1] * lv[1:2], axis=-1, keepdims=True))
           - jnp.exp(jnp.sum(lv[2:3] * lv[3:4], axis=-1, keepdims=True)) + lam_init)
    lane = lax.broadcasted_iota(jnp.int32, (q_ref.shape[1], HEAD_W), 1)
    scores = []
    for h in range(hps):
        hl = slice(h * HEAD_W, (h + 1) * HEAD_W)
        q = q_ref[0, :, hl]
        for c in range(2):
            in_map = (lane < DA_HEAD_DIM) if c == 0 else (lane >= DA_HEAD_DIM)
            qc = jnp.where(in_map, q, jnp.zeros_like(q))
            scores.append([_dot_nt(qc, k_ref[0, :, hl]) for k_ref in k_refs])
    for h in range(hps):
        hl = slice(h * HEAD_W, (h + 1) * HEAD_W)
        probs, coef = [], []
        for c in range(2):
            ss = scores[2 * h + c]
            m = ss[0].max(axis=-1, keepdims=True)
            for s_ in ss[1:]:
                m = jnp.maximum(m, s_.max(axis=-1, keepdims=True))
            ps = [jnp.exp2(s_ - m) for s_ in ss]
            l = ps[0].sum(axis=-1, keepdims=True)
            for p in ps[1:]:
                l = l + p.sum(axis=-1, keepdims=True)
            probs.append(ps)
            coef.append((1.0 if c == 0 else lam) / l)
        o = None
        for p0, p1, v_ref in zip(probs[0], probs[1], v_refs):
            pv = _dot((p0 * coef[0] - p1 * coef[1]).astype(BF16), v_ref[0, :, hl])
            o = pv if o is None else o + pv
        o = _rms(o, sg_ref[...]) * (1.0 - lam_init)
        o_ref[0, :, hl] = o.astype(o_ref.dtype)


def _diff_attn_core(q_src, kv_srcs, lam_vecs, subln_g, lam_init, *, tq=512, hps=2):
    bsz, sq, d3 = q_src.shape
    d = d3 // 3
    nh = d // HEAD_W
    assert nh % hps == 0
    nhb = nh // hps
    wb = hps * HEAD_W
    tq = _row_tile(sq, tq)
    n_kv = len(kv_srcs)
    in_specs = [pl.BlockSpec(lam_vecs.shape, lambda b, h, t: (0, 0)),
                pl.BlockSpec(subln_g.shape, lambda b, h, t: (0, 0)),
                pl.BlockSpec((1, tq, wb), lambda b, h, t: (b, t, h))]
    in_specs += [pl.BlockSpec((1, s.shape[1], wb), lambda b, h, t: (b, 0, nhb + h)) for s in kv_srcs]
    in_specs += [pl.BlockSpec((1, s.shape[1], wb), lambda b, h, t: (b, 0, 2 * nhb + h)) for s in kv_srcs]
    return pl.pallas_call(
        functools.partial(_attn_kernel, n_kv=n_kv, lam_init=lam_init, hps=hps),
        grid=(bsz, nhb, sq // tq),
        in_specs=in_specs,
        out_specs=pl.BlockSpec((1, tq, wb), lambda b, h, t: (b, t, h)),
        out_shape=jax.ShapeDtypeStruct((bsz, sq, d), BF16),
        compiler_params=_cparams(3), name="diff_attn",
    )(lam_vecs, subln_g, q_src, *kv_srcs, *kv_srcs)


def _gdn_prep_kernel(zz_ref, w_ref, o_ref, buf_ref, *, kw, heads, qk_scale):
    s = zz_ref.shape[1]
    width = heads * HEAD_W
    pad = SUBLANES
    rows = min(s, 256)
    buf_ref[0:pad, :] = jnp.zeros((pad, width), F32)
    buf_ref[pad:pad + s, :] = zz_ref[0].astype(F32)
    buf_ref[pad + s:, :] = jnp.zeros((pad, width), F32)
    kind = pl.program_id(1)
    for h in range(heads):
        lanes = slice(h * HEAD_W, (h + 1) * HEAD_W)
        for r0 in range(0, s, rows):
            acc = jnp.zeros((rows, HEAD_W), F32)
            for k in range(kw):
                off = pad + k - kw // 2 + r0
                acc = acc + w_ref[k:k + 1, lanes] * buf_ref[off:off + rows, lanes]
            y = _silu(acc)
            nrm = lax.rsqrt(jnp.sum(y * y, axis=-1, keepdims=True) + NORM_EPS)
            fac = jnp.where(kind == 0, nrm * qk_scale, jnp.where(kind == 1, nrm, jnp.ones_like(nrm)))
            o_ref[0, h, r0:r0 + rows, :] = (y * fac).astype(o_ref.dtype)


def _gdn_prep(zz, conv_w):
    bsz, s, _ = zz.shape
    kw, cch = conv_w.shape
    heads = cch // (3 * HEAD_W)
    width = heads * HEAD_W
    return pl.pallas_call(
        functools.partial(_gdn_prep_kernel, kw=kw, heads=heads, qk_scale=HEAD_W ** -0.5),
        grid=(bsz, 3),
        in_specs=[pl.BlockSpec((1, s, width), lambda b, j: (b, 0, j)),
                  pl.BlockSpec((kw, width), lambda b, j: (0, j))],
        out_specs=pl.BlockSpec((1, heads, s, HEAD_W), lambda b, j: (b, j, 0, 0)),
        out_shape=jax.ShapeDtypeStruct((bsz, 3 * heads, s, HEAD_W), BF16),
        scratch_shapes=[pltpu.VMEM((s + 2 * SUBLANES, width), F32)],
        compiler_params=_cparams(2), name="gdn_prep",
    )(zz, conv_w)


def _gdn_gates_kernel(ba_ref, alog_ref, dtb_ref, o_ref, *, heads):
    s = ba_ref.shape[1]
    c = GDN_CHUNK
    ii = lax.broadcasted_iota(jnp.int32, (c, c), 0)
    jj = lax.broadcasted_iota(jnp.int32, (c, c), 1)
    tri_f = (ii >= jj).astype(F32)
    tri_b = (ii <= jj).astype(F32)
    lane = lax.broadcasted_iota(jnp.int32, (c, LANES), 1)
    for r0 in range(0, s, c):
        x = ba_ref[0, r0:r0 + c, :]
        beta = _sigmoid(x)
        z = x + dtb_ref[...]
        softplus = jnp.maximum(z, 0.0) + jnp.log1p(jnp.exp(-jnp.abs(z)))
        g = -jnp.exp(alog_ref[...]) * softplus
        gf = _dot(tri_f, g, precision=HIGHEST)
        gb = _dot(tri_b, g, precision=HIGHEST)
        gsum = jnp.where(lane >= 3 * heads, gb, gf)
        o_ref[0, r0:r0 + c, :] = jnp.where(lane < 2 * heads, beta, gsum)


def _gdn_gates(ba, a_log, dt_bias, heads):
    bsz, s, _ = ba.shape
    pad = lambda v: jnp.zeros((1, LANES), F32).at[0, 2 * heads:4 * heads].set(v.reshape(-1).astype(F32))
    return pl.pallas_call(
        functools.partial(_gdn_gates_kernel, heads=heads),
        grid=(bsz,),
        in_specs=[pl.BlockSpec((1, s, LANES), lambda b: (b, 0, 0)),
                  pl.BlockSpec((1, LANES), lambda b: (0, 0)),
                  pl.BlockSpec((1, LANES), lambda b: (0, 0))],
        out_specs=pl.BlockSpec((1, s, LANES), lambda b: (b, 0, 0)),
        out_shape=jax.ShapeDtypeStruct((bsz, s, LANES), F32),
        compiler_params=_cparams(1), name="gdn_gates",
    )(ba, pad(a_log), pad(dt_bias))


def _inv_dot(a, b):
    return _dot(a.astype(BF16), b.astype(BF16))


def _gdn_chunk_kernel(qkv_f, qkv_b, gc_f, gc_b, gr_f, gr_b, s0_ref, of_ref, ob_ref, sout_ref,
                      s_ref, u_ref, wq_ref, kt_ref, qkd_ref, cd_ref, *, heads, nsub):
    t = pl.program_id(1)
    nt = pl.num_programs(1)
    c = GDN_CHUNK
    nb = s0_ref.shape[0]
    nch = 2 * heads

    @pl.when(t == 0)
    def _():
        s_ref[...] = s0_ref[...]

    ii = lax.broadcasted_iota(jnp.int32, (c, 2 * c), 0)
    lane = lax.broadcasted_iota(jnp.int32, (c, 2 * c), 1)
    jj = lane % c
    left = lane < c
    eye_right = (ii == jj) & (lane >= c)
    dirs = ((qkv_f, gc_f, gr_f, ii >= jj, ii > jj, c - 1), (qkv_b, gc_b, gr_b, ii <= jj, ii < jj, 0))

    def local(ci, carry):
        rows = pl.ds(pl.multiple_of(ci * c, c), c)
        chains = []
        for bi in range(nb):
            for di, (qkv, gcr, grr, incl, strict, last) in enumerate(dirs):
                gc = gcr[bi, rows, :]
                gr = grr[bi, ci]
                for h in range(heads):
                    ch = di * heads + h
                    chains.append(dict(
                        bi=bi, ch=ch, incl=incl, strict=strict, last=last,
                        q=qkv[bi, h, rows, :], k=qkv[bi, heads + h, rows, :], v=qkv[bi, 2 * heads + h, rows, :],
                        beta=gc[:, ch:ch + 1],
                        gcol=gc[:, 2 * heads + ch:2 * heads + ch + 1],
                        grow=gr[2 * heads + ch:2 * heads + ch + 1, :]))
        kk = [_dot_nt(a["k"], jnp.concatenate([a["k"], a["k"]], axis=0)) for a in chains]
        qk = [_dot_nt(a["q"], a["k"]) for a in chains]
        dec = [jnp.where(a["incl"], jnp.exp(jnp.where(a["incl"], a["gcol"] - a["grow"], 0.0)), 0.0)
               for a in chains]
        zs = [jnp.where(left, -jnp.where(a["strict"], a["beta"] * kki * dci, 0.0), jnp.where(eye_right, 1.0, 0.0))
              for a, kki, dci in zip(chains, kk, dec)]
        for _ in range(int(math.log2(c))):
            zs = [_inv_dot(z[:, 0:c], z) + jnp.where(left, 0.0, z) for z in zs]
        egs = [jnp.exp(a["gcol"]) for a in chains]
        k32 = [a["k"].astype(F32) for a in chains]
        rhs = [jnp.concatenate([a["v"].astype(F32) * a["beta"], kf * (a["beta"] * eg)], axis=1)
               for a, kf, eg in zip(chains, k32, egs)]
        sols = [_inv_dot(z[:, c:2 * c], r) for z, r in zip(zs, rhs)]
        for a, sol, kf, eg, qki, dci in zip(chains, sols, k32, egs, qk, dec):
            bi, ch = a["bi"], a["ch"]
            glast = a["gcol"][a["last"]:a["last"] + 1, :]
            u_ref[bi, ch, ci] = sol[:, :HEAD_W]
            wq_ref[bi, ch, ci, 0:c, :] = sol[:, HEAD_W:].astype(BF16)
            wq_ref[bi, ch, ci, c:2 * c, :] = (a["q"].astype(F32) * eg).astype(BF16)
            kt_ref[bi, ch, ci] = (kf * jnp.exp(glast - a["gcol"])).astype(BF16)
            qkd_ref[bi, ch, ci] = (qki * dci[:, 0:c]).astype(BF16)
            cd_ref[bi, ch, ci] = jnp.broadcast_to(jnp.exp(glast), (1, LANES))
        return carry

    lax.fori_loop(0, nsub, local, 0)

    def step(si, carry):
        chains = []
        for bi in range(nb):
            for di, o_ref in enumerate((of_ref, ob_ref)):
                ci = si if di == 0 else nsub - 1 - si
                rows = pl.ds(pl.multiple_of(ci * c, c), c)
                chains += [(bi, di * heads + h, h, ci, rows, o_ref) for h in range(heads)]
        ss = [s_ref[bi, ch] for bi, ch, _, _, _, _ in chains]
        ws = [_dot(wq_ref[bi, ch, ci], s.astype(BF16))
              for (bi, ch, _, ci, _, _), s in zip(chains, ss)]
        vn = [(u_ref[bi, ch, ci] - w[0:c]).astype(BF16) for (bi, ch, _, ci, _, _), w in zip(chains, ws)]
        outs = [w[c:2 * c] + _dot(qkd_ref[bi, ch, ci], v) for (bi, ch, _, ci, _, _), w, v in zip(chains, ws, vn)]
        sn = [cd_ref[bi, ch, ci] * s + _dot_tn(kt_ref[bi, ch, ci], v)
              for (bi, ch, _, ci, _, _), s, v in zip(chains, ss, vn)]
        for (bi, ch, h, ci, rows, o_ref), o, s in zip(chains, outs, sn):
            o_ref[bi, h, rows, :] = o.astype(o_ref.dtype)
            s_ref[bi, ch] = s
        return carry

    lax.fori_loop(0, nsub, step, 0)

    @pl.when(t == nt - 1)
    def _():
        sout_ref[...] = s_ref[...]


def _gdn_chunked(qkv_hm, gcol, s0, heads):
    bsz, _, s, _ = qkv_hm.shape
    c = GDN_CHUNK
    nsub = min(GDN_CHUNKS_PER_STEP, s // c)
    tb = nsub * c
    nt = s // tb
    grow = gcol[:, :, :4 * heads].reshape(bsz, s // c, c, 4 * heads).transpose(0, 1, 3, 2)
    grow = jnp.concatenate([grow, grow], axis=-1)
    nb = 2 if bsz % 2 == 0 else 1
    fwd = lambda b, t: (b, 0, t, 0)
    bwd = lambda b, t: (b, 0, nt - 1 - t, 0)
    nch = 2 * heads
    o_sds = jax.ShapeDtypeStruct((bsz, heads, s, HEAD_W), BF16)
    return pl.pallas_call(
        functools.partial(_gdn_chunk_kernel, heads=heads, nsub=nsub),
        grid=(bsz // nb, nt),
        in_specs=[pl.BlockSpec((nb, 3 * heads, tb, HEAD_W), fwd),
                  pl.BlockSpec((nb, 3 * heads, tb, HEAD_W), bwd),
                  pl.BlockSpec((nb, tb, LANES), lambda b, t: (b, t, 0)),
                  pl.BlockSpec((nb, tb, LANES), lambda b, t: (b, nt - 1 - t, 0)),
                  pl.BlockSpec((nb, nsub, 4 * heads, 2 * c), lambda b, t: (b, t, 0, 0)),
                  pl.BlockSpec((nb, nsub, 4 * heads, 2 * c), lambda b, t: (b, nt - 1 - t, 0, 0)),
                  pl.BlockSpec((nb, nch, HEAD_W, HEAD_W), lambda b, t: (b, 0, 0, 0))],
        out_specs=(pl.BlockSpec((nb, heads, tb, HEAD_W), fwd),
                   pl.BlockSpec((nb, heads, tb, HEAD_W), bwd),
                   pl.BlockSpec((nb, nch, HEAD_W, HEAD_W), lambda b, t: (b, 0, 0, 0))),
        out_shape=(o_sds, o_sds, jax.ShapeDtypeStruct(s0.shape, F32)),
        scratch_shapes=[pltpu.VMEM((nb, nch, HEAD_W, HEAD_W), F32),
                        pltpu.VMEM((nb, nch, nsub, c, HEAD_W), F32),
                        pltpu.VMEM((nb, nch, nsub, 2 * c, HEAD_W), BF16),
                        pltpu.VMEM((nb, nch, nsub, c, HEAD_W), BF16),
                        pltpu.VMEM((nb, nch, nsub, c, c), BF16),
                        pltpu.VMEM((nb, nch, nsub, 1, LANES), F32)],
        compiler_params=_cparams(2), name="gdn_chunk",
    )(qkv_hm, qkv_hm, gcol, gcol, grow, grow, s0)


def _route_rows(logits_t, rb):
    s = _sigmoid(logits_t)
    sel = s + rb
    rows = [sel[i:i + 1, :] for i in range(N_EXPERTS)]
    srows = [s[i:i + 1, :] for i in range(N_EXPERTS)]
    gscore = []
    for g in range(N_GROUPS):
        a, b, c, d = rows[GROUP_SIZE * g:GROUP_SIZE * (g + 1)]
        hi_ab, lo_ab = jnp.maximum(a, b), jnp.minimum(a, b)
        hi_cd, lo_cd = jnp.maximum(c, d), jnp.minimum(c, d)
        m1 = jnp.maximum(hi_ab, hi_cd)
        m2 = jnp.maximum(jnp.maximum(lo_ab, lo_cd), jnp.minimum(hi_ab, hi_cd))
        gscore.append(m1 + m2)
    best = gscore[0]
    bg = jnp.zeros_like(best)
    for g in range(1, N_GROUPS):
        upd = gscore[g] > best
        bg = jnp.where(upd, float(g), bg)
        best = jnp.where(upd, gscore[g], best)

    def pick(rws, j):
        out = rws[(N_GROUPS - 1) * GROUP_SIZE + j]
        for g in range(N_GROUPS - 2, -1, -1):
            out = jnp.where(bg == float(g), rws[g * GROUP_SIZE + j], out)
        return out

    v = [pick(rows, j) for j in range(GROUP_SIZE)]
    sv = [pick(srows, j) for j in range(GROUP_SIZE)]
    i1, b1, w1 = jnp.zeros_like(best), v[0], sv[0]
    for j in range(1, GROUP_SIZE):
        upd = v[j] > b1
        i1 = jnp.where(upd, float(j), i1)
        b1 = jnp.where(upd, v[j], b1)
        w1 = jnp.where(upd, sv[j], w1)
    i2 = b2 = w2 = None
    for j in range(GROUP_SIZE):
        ok = i1 != float(j)
        if b2 is None:
            i2 = jnp.where(ok, 0.0, 1.0)
            b2 = jnp.where(ok, v[0], v[1])
            w2 = jnp.where(ok, sv[0], sv[1])
            continue
        upd = ok & (v[j] > b2)
        i2 = jnp.where(upd, float(j), i2)
        b2 = jnp.where(upd, v[j], b2)
        w2 = jnp.where(upd, sv[j], w2)
    tot = w1 + w2
    return bg * GROUP_SIZE + i1, bg * GROUP_SIZE + i2, w1 / tot, w2 / tot


def _slot_ranks(e1, e2, base):
    tm = e1.shape[1]
    eid = lax.broadcasted_iota(jnp.int32, (N_EXPERTS, tm), 0).astype(F32)
    oh1 = jnp.where(eid == e1, 1.0, 0.0)
    oh2 = jnp.where(eid == e2, 1.0, 0.0)
    before = (lax.broadcasted_iota(jnp.int32, (tm, tm), 0) < lax.broadcasted_iota(jnp.int32, (tm, tm), 1))
    before = jnp.where(before, 1.0, 0.0).astype(BF16)
    pre = _dot(jnp.concatenate([oh1, oh2], axis=0).astype(BF16), before)
    n1 = jnp.sum(oh1, axis=1, keepdims=True)
    n2 = jnp.sum(oh2, axis=1, keepdims=True)
    r1 = jnp.sum(oh1 * (pre[0:N_EXPERTS] + base), axis=0, keepdims=True)
    r2 = jnp.sum(oh2 * (pre[N_EXPERTS:] + (base + n1)), axis=0, keepdims=True)
    return r1, r2, base + (n1 + n2)


def _out_proj_kernel(*refs, pro, heads):
    rest = refs[4:] if pro == "gdn" else refs[1:]
    x_ref, mod_ref, w_ref, b_ref, g_ref, rw_ref, rb_ref, cnt0_ref, x1_ref, h2_ref, rt_ref, cnt_ref = rest
    tm = x_ref.shape[1]
    sub = tm // 2 if tm % (2 * LANES) == 0 else tm
    blocks = [slice(r, r + sub) for r in range(0, tm, sub)]

    @pl.when((pl.program_id(0) == 0) & (pl.program_id(1) == 0))
    def _():
        cnt_ref[...] = cnt0_ref[...]

    ys = []
    for blk in blocks:
        if pro == "gdn":
            of_ref, ob_ref, z_ref, gn_ref = refs[:4]
            parts = [_rms(of_ref[0, h, blk, :].astype(F32) + ob_ref[0, h, blk, :].astype(F32), gn_ref[...])
                     for h in range(heads)]
            a = (jnp.concatenate(parts, axis=1) * _silu(z_ref[0, blk, :].astype(F32))).astype(BF16)
        else:
            a = refs[0][0, blk, :]
        ys.append(_dot(a, w_ref[...]) + b_ref[...])
    cnt = cnt_ref[:, 0:1]
    for blk, y in zip(blocks, ys):
        x1 = x_ref[0, blk, :] + mod_ref[0, 2:3, :] * y
        x1_ref[0, blk, :] = x1
        h2 = _norm_mod(x1, g_ref[...], mod_ref[0, 3:4, :], mod_ref[0, 4:5, :]).astype(BF16)
        h2_ref[0, blk, :] = h2
        e1, e2, w1, w2 = _route_rows(_dot_nt(rw_ref[...], h2), rb_ref[...])
        r1, r2, cnt = _slot_ranks(e1, e2, cnt)
        zero = jnp.zeros_like(e1)
        rt_ref[0, :, blk] = jnp.concatenate([e1, e2, w1, w2, r1, r2, zero, zero], axis=0)
    cnt_ref[...] = jnp.broadcast_to(cnt, cnt_ref.shape)


def _out_proj(pro, acts, x, mod, w, bias, norm_g, router_wt, router_b, cnt0, *, tm=512):
    g_, s, d = x.shape
    tm = _row_tile(s, tm)
    heads = d // HEAD_W
    full2 = lambda v: pl.BlockSpec(v.shape, lambda b, t: (0, 0))
    row3 = lambda width: pl.BlockSpec((1, tm, width), lambda b, t: (b, t, 0))
    if pro == "gdn":
        hm = pl.BlockSpec((1, heads, tm, HEAD_W), lambda b, t: (b, 0, t, 0))
        zcol = 3
        act_specs = [hm, hm, pl.BlockSpec((1, tm, d), lambda b, t: (b, t, zcol)), full2(acts[3])]
    else:
        act_specs = [row3(d)]
    in_specs = act_specs + [row3(d), pl.BlockSpec((1, 6, d), lambda b, t: (b, 0, 0)),
                            full2(w), full2(bias), full2(norm_g), full2(router_wt), full2(router_b), full2(cnt0)]
    return pl.pallas_call(
        functools.partial(_out_proj_kernel, pro=pro, heads=heads),
        grid=(g_, s // tm),
        in_specs=in_specs,
        out_specs=(row3(d), row3(d), pl.BlockSpec((1, 8, tm), lambda b, t: (b, 0, t)), full2(cnt0)),
        out_shape=(jax.ShapeDtypeStruct((g_, s, d), F32), jax.ShapeDtypeStruct((g_, s, d), BF16),
                   jax.ShapeDtypeStruct((g_, 8, s), F32), jax.ShapeDtypeStruct(cnt0.shape, F32)),
        compiler_params=_cparams(2), name="out_proj_" + pro,
    )(*acts, x, mod, w, bias, norm_g, router_wt, router_b, cnt0)


def _moe_ffn_kernel(te_ref, nv_ref, x_ref, wg_ref, wu_ref, wd_ref, *rest):
    o_ref = rest[-1]
    i = pl.program_id(0)

    @pl.when(i < nv_ref[0])
    def _():
        x = x_ref[...]
        a = (_silu(_dot(x, wg_ref[0, 0].astype(BF16))) * _dot(x, wu_ref[0, 0].astype(BF16))).astype(BF16)
        o_ref[...] = _dot(a, wd_ref[0, 0].astype(BF16)).astype(o_ref.dtype)

    @pl.when(i >= nv_ref[0])
    def _():
        o_ref[...] = jnp.zeros_like(o_ref)


def _moe_ffn(xs, tile_expert, n_valid, wg, wu, wd, layer, *, tm, n_tiles, tile_off=0, prev=None):
    r, d = xs.shape
    f = wg.shape[-1]
    xs_tiles = r // tm
    steps = n_tiles - tile_off if prev is None else xs_tiles
    last = xs_tiles - 1
    wspec = lambda shape: pl.BlockSpec(shape, lambda i, te, nv: (layer, te[jnp.minimum(i, last)], 0, 0))
    in_specs = [pl.BlockSpec((tm, d), lambda i, te, nv: (jnp.minimum(i, last), 0)),
                wspec((1, 1, d, f)), wspec((1, 1, d, f)), wspec((1, 1, f, d))]
    args = [tile_expert, n_valid, xs, wg, wu, wd]
    aliases = {}
    if prev is not None:
        in_specs.append(pl.BlockSpec(memory_space=pl.ANY))
        args.append(prev)
        aliases = {len(args) - 1: 0}
    return pl.pallas_call(
        _moe_ffn_kernel,
        grid_spec=pltpu.PrefetchScalarGridSpec(
            num_scalar_prefetch=2, grid=(steps,), in_specs=in_specs,
            out_specs=pl.BlockSpec((tm, d), lambda i, te, nv: (i + tile_off, 0))),
        out_shape=jax.ShapeDtypeStruct((n_tiles * tm, d), BF16),
        input_output_aliases=aliases,
        compiler_params=_cparams(1), name="moe_ffn",
    )(*args)


def _combine_kernel(x_ref, y0_ref, y1_ref, rt_ref, mod_ref, g_ref, o_ref, *, final):
    rt = rt_ref[0]
    y = rt[:, 2:3] * y0_ref[0].astype(F32) + rt[:, 3:4] * y1_ref[0].astype(F32)
    x2 = x_ref[0] + mod_ref[0, 5:6, :] * y
    o_ref[0] = _rms(x2, g_ref[...]) if final else x2


def _combine(x1, y0, y1, route_t, mod, final_g, *, final, tm=512):
    g_, s, d = x1.shape
    tm = _row_tile(s, tm)
    row3 = pl.BlockSpec((1, tm, d), lambda b, t: (b, t, 0))
    return pl.pallas_call(
        functools.partial(_combine_kernel, final=final),
        grid=(g_, s // tm),
        in_specs=[row3, row3, row3,
                  pl.BlockSpec((1, tm, 8), lambda b, t: (b, t, 0)),
                  pl.BlockSpec((1, 6, d), lambda b, t: (b, 0, 0)),
                  pl.BlockSpec(final_g.shape, lambda b, t: (0, 0))],
        out_specs=row3,
        out_shape=jax.ShapeDtypeStruct((g_, s, d), F32),
        compiler_params=_cparams(2), name="moe_combine",
    )(x1, y0, y1, route_t, mod, final_g)


def _invert_slots_kernel(seg_ref, pos_ref, src_ref, *, n, n_slots, n_rows):
    def fill(lo, hi):
        def body(r, carry):
            src_ref[r] = lax.rem(r, n)
            return carry
        lax.fori_loop(lo, hi, body, 0)

    end = 0
    for e in range(N_EXPERTS):
        first, used, reserved = seg_ref[e], seg_ref[N_EXPERTS + e], seg_ref[2 * N_EXPERTS + e]
        fill(first + used, first + reserved)
        end = first + reserved
    fill(end, n_rows)

    rows_per_choice = n // LANES
    for k in range(n_slots // n):
        def row(j, carry, k=k):
            for c in range(LANES):
                src_ref[pos_ref[k * rows_per_choice + j, c]] = j * LANES + c
            return carry
        lax.fori_loop(0, rows_per_choice, row, 0)


def _invert_slots(pos_all, seg, n, n_rows):
    n_slots = pos_all.size
    assert n % LANES == 0 and n_slots % n == 0
    smem = pl.BlockSpec(memory_space=pltpu.SMEM)
    return pl.pallas_call(
        functools.partial(_invert_slots_kernel, n=n, n_slots=n_slots, n_rows=n_rows),
        in_specs=[smem, smem], out_specs=smem,
        out_shape=jax.ShapeDtypeStruct((n_rows,), jnp.int32),
        name="invert_slots",
    )(seg, pos_all.reshape(n_slots // LANES, LANES))


def _moe(h2_list, route_list, counts, wg, wu, wd, layer, *, tm=512):
    d = h2_list[0].shape[-1]
    h2 = jnp.concatenate([h.reshape(-1, d) for h in h2_list], axis=0)
    n = h2.shape[0]
    cnt = counts[:, 0].astype(jnp.int32)
    padded = ((cnt + tm - 1) // tm) * tm
    ends = jnp.cumsum(padded)
    starts = (ends - padded).astype(F32)
    pos_list = []
    for r in route_list:
        e, rank = r[:, 0:2, :], r[:, 4:6, :]
        first = jnp.zeros_like(e)
        for k in range(N_EXPERTS):
            first = jnp.where(e == float(k), starts[k], first)
        pos_list.append((first + rank).astype(jnp.int32))
    pos_all = jnp.concatenate([p.transpose(1, 0, 2).reshape(2, -1) for p in pos_list], axis=1)
    n_tiles = (2 * n) // tm + N_EXPERTS
    seg = jnp.concatenate([(ends - padded), cnt, padded]).astype(jnp.int32)
    src = _invert_slots(pos_all, seg, n, n_tiles * tm)
    n_valid = (ends[-1] // tm).astype(jnp.int32)
    tile_start = jnp.minimum(jnp.arange(n_tiles, dtype=jnp.int32), n_valid - 1) * tm
    tile_expert = jnp.sum((tile_start[:, None] >= ends[None, :]).astype(jnp.int32), axis=1)
    tile_expert = jnp.minimum(tile_expert, N_EXPERTS - 1)
    parts = 2 if n_tiles % 2 == 0 else 1
    pt = n_tiles // parts
    ys = None
    for pi in range(parts):
        xs = h2.at[src[pi * pt * tm:(pi + 1) * pt * tm]].get(mode="promise_in_bounds")
        nv = jnp.clip(n_valid - pi * pt, 0, pt).reshape(1)
        ys = _moe_ffn(xs, tile_expert[pi * pt:(pi + 1) * pt], nv, wg, wu, wd, layer, tm=tm,
                      n_tiles=n_tiles, tile_off=pi * pt, prev=ys)
    outs = []
    for h, r, p in zip(h2_list, route_list, pos_list):
        y0 = ys.at[p[:, 0, :].reshape(-1)].get(mode="promise_in_bounds").reshape(h.shape)
        y1 = ys.at[p[:, 1, :].reshape(-1)].get(mode="promise_in_bounds").reshape(h.shape)
        outs.append((y0, y1, r.transpose(0, 2, 1)))
    return outs


def _rope_tables(seqlen):
    pos = jnp.arange(seqlen, dtype=jnp.int32)
    row = (pos // GRID_W).astype(F32)
    col = (pos % GRID_W).astype(F32)
    inv_freq = ROPE_THETA ** (-jnp.arange(ROPE_FREQS, dtype=F32) / ROPE_FREQS)
    lane = jnp.arange(LANES)
    within = lane % DA_HEAD_DIM
    axis = within // (2 * ROPE_FREQS)
    half = (within % (2 * ROPE_FREQS)) // ROPE_FREQS
    ang = jnp.where(axis[None, :] == 0, row[:, None], col[:, None]) * inv_freq[within % ROPE_FREQS][None, :]
    sign = jnp.where(half == 0, -1.0, 1.0).astype(F32)
    return jnp.cos(ang), jnp.sin(ang) * sign[None, :]


def kernel(x, c, ctx, c_ctx, ada_w, ada_b, norm_mix_g, norm_ffn_g, final_norm_g, conv_pw1_w, conv_pw1_b, conv_dw_w, conv_dw_b, conv_ln_g, conv_ln_b, conv_pw2_w, conv_pw2_b, diff_w_qkv, diff_lambda, diff_subln_g, diff_w_o, gdn_w_in, gdn_conv_w, gdn_a_log, gdn_dt_bias, gdn_norm_g, gdn_w_o, router_w, router_b, moe_w_gate, moe_w_up, moe_w_down):
    bsz, seqlen, d = x.shape
    n_ctx = ctx.shape[1]
    depth = ada_w.shape[0]
    heads = d // HEAD_W
    row = lambda v: v.reshape(1, -1).astype(F32)

    mods = _ada_all(jnp.concatenate([c, c_ctx[None, :]], axis=0), ada_w, ada_b)
    mods = mods.reshape(depth, bsz + 1, 6, d)
    router_wt = router_w.T.astype(BF16)
    router_bc = router_b.reshape(N_EXPERTS, 1).astype(F32)
    cos_t, sin_t = _rope_tables(seqlen)
    zero_bias = jnp.zeros((1, d), F32)

    def layer(i, lat, cx, mods_i, w):
        gb = lat.shape[0]
        last = i == depth - 1
        kind, j = i % N_MIXERS, i // N_MIXERS
        need_ctx = (not last) or kind != 0
        m_lat, m_ctx = mods_i
        gmix = row(norm_mix_g[i])
        streams = [(lat, m_lat)] + ([(cx, m_ctx)] if need_ctx else [])
        acts = []
        if kind == 0:
            for xs, ms in streams:
                a = _in_proj("glu", xs, ms, gmix, [w["w1"], row(conv_pw1_b[j])])
                shp = a.shape
                a = a.reshape(gb, -1, d)
                a = _dwconv_ln_silu(a, conv_dw_w[j].astype(F32), row(conv_dw_b[j]),
                                    row(conv_ln_g[j]), row(conv_ln_b[j]))
                acts.append(("plain", [a.reshape(shp)]))
        elif kind == 1:
            lam_init = 0.8 - 0.6 * math.exp(-0.3 * i)
            qscale = DA_HEAD_DIM ** -0.5 * math.log2(math.e)
            qkv_l = _in_proj("qkv", lat, m_lat, gmix, [w["wq"]], (cos_t, sin_t), rope=True, qscale=qscale)
            qkv_c = _in_proj("qkv", cx, m_ctx, gmix, [w["wq"]], qscale=qscale).reshape(gb, n_ctx, 3 * d)
            lam_v, sub_g = diff_lambda[j].astype(F32), row(diff_subln_g[j])
            o_l = _diff_attn_core(qkv_l, [qkv_c, qkv_l], lam_v, sub_g, lam_init)
            acts.append(("plain", [o_l]))
            if not last:
                o_c = _diff_attn_core(qkv_c, [qkv_c], lam_v, sub_g, lam_init)
                acts.append(("plain", [o_c.reshape(cx.shape)]))
        else:
            conv_w = gdn_conv_w[j].astype(F32)
            zz_l, ba_l = _in_proj("gdn", lat, m_lat, gmix, [w["w_main"], w["w_ba"]])
            zz_c, ba_c = _in_proj("gdn", cx, m_ctx, gmix, [w["w_main"], w["w_ba"]])
            zz_c = zz_c.reshape(gb, n_ctx, -1)
            ba_c = ba_c.reshape(gb, n_ctx, LANES)
            s0 = jnp.zeros((gb, 2 * heads, HEAD_W, HEAD_W), F32)
            ocf, ocb, s1 = _gdn_chunked(_gdn_prep(zz_c, conv_w),
                                        _gdn_gates(ba_c, gdn_a_log[j], gdn_dt_bias[j], heads), s0, heads)
            olf, olb, _ = _gdn_chunked(_gdn_prep(zz_l, conv_w),
                                       _gdn_gates(ba_l, gdn_a_log[j], gdn_dt_bias[j], heads), s1, heads)
            gn = row(gdn_norm_g[j])
            acts.append(("gdn", [olf, olb, zz_l, gn]))
            if not last:
                acts.append(("gdn", [ocf, ocb, zz_c, gn]))

        gffn = row(norm_ffn_g[i])
        x1s, h2s, routes = [], [], []
        counts = jnp.zeros((N_EXPERTS, LANES), F32)
        for (xs, ms), (pro, a) in zip(streams, acts):
            if pro == "gdn" and xs.shape[0] != a[0].shape[0]:
                xin = xs.reshape(gb, -1, d)
                msb = jnp.broadcast_to(ms, (gb, 6, d))
                x1, h2, rt, counts = _out_proj(pro, a, xin, msb, w["w_out"], w["b_out"], gffn,
                                               router_wt, router_bc, counts)
                x1, h2 = x1.reshape(xs.shape), h2.reshape(xs.shape)
                rt = rt.transpose(1, 0, 2).reshape(1, 8, -1)
            else:
                x1, h2, rt, counts = _out_proj(pro, a, xs, ms, w["w_out"], w["b_out"], gffn,
                                               router_wt, router_bc, counts)
            x1s.append(x1)
            h2s.append(h2)
            routes.append(rt)
        moe_out = _moe(h2s, routes, counts, moe_w_gate, moe_w_up, moe_w_down, i)
        fg = row(final_norm_g)
        lat = _combine(x1s[0], *moe_out[0], m_lat, fg, final=last)
        if not last:
            cx = _combine(x1s[1], *moe_out[1], m_ctx, fg, final=False)
        return lat, cx

    def layer_weights(i):
        kind, j = i % N_MIXERS, i // N_MIXERS
        if kind == 0:
            return dict(w1=conv_pw1_w[j].astype(BF16), w_out=conv_pw2_w[j].astype(BF16), b_out=row(conv_pw2_b[j]))
        if kind == 1:
            return dict(wq=diff_w_qkv[j].astype(BF16), w_out=diff_w_o[j].astype(BF16), b_out=zero_bias)
        cch = gdn_conv_w.shape[-1]
        w_in = gdn_w_in[j]
        w_ba = jnp.zeros((d, LANES), BF16).at[:, :4 * heads].set(w_in[:, cch + d:].astype(BF16))
        return dict(w_main=w_in[:, :cch + d].astype(BF16), w_ba=w_ba,
                    w_out=gdn_w_o[j].astype(BF16), b_out=zero_bias)

    lat = x
    cx = ctx.reshape(1, bsz * n_ctx, d)
    for i in range(depth):
        lat, cx = layer(i, lat, cx, (mods[i, :bsz], mods[i, bsz:]), layer_weights(i))
    return lat
```

```python
import functools
import math

import jax
import jax.numpy as jnp
from jax import lax
from jax.experimental import pallas as pl
from jax.experimental.pallas import tpu as pltpu

F32 = jnp.float32
BF16 = jnp.bfloat16
HIGHEST = lax.Precision.HIGHEST

NORM_EPS = 1e-6
N_MIXERS = 3
GRID_W = 64
ROPE_THETA = 10000.0
DA_HEAD_DIM = 64
ROPE_FREQS = DA_HEAD_DIM // 4
HEAD_W = 128
GDN_CHUNK = 64
GDN_CHUNKS_PER_STEP = 4
N_EXPERTS = 16
N_GROUPS = 4
GROUP_SIZE = N_EXPERTS // N_GROUPS
LANES = 128
SUBLANES = 8
CONV_HALO = 16
VMEM_LIMIT = 56 * 1024 * 1024


def _cparams(n_axes):
    return pltpu.CompilerParams(dimension_semantics=("arbitrary",) * n_axes,
                                vmem_limit_bytes=VMEM_LIMIT)


def _dot(a, b, **kw):
    return jnp.dot(a, b, preferred_element_type=F32, **kw)


def _dot_nt(a, b, **kw):
    return lax.dot_general(a, b, (((1,), (1,)), ((), ())), preferred_element_type=F32, **kw)


def _dot_tn(a, b, **kw):
    return lax.dot_general(a, b, (((0,), (0,)), ((), ())), preferred_element_type=F32, **kw)


def _sigmoid(x):
    return 1.0 / (1.0 + jnp.exp(-x))


def _silu(x):
    return x * _sigmoid(x)


def _rms(x, g):
    return x * lax.rsqrt(jnp.mean(x * x, axis=-1, keepdims=True) + NORM_EPS) * g


def _norm_mod(x, g, shift, scale):
    return _rms(x, g) * (1.0 + scale) + shift


def _row_tile(n, want):
    t = min(n, want)
    assert n % t == 0, (n, t)
    return t


def _ada_kernel(c_ref, w_ref, b_ref, o_ref):
    s = _silu(c_ref[...]).astype(BF16)
    o_ref[0] = _dot(s, w_ref[0].astype(BF16)) + b_ref[0]


def _ada_all(cvec, ada_w, ada_b):
    depth, d, n = ada_w.shape
    r = cvec.shape[0]
    tn = d
    return pl.pallas_call(
        _ada_kernel,
        grid=(depth, n // tn),
        in_specs=[pl.BlockSpec((r, d), lambda i, j: (0, 0)),
                  pl.BlockSpec((1, d, tn), lambda i, j: (i, 0, j)),
                  pl.BlockSpec((1, 1, tn), lambda i, j: (i, 0, j))],
        out_specs=pl.BlockSpec((1, r, tn), lambda i, j: (i, 0, j)),
        out_shape=jax.ShapeDtypeStruct((depth, r, n), F32),
        compiler_params=_cparams(2),
        name="adaln",
    )(cvec, ada_w, ada_b.reshape(depth, 1, n))


def _glu_kernel(x_ref, mod_ref, g_ref, w_ref, b_ref, o_ref, *, tn):
    h = _norm_mod(x_ref[0], g_ref[...], mod_ref[0, 0:1, :], mod_ref[0, 1:2, :]).astype(BF16)
    d = o_ref.shape[-1]
    for j in range(0, d, tn):
        a = _dot(h, w_ref[:, j:j + tn]) + b_ref[:, j:j + tn]
        gt = _dot(h, w_ref[:, d + j:d + j + tn]) + b_ref[:, d + j:d + j + tn]
        o_ref[0, :, j:j + tn] = (a * _sigmoid(gt)).astype(o_ref.dtype)


def _rope_lanes(t, cos, sin_signed, first_half):
    swapped = jnp.where(first_half, pltpu.roll(t, LANES - ROPE_FREQS, 1), pltpu.roll(t, ROPE_FREQS, 1))
    return t * cos + swapped * sin_signed


def _qkv_kernel(*refs, tn, rope, qscale):
    if rope:
        x_ref, mod_ref, g_ref, w_ref, cos_ref, sin_ref, o_ref = refs
    else:
        x_ref, mod_ref, g_ref, w_ref, o_ref = refs
    h = _norm_mod(x_ref[0], g_ref[...], mod_ref[0, 0:1, :], mod_ref[0, 1:2, :]).astype(BF16)
    d = x_ref.shape[-1]
    if rope:
        cos = cos_ref[...]
        sin = sin_ref[...]
        lane = lax.broadcasted_iota(jnp.int32, cos.shape, 1)
        first_half = (lane % (2 * ROPE_FREQS)) < ROPE_FREQS
    for j in range(0, 3 * d, tn):
        y = _dot(h, w_ref[:, j:j + tn])
        if j < 2 * d and rope:
            y = jnp.concatenate(
                [_rope_lanes(y[:, s:s + LANES], cos, sin, first_half) for s in range(0, tn, LANES)], axis=1)
        if j < d:
            y = y * qscale
        o_ref[0, :, j:j + tn] = y.astype(o_ref.dtype)


def _gdn_in_kernel(x_ref, mod_ref, g_ref, w_ref, wba_ref, zz_ref, ba_ref, *, tn):
    h = _norm_mod(x_ref[0], g_ref[...], mod_ref[0, 0:1, :], mod_ref[0, 1:2, :]).astype(BF16)
    for j in range(0, zz_ref.shape[-1], tn):
        zz_ref[0, :, j:j + tn] = _dot(h, w_ref[:, j:j + tn]).astype(zz_ref.dtype)
    ba_ref[0] = _dot(h, wba_ref[...])


def _in_proj(kind, x, mod, norm_g, weights, extra=(), *, tm=512, tn=512, rope=False, qscale=1.0):
    g_, s, d = x.shape
    tm = _row_tile(s, tm)
    grid = (g_, s // tm)
    full2 = lambda a: pl.BlockSpec(a.shape, lambda b, t: (0, 0))
    in_specs = [pl.BlockSpec((1, tm, d), lambda b, t: (b, t, 0)),
                pl.BlockSpec((1, 6, d), lambda b, t: (b, 0, 0)),
                full2(norm_g)] + [full2(w) for w in weights]
    args = [x, mod, norm_g] + list(weights)
    if kind == "glu":
        body = functools.partial(_glu_kernel, tn=tn)
        out_shape = jax.ShapeDtypeStruct((g_, s, d), BF16)
        out_specs = pl.BlockSpec((1, tm, d), lambda b, t: (b, t, 0))
    elif kind == "qkv":
        body = functools.partial(_qkv_kernel, tn=tn, rope=rope, qscale=qscale)
        if rope:
            in_specs += [pl.BlockSpec((tm, LANES), lambda b, t: (t, 0))] * 2
            args += list(extra)
        out_shape = jax.ShapeDtypeStruct((g_, s, 3 * d), BF16)
        out_specs = pl.BlockSpec((1, tm, 3 * d), lambda b, t: (b, t, 0))
    else:
        body = functools.partial(_gdn_in_kernel, tn=tn)
        nz = weights[0].shape[1]
        out_shape = (jax.ShapeDtypeStruct((g_, s, nz), BF16), jax.ShapeDtypeStruct((g_, s, LANES), F32))
        out_specs = (pl.BlockSpec((1, tm, nz), lambda b, t: (b, t, 0)),
                     pl.BlockSpec((1, tm, LANES), lambda b, t: (b, t, 0)))
    return pl.pallas_call(body, grid=grid, in_specs=in_specs, out_specs=out_specs, out_shape=out_shape,
                          compiler_params=_cparams(2), name="in_proj_" + kind)(*args)


def _dwconv_kernel(a_ref, prev_ref, next_ref, w_ref, b_ref, lg_ref, lb_ref, o_ref, buf_ref, acc_ref,
                   *, kw, rows, cols):
    t = pl.program_id(1)
    nt = pl.num_programs(1)
    tm, d = a_ref.shape[1], a_ref.shape[2]
    halo = CONV_HALO
    span = tm + 2 * halo
    buf_ref[0, 0:halo, :] = jnp.where(t > 0, prev_ref[0].astype(F32), 0.0)
    buf_ref[0, halo:halo + tm, :] = a_ref[0].astype(F32)
    buf_ref[0, halo + tm:span, :] = jnp.where(t < nt - 1, next_ref[0].astype(F32), 0.0)
    buf_ref[0, span:, :] = jnp.zeros((SUBLANES, d), F32)
    for s in range(1, SUBLANES):
        buf_ref[s, 0:span, :] = buf_ref[0, s:s + span, :]
    base = halo - kw // 2
    for r0 in range(0, tm, rows):
        for c0 in range(0, d, cols):
            acc = jnp.zeros((rows, cols), F32)
            for k in range(kw):
                off = base + k + r0
                s = off % SUBLANES
                acc = acc + w_ref[k:k + 1, c0:c0 + cols] * buf_ref[s, off - s:off - s + rows, c0:c0 + cols]
            acc_ref[r0:r0 + rows, c0:c0 + cols] = acc
    y = acc_ref[...] + b_ref[...]
    yc = y - jnp.mean(y, axis=-1, keepdims=True)
    yn = yc * lax.rsqrt(jnp.mean(yc * yc, axis=-1, keepdims=True) + NORM_EPS) * lg_ref[...] + lb_ref[...]
    o_ref[0] = _silu(yn).astype(o_ref.dtype)


def _dwconv_ln_silu(a, dw_w, dw_b, ln_g, ln_b, *, tm=256):
    bsz, s, d = a.shape
    kw = dw_w.shape[0]
    assert kw // 2 < CONV_HALO
    tm = _row_tile(s, tm)
    hb = tm // CONV_HALO
    nhb = s // CONV_HALO
    full2 = lambda v: pl.BlockSpec(v.shape, lambda b, t: (0, 0))
    return pl.pallas_call(
        functools.partial(_dwconv_kernel, kw=kw, rows=64, cols=256),
        grid=(bsz, s // tm),
        in_specs=[pl.BlockSpec((1, tm, d), lambda b, t: (b, t, 0)),
                  pl.BlockSpec((1, CONV_HALO, d), lambda b, t: (b, jnp.maximum(t * hb - 1, 0), 0)),
                  pl.BlockSpec((1, CONV_HALO, d), lambda b, t: (b, jnp.minimum((t + 1) * hb, nhb - 1), 0)),
                  full2(dw_w), full2(dw_b), full2(ln_g), full2(ln_b)],
        out_specs=pl.BlockSpec((1, tm, d), lambda b, t: (b, t, 0)),
        out_shape=jax.ShapeDtypeStruct((bsz, s, d), BF16),
        scratch_shapes=[pltpu.VMEM((SUBLANES, tm + 2 * CONV_HALO + SUBLANES, d), F32), pltpu.VMEM((tm, d), F32)],
        compiler_params=_cparams(2), name="dwconv_ln_silu",
    )(a, a, a, dw_w, dw_b, ln_g, ln_b)


def _attn_kernel(*refs, n_kv, lam_init, hps):
    lam_ref, sg_ref, q_ref = refs[:3]
    k_refs = refs[3:3 + n_kv]
    v_refs = refs[3 + n_kv:3 + 2 * n_kv]
    o_ref = refs[3 + 2 * n_kv]
    lv = lam_ref[...]
    lam = (jnp.exp(jnp.sum(lv[0:1] * lv[1:2], axis=-1, keepdims=True))
           - jnp.exp(jnp.sum(lv[2:3] * lv[3:4], axis=-1, keepdims=True)) + lam_init)
    lane = lax.broadcasted_iota(jnp.int32, (q_ref.shape[1], HEAD_W), 1)
    scores = []
    for h in range(hps):
        hl = slice(h * HEAD_W, (h + 1) * HEAD_W)
        q = q_ref[0, :, hl]
        for c in range(2):
            in_map = (lane < DA_HEAD_DIM) if c == 0 else (lane >= DA_HEAD_DIM)
            qc = jnp.where(in_map, q, jnp.zeros_like(q))
            scores.append([_dot_nt(qc, k_ref[0, :, hl]) for k_ref in k_refs])
    for h in range(hps):
        hl = slice(h * HEAD_W, (h + 1) * HEAD_W)
        probs, coef = [], []
        for c in range(2):
            ss = scores[2 * h + c]
            m = ss[0].max(axis=-1, keepdims=True)
            for s_ in ss[1:]:
                m = jnp.maximum(m, s_.max(axis=-1, keepdims=True))
            ps = [jnp.exp2(s_ - m) for s_ in ss]
            l = ps[0].sum(axis=-1, keepdims=True)
            for p in ps[1:]:
                l = l + p.sum(axis=-1, keepdims=True)
            probs.append(ps)
            coef.append((1.0 if c == 0 else lam) / l)
        o = None
        for p0, p1, v_ref in zip(probs[0], probs[1], v_refs):
            pv = _dot((p0 * coef[0] - p1 * coef[1]).astype(BF16), v_ref[0, :, hl])
            o = pv if o is None else o + pv
        o = _rms(o, sg_ref[...]) * (1.0 - lam_init)
        o_ref[0, :, hl] = o.astype(o_ref.dtype)


def _diff_attn_core(q_src, kv_srcs, lam_vecs, subln_g, lam_init, *, tq=512, hps=2):
    bsz, sq, d3 = q_src.shape
    d = d3 // 3
    nh = d // HEAD_W
    assert nh % hps == 0
    nhb = nh // hps
    wb = hps * HEAD_W
    tq = _row_tile(sq, tq)
    n_kv = len(kv_srcs)
    in_specs = [pl.BlockSpec(lam_vecs.shape, lambda b, h, t: (0, 0)),
                pl.BlockSpec(subln_g.shape, lambda b, h, t: (0, 0)),
                pl.BlockSpec((1, tq, wb), lambda b, h, t: (b, t, h))]
    in_specs += [pl.BlockSpec((1, s.shape[1], wb), lambda b, h, t: (b, 0, nhb + h)) for s in kv_srcs]
    in_specs += [pl.BlockSpec((1, s.shape[1], wb), lambda b, h, t: (b, 0, 2 * nhb + h)) for s in kv_srcs]
    return pl.pallas_call(
        functools.partial(_attn_kernel, n_kv=n_kv, lam_init=lam_init, hps=hps),
        grid=(bsz, nhb, sq // tq),
        in_specs=in_specs,
        out_specs=pl.BlockSpec((1, tq, wb), lambda b, h, t: (b, t, h)),
        out_shape=jax.ShapeDtypeStruct((bsz, sq, d), BF16),
        compiler_params=_cparams(3), name="diff_attn",
    )(lam_vecs, subln_g, q_src, *kv_srcs, *kv_srcs)


def _gdn_prep_kernel(zz_ref, w_ref, o_ref, buf_ref, *, kw, heads, qk_scale):
    s = zz_ref.shape[1]
    width = heads * HEAD_W
    pad = SUBLANES
    rows = min(s, 256)
    buf_ref[0:pad, :] = jnp.zeros((pad, width), F32)
    buf_ref[pad:pad + s, :] = zz_ref[0].astype(F32)
    buf_ref[pad + s:, :] = jnp.zeros((pad, width), F32)
    kind = pl.program_id(1)
    for h in range(heads):
        lanes = slice(h * HEAD_W, (h + 1) * HEAD_W)
        for r0 in range(0, s, rows):
            acc = jnp.zeros((rows, HEAD_W), F32)
            for k in range(kw):
                off = pad + k - kw // 2 + r0
                acc = acc + w_ref[k:k + 1, lanes] * buf_ref[off:off + rows, lanes]
            y = _silu(acc)
            nrm = lax.rsqrt(jnp.sum(y * y, axis=-1, keepdims=True) + NORM_EPS)
            fac = jnp.where(kind == 0, nrm * qk_scale, jnp.where(kind == 1, nrm, jnp.ones_like(nrm)))
            o_ref[0, h, r0:r0 + rows, :] = (y * fac).astype(o_ref.dtype)


def _gdn_prep(zz, conv_w):
    bsz, s, _ = zz.shape
    kw, cch = conv_w.shape
    heads = cch // (3 * HEAD_W)
    width = heads * HEAD_W
    return pl.pallas_call(
        functools.partial(_gdn_prep_kernel, kw=kw, heads=heads, qk_scale=HEAD_W ** -0.5),
        grid=(bsz, 3),
        in_specs=[pl.BlockSpec((1, s, width), lambda b, j: (b, 0, j)),
                  pl.BlockSpec((kw, width), lambda b, j: (0, j))],
        out_specs=pl.BlockSpec((1, heads, s, HEAD_W), lambda b, j: (b, j, 0, 0)),
        out_shape=jax.ShapeDtypeStruct((bsz, 3 * heads, s, HEAD_W), BF16),
        scratch_shapes=[pltpu.VMEM((s + 2 * SUBLANES, width), F32)],
        compiler_params=_cparams(2), name="gdn_prep",
    )(zz, conv_w)


def _gdn_gates_kernel(ba_ref, alog_ref, dtb_ref, o_ref, *, heads):
    s = ba_ref.shape[1]
    c = GDN_CHUNK
    ii = lax.broadcasted_iota(jnp.int32, (c, c), 0)
    jj = lax.broadcasted_iota(jnp.int32, (c, c), 1)
    tri_f = (ii >= jj).astype(F32)
    tri_b = (ii <= jj).astype(F32)
    lane = lax.broadcasted_iota(jnp.int32, (c, LANES), 1)
    for r0 in range(0, s, c):
        x = ba_ref[0, r0:r0 + c, :]
        beta = _sigmoid(x)
        z = x + dtb_ref[...]
        softplus = jnp.maximum(z, 0.0) + jnp.log1p(jnp.exp(-jnp.abs(z)))
        g = -jnp.exp(alog_ref[...]) * softplus
        gf = _dot(tri_f, g, precision=HIGHEST)
        gb = _dot(tri_b, g, precision=HIGHEST)
        gsum = jnp.where(lane >= 3 * heads, gb, gf)
        o_ref[0, r0:r0 + c, :] = jnp.where(lane < 2 * heads, beta, gsum)


def _gdn_gates(ba, a_log, dt_bias, heads):
    bsz, s, _ = ba.shape
    pad = lambda v: jnp.zeros((1, LANES), F32).at[0, 2 * heads:4 * heads].set(v.reshape(-1).astype(F32))
    return pl.pallas_call(
        functools.partial(_gdn_gates_kernel, heads=heads),
        grid=(bsz,),
        in_specs=[pl.BlockSpec((1, s, LANES), lambda b: (b, 0, 0)),
                  pl.BlockSpec((1, LANES), lambda b: (0, 0)),
                  pl.BlockSpec((1, LANES), lambda b: (0, 0))],
        out_specs=pl.BlockSpec((1, s, LANES), lambda b: (b, 0, 0)),
        out_shape=jax.ShapeDtypeStruct((bsz, s, LANES), F32),
        compiler_params=_cparams(1), name="gdn_gates",
    )(ba, pad(a_log), pad(dt_bias))


def _inv_dot(a, b):
    return _dot(a.astype(BF16), b.astype(BF16))


def _gdn_chunk_kernel(qkv_f, qkv_b, gc_f, gc_b, gr_f, gr_b, s0_ref, of_ref, ob_ref, sout_ref,
                      s_ref, u_ref, wq_ref, kt_ref, qkd_ref, cd_ref, *, heads, nsub):
    t = pl.program_id(1)
    nt = pl.num_programs(1)
    c = GDN_CHUNK
    nb = s0_ref.shape[0]
    nch = 2 * heads

    @pl.when(t == 0)
    def _():
        s_ref[...] = s0_ref[...]

    ii = lax.broadcasted_iota(jnp.int32, (c, 2 * c), 0)
    lane = lax.broadcasted_iota(jnp.int32, (c, 2 * c), 1)
    jj = lane % c
    left = lane < c
    eye_right = (ii == jj) & (lane >= c)
    dirs = ((qkv_f, gc_f, gr_f, ii >= jj, ii > jj, c - 1), (qkv_b, gc_b, gr_b, ii <= jj, ii < jj, 0))

    def local(ci, carry):
        rows = pl.ds(pl.multiple_of(ci * c, c), c)
        chains = []
        for bi in range(nb):
            for di, (qkv, gcr, grr, incl, strict, last) in enumerate(dirs):
                gc = gcr[bi, rows, :]
                gr = grr[bi, ci]
                for h in range(heads):
                    ch = di * heads + h
                    chains.append(dict(
                        bi=bi, ch=ch, incl=incl, strict=strict, last=last,
                        q=qkv[bi, h, rows, :], k=qkv[bi, heads + h, rows, :], v=qkv[bi, 2 * heads + h, rows, :],
                        beta=gc[:, ch:ch + 1],
                        gcol=gc[:, 2 * heads + ch:2 * heads + ch + 1],
                        grow=gr[2 * heads + ch:2 * heads + ch + 1, :]))
        kk = [_dot_nt(a["k"], jnp.concatenate([a["k"], a["k"]], axis=0)) for a in chains]
        qk = [_dot_nt(a["q"], a["k"]) for a in chains]
        dec = [jnp.where(a["incl"], jnp.exp(jnp.where(a["incl"], a["gcol"] - a["grow"], 0.0)), 0.0)
               for a in chains]
        zs = [jnp.where(left, -jnp.where(a["strict"], a["beta"] * kki * dci, 0.0), jnp.where(eye_right, 1.0, 0.0))
              for a, kki, dci in zip(chains, kk, dec)]
        for _ in range(int(math.log2(c))):
            zs = [_inv_dot(z[:, 0:c], z) + jnp.where(left, 0.0, z) for z in zs]
        egs = [jnp.exp(a["gcol"]) for a in chains]
        k32 = [a["k"].astype(F32) for a in chains]
        rhs = [jnp.concatenate([a["v"].astype(F32) * a["beta"], kf * (a["beta"] * eg)], axis=1)
               for a, kf, eg in zip(chains, k32, egs)]
        sols = [_inv_dot(z[:, c:2 * c], r) for z, r in zip(zs, rhs)]
        for a, sol, kf, eg, qki, dci in zip(chains, sols, k32, egs, qk, dec):
            bi, ch = a["bi"], a["ch"]
            glast = a["gcol"][a["last"]:a["last"] + 1, :]
            u_ref[bi, ch, ci] = sol[:, :HEAD_W]
            wq_ref[bi, ch, ci, 0:c, :] = sol[:, HEAD_W:].astype(BF16)
            wq_ref[bi, ch, ci, c:2 * c, :] = (a["q"].astype(F32) * eg).astype(BF16)
            kt_ref[bi, ch, ci] = (kf * jnp.exp(glast - a["gcol"])).astype(BF16)
            qkd_ref[bi, ch, ci] = (qki * dci[:, 0:c]).astype(BF16)
            cd_ref[bi, ch, ci] = jnp.broadcast_to(jnp.exp(glast), (1, LANES))
        return carry

    lax.fori_loop(0, nsub, local, 0)

    def step(si, carry):
        chains = []
        for bi in range(nb):
            for di, o_ref in enumerate((of_ref, ob_ref)):
                ci = si if di == 0 else nsub - 1 - si
                rows = pl.ds(pl.multiple_of(ci * c, c), c)
                chains += [(bi, di * heads + h, h, ci, rows, o_ref) for h in range(heads)]
        ss = [s_ref[bi, ch] for bi, ch, _, _, _, _ in chains]
        ws = [_dot(wq_ref[bi, ch, ci], s.astype(BF16))
              for (bi, ch, _, ci, _, _), s in zip(chains, ss)]
        vn = [(u_ref[bi, ch, ci] - w[0:c]).astype(BF16) for (bi, ch, _, ci, _, _), w in zip(chains, ws)]
        outs = [w[c:2 * c] + _dot(qkd_ref[bi, ch, ci], v) for (bi, ch, _, ci, _, _), w, v in zip(chains, ws, vn)]
        sn = [cd_ref[bi, ch, ci] * s + _dot_tn(kt_ref[bi, ch, ci], v)
              for (bi, ch, _, ci, _, _), s, v in zip(chains, ss, vn)]
        for (bi, ch, h, ci, rows, o_ref), o, s in zip(chains, outs, sn):
            o_ref[bi, h, rows, :] = o.astype(o_ref.dtype)
            s_ref[bi, ch] = s
        return carry

    lax.fori_loop(0, nsub, step, 0)

    @pl.when(t == nt - 1)
    def _():
        sout_ref[...] = s_ref[...]


def _gdn_chunked(qkv_hm, gcol, s0, heads):
    bsz, _, s, _ = qkv_hm.shape
    c = GDN_CHUNK
    nsub = min(GDN_CHUNKS_PER_STEP, s // c)
    tb = nsub * c
    nt = s // tb
    grow = gcol[:, :, :4 * heads].reshape(bsz, s // c, c, 4 * heads).transpose(0, 1, 3, 2)
    grow = jnp.concatenate([grow, grow], axis=-1)
    nb = 2 if bsz % 2 == 0 else 1
    fwd = lambda b, t: (b, 0, t, 0)
    bwd = lambda b, t: (b, 0, nt - 1 - t, 0)
    nch = 2 * heads
    o_sds = jax.ShapeDtypeStruct((bsz, heads, s, HEAD_W), BF16)
    return pl.pallas_call(
        functools.partial(_gdn_chunk_kernel, heads=heads, nsub=nsub),
        grid=(bsz // nb, nt),
        in_specs=[pl.BlockSpec((nb, 3 * heads, tb, HEAD_W), fwd),
                  pl.BlockSpec((nb, 3 * heads, tb, HEAD_W), bwd),
                  pl.BlockSpec((nb, tb, LANES), lambda b, t: (b, t, 0)),
                  pl.BlockSpec((nb, tb, LANES), lambda b, t: (b, nt - 1 - t, 0)),
                  pl.BlockSpec((nb, nsub, 4 * heads, 2 * c), lambda b, t: (b, t, 0, 0)),
                  pl.BlockSpec((nb, nsub, 4 * heads, 2 * c), lambda b, t: (b, nt - 1 - t, 0, 0)),
                  pl.BlockSpec((nb, nch, HEAD_W, HEAD_W), lambda b, t: (b, 0, 0, 0))],
        out_specs=(pl.BlockSpec((nb, heads, tb, HEAD_W), fwd),
                   pl.BlockSpec((nb, heads, tb, HEAD_W), bwd),
                   pl.BlockSpec((nb, nch, HEAD_W, HEAD_W), lambda b, t: (b, 0, 0, 0))),
        out_shape=(o_sds, o_sds, jax.ShapeDtypeStruct(s0.shape, F32)),
        scratch_shapes=[pltpu.VMEM((nb, nch, HEAD_W, HEAD_W), F32),
                        pltpu.VMEM((nb, nch, nsub, c, HEAD_W), F32),
                        pltpu.VMEM((nb, nch, nsub, 2 * c, HEAD_W), BF16),
                        pltpu.VMEM((nb, nch, nsub, c, HEAD_W), BF16),
                        pltpu.VMEM((nb, nch, nsub, c, c), BF16),
                        pltpu.VMEM((nb, nch, nsub, 1, LANES), F32)],
        compiler_params=_cparams(2), name="gdn_chunk",
    )(qkv_hm, qkv_hm, gcol, gcol, grow, grow, s0)


def _route_rows(logits_t, rb):
    s = _sigmoid(logits_t)
    sel = s + rb
    rows = [sel[i:i + 1, :] for i in range(N_EXPERTS)]
    srows = [s[i:i + 1, :] for i in range(N_EXPERTS)]
    gscore = []
    for g in range(N_GROUPS):
        a, b, c, d = rows[GROUP_SIZE * g:GROUP_SIZE * (g + 1)]
        hi_ab, lo_ab = jnp.maximum(a, b), jnp.minimum(a, b)
        hi_cd, lo_cd = jnp.maximum(c, d), jnp.minimum(c, d)
        m1 = jnp.maximum(hi_ab, hi_cd)
        m2 = jnp.maximum(jnp.maximum(lo_ab, lo_cd), jnp.minimum(hi_ab, hi_cd))
        gscore.append(m1 + m2)
    best = gscore[0]
    bg = jnp.zeros_like(best)
    for g in range(1, N_GROUPS):
        upd = gscore[g] > best
        bg = jnp.where(upd, float(g), bg)
        best = jnp.where(upd, gscore[g], best)

    def pick(rws, j):
        out = rws[(N_GROUPS - 1) * GROUP_SIZE + j]
        for g in range(N_GROUPS - 2, -1, -1):
            out = jnp.where(bg == float(g), rws[g * GROUP_SIZE + j], out)
        return out

    v = [pick(rows, j) for j in range(GROUP_SIZE)]
    sv = [pick(srows, j) for j in range(GROUP_SIZE)]
    i1, b1, w1 = jnp.zeros_like(best), v[0], sv[0]
    for j in range(1, GROUP_SIZE):
        upd = v[j] > b1
        i1 = jnp.where(upd, float(j), i1)
        b1 = jnp.where(upd, v[j], b1)
        w1 = jnp.where(upd, sv[j], w1)
    i2 = b2 = w2 = None
    for j in range(GROUP_SIZE):
        ok = i1 != float(j)
        if b2 is None:
            i2 = jnp.where(ok, 0.0, 1.0)
            b2 = jnp.where(ok, v[0], v[1])
            w2 = jnp.where(ok, sv[0], sv[1])
            continue
        upd = ok & (v[j] > b2)
        i2 = jnp.where(upd, float(j), i2)
        b2 = jnp.where(upd, v[j], b2)
        w2 = jnp.where(upd, sv[j], w2)
    tot = w1 + w2
    return bg * GROUP_SIZE + i1, bg * GROUP_SIZE + i2, w1 / tot, w2 / tot


def _slot_ranks(e1, e2, base):
    tm = e1.shape[1]
    eid = lax.broadcasted_iota(jnp.int32, (N_EXPERTS, tm), 0).astype(F32)
    oh1 = jnp.where(eid == e1, 1.0, 0.0)
    oh2 = jnp.where(eid == e2, 1.0, 0.0)
    before = (lax.broadcasted_iota(jnp.int32, (tm, tm), 0) < lax.broadcasted_iota(jnp.int32, (tm, tm), 1))
    before = jnp.where(before, 1.0, 0.0).astype(BF16)
    pre = _dot(jnp.concatenate([oh1, oh2], axis=0).astype(BF16), before)
    n1 = jnp.sum(oh1, axis=1, keepdims=True)
    n2 = jnp.sum(oh2, axis=1, keepdims=True)
    r1 = jnp.sum(oh1 * (pre[0:N_EXPERTS] + base), axis=0, keepdims=True)
    r2 = jnp.sum(oh2 * (pre[N_EXPERTS:] + (base + n1)), axis=0, keepdims=True)
    return r1, r2, base + (n1 + n2)


def _out_proj_kernel(*refs, pro, heads):
    rest = refs[4:] if pro == "gdn" else refs[1:]
    x_ref, mod_ref, w_ref, b_ref, g_ref, rw_ref, rb_ref, cnt0_ref, x1_ref, h2_ref, rt_ref, cnt_ref = rest
    tm = x_ref.shape[1]
    sub = tm // 2 if tm % (2 * LANES) == 0 else tm
    blocks = [slice(r, r + sub) for r in range(0, tm, sub)]

    @pl.when((pl.program_id(0) == 0) & (pl.program_id(1) == 0))
    def _():
        cnt_ref[...] = cnt0_ref[...]

    ys = []
    for blk in blocks:
        if pro == "gdn":
            of_ref, ob_ref, z_ref, gn_ref = refs[:4]
            parts = [_rms(of_ref[0, h, blk, :].astype(F32) + ob_ref[0, h, blk, :].astype(F32), gn_ref[...])
                     for h in range(heads)]
            a = (jnp.concatenate(parts, axis=1) * _silu(z_ref[0, blk, :].astype(F32))).astype(BF16)
        else:
            a = refs[0][0, blk, :]
        ys.append(_dot(a, w_ref[...]) + b_ref[...])
    cnt = cnt_ref[:, 0:1]
    for blk, y in zip(blocks, ys):
        x1 = x_ref[0, blk, :] + mod_ref[0, 2:3, :] * y
        x1_ref[0, blk, :] = x1
        h2 = _norm_mod(x1, g_ref[...], mod_ref[0, 3:4, :], mod_ref[0, 4:5, :]).astype(BF16)
        h2_ref[0, blk, :] = h2
        e1, e2, w1, w2 = _route_rows(_dot_nt(rw_ref[...], h2), rb_ref[...])
        r1, r2, cnt = _slot_ranks(e1, e2, cnt)
        zero = jnp.zeros_like(e1)
        rt_ref[0, :, blk] = jnp.concatenate([e1, e2, w1, w2, r1, r2, zero, zero], axis=0)
    cnt_ref[...] = jnp.broadcast_to(cnt, cnt_ref.shape)


def _out_proj(pro, acts, x, mod, w, bias, norm_g, router_wt, router_b, cnt0, *, tm=512):
    g_, s, d = x.shape
    tm = _row_tile(s, tm)
    heads = d // HEAD_W
    full2 = lambda v: pl.BlockSpec(v.shape, lambda b, t: (0, 0))
    row3 = lambda width: pl.BlockSpec((1, tm, width), lambda b, t: (b, t, 0))
    if pro == "gdn":
        hm = pl.BlockSpec((1, heads, tm, HEAD_W), lambda b, t: (b, 0, t, 0))
        zcol = 3
        act_specs = [hm, hm, pl.BlockSpec((1, tm, d), lambda b, t: (b, t, zcol)), full2(acts[3])]
    else:
        act_specs = [row3(d)]
    in_specs = act_specs + [row3(d), pl.BlockSpec((1, 6, d), lambda b, t: (b, 0, 0)),
                            full2(w), full2(bias), full2(norm_g), full2(router_wt), full2(router_b), full2(cnt0)]
    return pl.pallas_call(
        functools.partial(_out_proj_kernel, pro=pro, heads=heads),
        grid=(g_, s // tm),
        in_specs=in_specs,
        out_specs=(row3(d), row3(d), pl.BlockSpec((1, 8, tm), lambda b, t: (b, 0, t)), full2(cnt0)),
        out_shape=(jax.ShapeDtypeStruct((g_, s, d), F32), jax.ShapeDtypeStruct((g_, s, d), BF16),
                   jax.ShapeDtypeStruct((g_, 8, s), F32), jax.ShapeDtypeStruct(cnt0.shape, F32)),
        compiler_params=_cparams(2), name="out_proj_" + pro,
    )(*acts, x, mod, w, bias, norm_g, router_wt, router_b, cnt0)


def _moe_ffn_kernel(te_ref, nv_ref, x_ref, wg_ref, wu_ref, wd_ref, prev_ref, o_ref):
    i = pl.program_id(0)

    @pl.when(i < nv_ref[0])
    def _():
        x = x_ref[...]
        a = (_silu(_dot(x, wg_ref[0, 0].astype(BF16))) * _dot(x, wu_ref[0, 0].astype(BF16))).astype(BF16)
        o_ref[...] = _dot(a, wd_ref[0, 0].astype(BF16)).astype(o_ref.dtype)

    @pl.when(i >= nv_ref[0])
    def _():
        o_ref[...] = jnp.zeros_like(o_ref)


def _moe_ffn(xs, tile_expert, n_valid, wg, wu, wd, layer, prev, *, tm, tile_off):
    r, d = xs.shape
    f = wg.shape[-1]
    wspec = lambda shape: pl.BlockSpec(shape, lambda i, te, nv: (layer, te[i], 0, 0))
    return pl.pallas_call(
        _moe_ffn_kernel,
        grid_spec=pltpu.PrefetchScalarGridSpec(
            num_scalar_prefetch=2, grid=(r // tm,),
            in_specs=[pl.BlockSpec((tm, d), lambda i, te, nv: (i, 0)),
                      wspec((1, 1, d, f)), wspec((1, 1, d, f)), wspec((1, 1, f, d)),
                      pl.BlockSpec(memory_space=pl.ANY)],
            out_specs=pl.BlockSpec((tm, d), lambda i, te, nv: (i + tile_off, 0))),
        out_shape=jax.ShapeDtypeStruct(prev.shape, prev.dtype),
        input_output_aliases={6: 0},
        compiler_params=_cparams(1), name="moe_ffn",
    )(tile_expert, n_valid, xs, wg, wu, wd, prev)


def _combine_kernel(x_ref, y0_ref, y1_ref, rt_ref, mod_ref, g_ref, o_ref, *, final):
    rt = rt_ref[0]
    y = rt[:, 2:3] * y0_ref[0].astype(F32) + rt[:, 3:4] * y1_ref[0].astype(F32)
    x2 = x_ref[0] + mod_ref[0, 5:6, :] * y
    o_ref[0] = _rms(x2, g_ref[...]) if final else x2


def _combine(x1, y0, y1, route_t, mod, final_g, *, final, tm=512):
    g_, s, d = x1.shape
    tm = _row_tile(s, tm)
    row3 = pl.BlockSpec((1, tm, d), lambda b, t: (b, t, 0))
    return pl.pallas_call(
        functools.partial(_combine_kernel, final=final),
        grid=(g_, s // tm),
        in_specs=[row3, row3, row3,
                  pl.BlockSpec((1, tm, 8), lambda b, t: (b, t, 0)),
                  pl.BlockSpec((1, 6, d), lambda b, t: (b, 0, 0)),
                  pl.BlockSpec(final_g.shape, lambda b, t: (0, 0))],
        out_specs=row3,
        out_shape=jax.ShapeDtypeStruct((g_, s, d), F32),
        compiler_params=_cparams(2), name="moe_combine",
    )(x1, y0, y1, route_t, mod, final_g)


def _invert_slots_kernel(seg_ref, pos_ref, src_ref, *, n, n_slots, n_rows):
    def fill(lo, hi):
        def body(r, carry):
            src_ref[r] = lax.rem(r, n)
            return carry
        lax.fori_loop(lo, hi, body, 0)

    end = 0
    for e in range(N_EXPERTS):
        first, used, reserved = seg_ref[e], seg_ref[N_EXPERTS + e], seg_ref[2 * N_EXPERTS + e]
        fill(first + used, first + reserved)
        end = first + reserved
    fill(end, n_rows)

    rows_per_choice = n // LANES
    for k in range(n_slots // n):
        def row(j, carry, k=k):
            for c in range(LANES):
                src_ref[pos_ref[k * rows_per_choice + j, c]] = j * LANES + c
            return carry
        lax.fori_loop(0, rows_per_choice, row, 0)


def _invert_slots(pos_all, seg, n, n_rows):
    n_slots = pos_all.size
    assert n % LANES == 0 and n_slots % n == 0
    smem = pl.BlockSpec(memory_space=pltpu.SMEM)
    return pl.pallas_call(
        functools.partial(_invert_slots_kernel, n=n, n_slots=n_slots, n_rows=n_rows),
        in_specs=[smem, smem], out_specs=smem,
        out_shape=jax.ShapeDtypeStruct((n_rows,), jnp.int32),
        name="invert_slots",
    )(seg, pos_all.reshape(n_slots // LANES, LANES))


def _moe(h2_list, route_list, counts, wg, wu, wd, layer, *, tm=512):
    d = h2_list[0].shape[-1]
    h2 = jnp.concatenate([h.reshape(-1, d) for h in h2_list], axis=0)
    n = h2.shape[0]
    cnt = counts[:, 0].astype(jnp.int32)
    padded = ((cnt + tm - 1) // tm) * tm
    ends = jnp.cumsum(padded)
    starts = (ends - padded).astype(F32)
    pos_list = []
    for r in route_list:
        e, rank = r[:, 0:2, :], r[:, 4:6, :]
        first = jnp.zeros_like(e)
        for k in range(N_EXPERTS):
            first = jnp.where(e == float(k), starts[k], first)
        pos_list.append((first + rank).astype(jnp.int32))
    pos_all = jnp.concatenate([p.transpose(1, 0, 2).reshape(2, -1) for p in pos_list], axis=1)
    n_tiles = (2 * n) // tm + N_EXPERTS
    seg = jnp.concatenate([(ends - padded), cnt, padded]).astype(jnp.int32)
    src = _invert_slots(pos_all, seg, n, n_tiles * tm)
    n_valid = (ends[-1] // tm).astype(jnp.int32)
    tile_start = jnp.minimum(jnp.arange(n_tiles, dtype=jnp.int32), n_valid - 1) * tm
    tile_expert = jnp.sum((tile_start[:, None] >= ends[None, :]).astype(jnp.int32), axis=1)
    tile_expert = jnp.minimum(tile_expert, N_EXPERTS - 1)
    parts = 2 if n_tiles % 2 == 0 else 1
    pt = n_tiles // parts
    ys = jnp.zeros((n_tiles * tm, d), BF16)
    for pi in range(parts):
        xs = h2.at[src[pi * pt * tm:(pi + 1) * pt * tm]].get(mode="promise_in_bounds")
        nv = jnp.clip(n_valid - pi * pt, 0, pt).reshape(1)
        ys = _moe_ffn(xs, tile_expert[pi * pt:(pi + 1) * pt], nv, wg, wu, wd, layer, ys, tm=tm, tile_off=pi * pt)
    outs = []
    for h, r, p in zip(h2_list, route_list, pos_list):
        y0 = ys.at[p[:, 0, :].reshape(-1)].get(mode="promise_in_bounds").reshape(h.shape)
        y1 = ys.at[p[:, 1, :].reshape(-1)].get(mode="promise_in_bounds").reshape(h.shape)
        outs.append((y0, y1, r.transpose(0, 2, 1)))
    return outs


def _rope_tables(seqlen):
    pos = jnp.arange(seqlen, dtype=jnp.int32)
    row = (pos // GRID_W).astype(F32)
    col = (pos % GRID_W).astype(F32)
    inv_freq = ROPE_THETA ** (-jnp.arange(ROPE_FREQS, dtype=F32) / ROPE_FREQS)
    lane = jnp.arange(LANES)
    within = lane % DA_HEAD_DIM
    axis = within // (2 * ROPE_FREQS)
    half = (within % (2 * ROPE_FREQS)) // ROPE_FREQS
    ang = jnp.where(axis[None, :] == 0, row[:, None], col[:, None]) * inv_freq[within % ROPE_FREQS][None, :]
    sign = jnp.where(half == 0, -1.0, 1.0).astype(F32)
    return jnp.cos(ang), jnp.sin(ang) * sign[None, :]


def kernel(x, c, ctx, c_ctx, ada_w, ada_b, norm_mix_g, norm_ffn_g, final_norm_g, conv_pw1_w, conv_pw1_b, conv_dw_w, conv_dw_b, conv_ln_g, conv_ln_b, conv_pw2_w, conv_pw2_b, diff_w_qkv, diff_lambda, diff_subln_g, diff_w_o, gdn_w_in, gdn_conv_w, gdn_a_log, gdn_dt_bias, gdn_norm_g, gdn_w_o, router_w, router_b, moe_w_gate, moe_w_up, moe_w_down):
    bsz, seqlen, d = x.shape
    n_ctx = ctx.shape[1]
    depth = ada_w.shape[0]
    heads = d // HEAD_W
    row = lambda v: v.reshape(1, -1).astype(F32)

    mods = _ada_all(jnp.concatenate([c, c_ctx[None, :]], axis=0), ada_w, ada_b)
    mods = mods.reshape(depth, bsz + 1, 6, d)
    router_wt = router_w.T.astype(BF16)
    router_bc = router_b.reshape(N_EXPERTS, 1).astype(F32)
    cos_t, sin_t = _rope_tables(seqlen)
    zero_bias = jnp.zeros((1, d), F32)

    def layer(i, lat, cx, mods_i, w):
        gb = lat.shape[0]
        last = i == depth - 1
        kind, j = i % N_MIXERS, i // N_MIXERS
        need_ctx = (not last) or kind != 0
        m_lat, m_ctx = mods_i
        gmix = row(norm_mix_g[i])
        streams = [(lat, m_lat)] + ([(cx, m_ctx)] if need_ctx else [])
        acts = []
        if kind == 0:
            for xs, ms in streams:
                a = _in_proj("glu", xs, ms, gmix, [w["w1"], row(conv_pw1_b[j])])
                shp = a.shape
                a = a.reshape(gb, -1, d)
                a = _dwconv_ln_silu(a, conv_dw_w[j].astype(F32), row(conv_dw_b[j]),
                                    row(conv_ln_g[j]), row(conv_ln_b[j]))
                acts.append(("plain", [a.reshape(shp)]))
        elif kind == 1:
            lam_init = 0.8 - 0.6 * math.exp(-0.3 * i)
            qscale = DA_HEAD_DIM ** -0.5 * math.log2(math.e)
            qkv_l = _in_proj("qkv", lat, m_lat, gmix, [w["wq"]], (cos_t, sin_t), rope=True, qscale=qscale)
            qkv_c = _in_proj("qkv", cx, m_ctx, gmix, [w["wq"]], qscale=qscale).reshape(gb, n_ctx, 3 * d)
            lam_v, sub_g = diff_lambda[j].astype(F32), row(diff_subln_g[j])
            o_l = _diff_attn_core(qkv_l, [qkv_c, qkv_l], lam_v, sub_g, lam_init)
            acts.append(("plain", [o_l]))
            if not last:
                o_c = _diff_attn_core(qkv_c, [qkv_c], lam_v, sub_g, lam_init)
                acts.append(("plain", [o_c.reshape(cx.shape)]))
        else:
            conv_w = gdn_conv_w[j].astype(F32)
            zz_l, ba_l = _in_proj("gdn", lat, m_lat, gmix, [w["w_main"], w["w_ba"]])
            zz_c, ba_c = _in_proj("gdn", cx, m_ctx, gmix, [w["w_main"], w["w_ba"]])
            zz_c = zz_c.reshape(gb, n_ctx, -1)
            ba_c = ba_c.reshape(gb, n_ctx, LANES)
            s0 = jnp.zeros((gb, 2 * heads, HEAD_W, HEAD_W), F32)
            ocf, ocb, s1 = _gdn_chunked(_gdn_prep(zz_c, conv_w),
                                        _gdn_gates(ba_c, gdn_a_log[j], gdn_dt_bias[j], heads), s0, heads)
            olf, olb, _ = _gdn_chunked(_gdn_prep(zz_l, conv_w),
                                       _gdn_gates(ba_l, gdn_a_log[j], gdn_dt_bias[j], heads), s1, heads)
            gn = row(gdn_norm_g[j])
            acts.append(("gdn", [olf, olb, zz_l, gn]))
            if not last:
                acts.append(("gdn", [ocf, ocb, zz_c, gn]))

        gffn = row(norm_ffn_g[i])
        x1s, h2s, routes = [], [], []
        counts = jnp.zeros((N_EXPERTS, LANES), F32)
        for (xs, ms), (pro, a) in zip(streams, acts):
            if pro == "gdn" and xs.shape[0] != a[0].shape[0]:
                xin = xs.reshape(gb, -1, d)
                msb = jnp.broadcast_to(ms, (gb, 6, d))
                x1, h2, rt, counts = _out_proj(pro, a, xin, msb, w["w_out"], w["b_out"], gffn,
                                               router_wt, router_bc, counts)
                x1, h2 = x1.reshape(xs.shape), h2.reshape(xs.shape)
                rt = rt.transpose(1, 0, 2).reshape(1, 8, -1)
            else:
                x1, h2, rt, counts = _out_proj(pro, a, xs, ms, w["w_out"], w["b_out"], gffn,
                                               router_wt, router_bc, counts)
            x1s.append(x1)
            h2s.append(h2)
            routes.append(rt)
        moe_out = _moe(h2s, routes, counts, moe_w_gate, moe_w_up, moe_w_down, i)
        fg = row(final_norm_g)
        lat = _combine(x1s[0], *moe_out[0], m_lat, fg, final=last)
        if not last:
            cx = _combine(x1s[1], *moe_out[1], m_ctx, fg, final=False)
        return lat, cx

    def layer_weights(i):
        kind, j = i % N_MIXERS, i // N_MIXERS
        if kind == 0:
            return dict(w1=conv_pw1_w[j].astype(BF16), w_out=conv_pw2_w[j].astype(BF16), b_out=row(conv_pw2_b[j]))
        if kind == 1:
            return dict(wq=diff_w_qkv[j].astype(BF16), w_out=diff_w_o[j].astype(BF16), b_out=zero_bias)
        cch = gdn_conv_w.shape[-1]
        w_in = gdn_w_in[j]
        w_ba = jnp.zeros((d, LANES), BF16).at[:, :4 * heads].set(w_in[:, cch + d:].astype(BF16))
        return dict(w_main=w_in[:, :cch + d].astype(BF16), w_ba=w_ba,
                    w_out=gdn_w_o[j].astype(BF16), b_out=zero_bias)

    lat = x
    cx = ctx.reshape(1, bsz * n_ctx, d)
    for i in range(depth):
        lat, cx = layer(i, lat, cx, (mods[i, :bsz], mods[i, bsz:]), layer_weights(i))
    return lat
```

```python
import functools
import math

import jax
import jax.numpy as jnp
from jax import lax
from jax.experimental import pallas as pl
from jax.experimental.pallas import tpu as pltpu

F32 = jnp.float32
BF16 = jnp.bfloat16
HIGHEST = lax.Precision.HIGHEST

NORM_EPS = 1e-6
N_MIXERS = 3
GRID_W = 64
ROPE_THETA = 10000.0
DA_HEAD_DIM = 64
ROPE_FREQS = DA_HEAD_DIM // 4
HEAD_W = 128
GDN_CHUNK = 64
GDN_CHUNKS_PER_STEP = 4
N_EXPERTS = 16
N_GROUPS = 4
GROUP_SIZE = N_EXPERTS // N_GROUPS
LANES = 128
SUBLANES = 8
CONV_HALO = 16
VMEM_LIMIT = 56 * 1024 * 1024


def _cparams(n_axes):
    return pltpu.CompilerParams(dimension_semantics=("arbitrary",) * n_axes,
                                vmem_limit_bytes=VMEM_LIMIT)


def _dot(a, b, **kw):
    return jnp.dot(a, b, preferred_element_type=F32, **kw)


def _dot_nt(a, b, **kw):
    return lax.dot_general(a, b, (((1,), (1,)), ((), ())), preferred_element_type=F32, **kw)


def _dot_tn(a, b, **kw):
    return lax.dot_general(a, b, (((0,), (0,)), ((), ())), preferred_element_type=F32, **kw)


def _sigmoid(x):
    return 1.0 / (1.0 + jnp.exp(-x))


def _silu(x):
    return x * _sigmoid(x)


def _rms(x, g):
    return x * lax.rsqrt(jnp.mean(x * x, axis=-1, keepdims=True) + NORM_EPS) * g


def _norm_mod(x, g, shift, scale):
    return _rms(x, g) * (1.0 + scale) + shift


def _row_tile(n, want):
    t = min(n, want)
    assert n % t == 0, (n, t)
    return t


def _ada_kernel(c_ref, w_ref, b_ref, o_ref):
    s = _silu(c_ref[...]).astype(BF16)
    o_ref[0] = _dot(s, w_ref[0].astype(BF16)) + b_ref[0]


def _ada_all(cvec, ada_w, ada_b):
    depth, d, n = ada_w.shape
    r = cvec.shape[0]
    tn = d
    return pl.pallas_call(
        _ada_kernel,
        grid=(depth, n // tn),
        in_specs=[pl.BlockSpec((r, d), lambda i, j: (0, 0)),
                  pl.BlockSpec((1, d, tn), lambda i, j: (i, 0, j)),
                  pl.BlockSpec((1, 1, tn), lambda i, j: (i, 0, j))],
        out_specs=pl.BlockSpec((1, r, tn), lambda i, j: (i, 0, j)),
        out_shape=jax.ShapeDtypeStruct((depth, r, n), F32),
        compiler_params=_cparams(2),
        name="adaln",
    )(cvec, ada_w, ada_b.reshape(depth, 1, n))


def _glu_kernel(x_ref, mod_ref, g_ref, w_ref, b_ref, o_ref, *, tn):
    h = _norm_mod(x_ref[0], g_ref[...], mod_ref[0, 0:1, :], mod_ref[0, 1:2, :]).astype(BF16)
    d = o_ref.shape[-1]
    for j in range(0, d, tn):
        a = _dot(h, w_ref[:, j:j + tn]) + b_ref[:, j:j + tn]
        gt = _dot(h, w_ref[:, d + j:d + j + tn]) + b_ref[:, d + j:d + j + tn]
        o_ref[0, :, j:j + tn] = (a * _sigmoid(gt)).astype(o_ref.dtype)


def _rope_lanes(t, cos, sin_signed, first_half):
    swapped = jnp.where(first_half, pltpu.roll(t, LANES - ROPE_FREQS, 1), pltpu.roll(t, ROPE_FREQS, 1))
    return t * cos + swapped * sin_signed


def _qkv_kernel(*refs, tn, rope, qscale):
    if rope:
        x_ref, mod_ref, g_ref, w_ref, cos_ref, sin_ref, o_ref = refs
    else:
        x_ref, mod_ref, g_ref, w_ref, o_ref = refs
    h = _norm_mod(x_ref[0], g_ref[...], mod_ref[0, 0:1, :], mod_ref[0, 1:2, :]).astype(BF16)
    d = x_ref.shape[-1]
    if rope:
        cos = cos_ref[...]
        sin = sin_ref[...]
        lane = lax.broadcasted_iota(jnp.int32, cos.shape, 1)
        first_half = (lane % (2 * ROPE_FREQS)) < ROPE_FREQS
    for j in range(0, 3 * d, tn):
        y = _dot(h, w_ref[:, j:j + tn])
        if j < 2 * d and rope:
            y = jnp.concatenate(
                [_rope_lanes(y[:, s:s + LANES], cos, sin, first_half) for s in range(0, tn, LANES)], axis=1)
        if j < d:
            y = y * qscale
        o_ref[0, :, j:j + tn] = y.astype(o_ref.dtype)


def _gdn_in_kernel(x_ref, mod_ref, g_ref, w_ref, wba_ref, zz_ref, ba_ref, *, tn):
    h = _norm_mod(x_ref[0], g_ref[...], mod_ref[0, 0:1, :], mod_ref[0, 1:2, :]).astype(BF16)
    for j in range(0, zz_ref.shape[-1], tn):
        zz_ref[0, :, j:j + tn] = _dot(h, w_ref[:, j:j + tn]).astype(zz_ref.dtype)
    ba_ref[0] = _dot(h, wba_ref[...])


def _in_proj(kind, x, mod, norm_g, weights, extra=(), *, tm=512, tn=512, rope=False, qscale=1.0):
    g_, s, d = x.shape
    tm = _row_tile(s, tm)
    grid = (g_, s // tm)
    full2 = lambda a: pl.BlockSpec(a.shape, lambda b, t: (0, 0))
    in_specs = [pl.BlockSpec((1, tm, d), lambda b, t: (b, t, 0)),
                pl.BlockSpec((1, 6, d), lambda b, t: (b, 0, 0)),
                full2(norm_g)] + [full2(w) for w in weights]
    args = [x, mod, norm_g] + list(weights)
    if kind == "glu":
        body = functools.partial(_glu_kernel, tn=tn)
        out_shape = jax.ShapeDtypeStruct((g_, s, d), BF16)
        out_specs = pl.BlockSpec((1, tm, d), lambda b, t: (b, t, 0))
    elif kind == "qkv":
        body = functools.partial(_qkv_kernel, tn=tn, rope=rope, qscale=qscale)
        if rope:
            in_specs += [pl.BlockSpec((tm, LANES), lambda b, t: (t, 0))] * 2
            args += list(extra)
        out_shape = jax.ShapeDtypeStruct((g_, s, 3 * d), BF16)
        out_specs = pl.BlockSpec((1, tm, 3 * d), lambda b, t: (b, t, 0))
    else:
        body = functools.partial(_gdn_in_kernel, tn=tn)
        nz = weights[0].shape[1]
        out_shape = (jax.ShapeDtypeStruct((g_, s, nz), BF16), jax.ShapeDtypeStruct((g_, s, LANES), F32))
        out_specs = (pl.BlockSpec((1, tm, nz), lambda b, t: (b, t, 0)),
                     pl.BlockSpec((1, tm, LANES), lambda b, t: (b, t, 0)))
    return pl.pallas_call(body, grid=grid, in_specs=in_specs, out_specs=out_specs, out_shape=out_shape,
                          compiler_params=_cparams(2), name="in_proj_" + kind)(*args)


def _dwconv_kernel(a_ref, prev_ref, next_ref, w_ref, b_ref, lg_ref, lb_ref, o_ref, buf_ref, acc_ref,
                   *, kw, rows, cols):
    t = pl.program_id(1)
    nt = pl.num_programs(1)
    tm, d = a_ref.shape[1], a_ref.shape[2]
    halo = CONV_HALO
    span = tm + 2 * halo
    buf_ref[0, 0:halo, :] = jnp.where(t > 0, prev_ref[0].astype(F32), 0.0)
    buf_ref[0, halo:halo + tm, :] = a_ref[0].astype(F32)
    buf_ref[0, halo + tm:span, :] = jnp.where(t < nt - 1, next_ref[0].astype(F32), 0.0)
    buf_ref[0, span:, :] = jnp.zeros((SUBLANES, d), F32)
    for s in range(1, SUBLANES):
        buf_ref[s, 0:span, :] = buf_ref[0, s:s + span, :]
    base = halo - kw // 2
    for r0 in range(0, tm, rows):
        for c0 in range(0, d, cols):
            acc = jnp.zeros((rows, cols), F32)
            for k in range(kw):
                off = base + k + r0
                s = off % SUBLANES
                acc = acc + w_ref[k:k + 1, c0:c0 + cols] * buf_ref[s, off - s:off - s + rows, c0:c0 + cols]
            acc_ref[r0:r0 + rows, c0:c0 + cols] = acc
    y = acc_ref[...] + b_ref[...]
    yc = y - jnp.mean(y, axis=-1, keepdims=True)
    yn = yc * lax.rsqrt(jnp.mean(yc * yc, axis=-1, keepdims=True) + NORM_EPS) * lg_ref[...] + lb_ref[...]
    o_ref[0] = _silu(yn).astype(o_ref.dtype)


def _dwconv_ln_silu(a, dw_w, dw_b, ln_g, ln_b, *, tm=256):
    bsz, s, d = a.shape
    kw = dw_w.shape[0]
    assert kw // 2 < CONV_HALO
    tm = _row_tile(s, tm)
    hb = tm // CONV_HALO
    nhb = s // CONV_HALO
    full2 = lambda v: pl.BlockSpec(v.shape, lambda b, t: (0, 0))
    return pl.pallas_call(
        functools.partial(_dwconv_kernel, kw=kw, rows=64, cols=256),
        grid=(bsz, s // tm),
        in_specs=[pl.BlockSpec((1, tm, d), lambda b, t: (b, t, 0)),
                  pl.BlockSpec((1, CONV_HALO, d), lambda b, t: (b, jnp.maximum(t * hb - 1, 0), 0)),
                  pl.BlockSpec((1, CONV_HALO, d), lambda b, t: (b, jnp.minimum((t + 1) * hb, nhb - 1), 0)),
                  full2(dw_w), full2(dw_b), full2(ln_g), full2(ln_b)],
        out_specs=pl.BlockSpec((1, tm, d), lambda b, t: (b, t, 0)),
        out_shape=jax.ShapeDtypeStruct((bsz, s, d), BF16),
        scratch_shapes=[pltpu.VMEM((SUBLANES, tm + 2 * CONV_HALO + SUBLANES, d), F32), pltpu.VMEM((tm, d), F32)],
        compiler_params=_cparams(2), name="dwconv_ln_silu",
    )(a, a, a, dw_w, dw_b, ln_g, ln_b)


def _attn_kernel(*refs, n_kv, lam_init, hps):
    lam_ref, sg_ref, q_ref = refs[:3]
    k_refs = refs[3:3 + n_kv]
    v_refs = refs[3 + n_kv:3 + 2 * n_kv]
    o_ref = refs[3 + 2 * n_kv]
    lv = lam_ref[...]
    lam = (jnp.exp(jnp.sum(lv[0:1] * lv[1:2], axis=-1, keepdims=True))
           - jnp.exp(jnp.sum(lv[2:3] * lv[3:4], axis=-1, keepdims=True)) + lam_init)
    lane = lax.broadcasted_iota(jnp.int32, (q_ref.shape[1], HEAD_W), 1)
    scores = []
    for h in range(hps):
        hl = slice(h * HEAD_W, (h + 1) * HEAD_W)
        q = q_ref[0, :, hl]
        for c in range(2):
            in_map = (lane < DA_HEAD_DIM) if c == 0 else (lane >= DA_HEAD_DIM)
            qc = jnp.where(in_map, q, jnp.zeros_like(q))
            scores.append([_dot_nt(qc, k_ref[0, :, hl]) for k_ref in k_refs])
    for h in range(hps):
        hl = slice(h * HEAD_W, (h + 1) * HEAD_W)
        probs, coef = [], []
        for c in range(2):
            ss = scores[2 * h + c]
            m = ss[0].max(axis=-1, keepdims=True)
            for s_ in ss[1:]:
                m = jnp.maximum(m, s_.max(axis=-1, keepdims=True))
            ps = [jnp.exp2(s_ - m) for s_ in ss]
            l = ps[0].sum(axis=-1, keepdims=True)
            for p in ps[1:]:
                l = l + p.sum(axis=-1, keepdims=True)
            probs.append(ps)
            coef.append((1.0 if c == 0 else lam) / l)
        o = None
        for p0, p1, v_ref in zip(probs[0], probs[1], v_refs):
            pv = _dot((p0 * coef[0] - p1 * coef[1]).astype(BF16), v_ref[0, :, hl])
            o = pv if o is None else o + pv
        o = _rms(o, sg_ref[...]) * (1.0 - lam_init)
        o_ref[0, :, hl] = o.astype(o_ref.dtype)


def _diff_attn_core(q_src, kv_srcs, lam_vecs, subln_g, lam_init, *, tq=512, hps=2):
    bsz, sq, d3 = q_src.shape
    d = d3 // 3
    nh = d // HEAD_W
    assert nh % hps == 0
    nhb = nh // hps
    wb = hps * HEAD_W
    tq = _row_tile(sq, tq)
    n_kv = len(kv_srcs)
    in_specs = [pl.BlockSpec(lam_vecs.shape, lambda b, h, t: (0, 0)),
                pl.BlockSpec(subln_g.shape, lambda b, h, t: (0, 0)),
                pl.BlockSpec((1, tq, wb), lambda b, h, t: (b, t, h))]
    in_specs += [pl.BlockSpec((1, s.shape[1], wb), lambda b, h, t: (b, 0, nhb + h)) for s in kv_srcs]
    in_specs += [pl.BlockSpec((1, s.shape[1], wb), lambda b, h, t: (b, 0, 2 * nhb + h)) for s in kv_srcs]
    return pl.pallas_call(
        functools.partial(_attn_kernel, n_kv=n_kv, lam_init=lam_init, hps=hps),
        grid=(bsz, nhb, sq // tq),
        in_specs=in_specs,
        out_specs=pl.BlockSpec((1, tq, wb), lambda b, h, t: (b, t, h)),
        out_shape=jax.ShapeDtypeStruct((bsz, sq, d), BF16),
        compiler_params=_cparams(3), name="diff_attn",
    )(lam_vecs, subln_g, q_src, *kv_srcs, *kv_srcs)


def _gdn_prep_kernel(zz_ref, w_ref, o_ref, buf_ref, *, kw, heads, qk_scale):
    s = zz_ref.shape[1]
    width = heads * HEAD_W
    pad = SUBLANES
    rows = min(s, 256)
    buf_ref[0:pad, :] = jnp.zeros((pad, width), F32)
    buf_ref[pad:pad + s, :] = zz_ref[0].astype(F32)
    buf_ref[pad + s:, :] = jnp.zeros((pad, width), F32)
    kind = pl.program_id(1)
    for h in range(heads):
        lanes = slice(h * HEAD_W, (h + 1) * HEAD_W)
        for r0 in range(0, s, rows):
            acc = jnp.zeros((rows, HEAD_W), F32)
            for k in range(kw):
                off = pad + k - kw // 2 + r0
                acc = acc + w_ref[k:k + 1, lanes] * buf_ref[off:off + rows, lanes]
            y = _silu(acc)
            nrm = lax.rsqrt(jnp.sum(y * y, axis=-1, keepdims=True) + NORM_EPS)
            fac = jnp.where(kind == 0, nrm * qk_scale, jnp.where(kind == 1, nrm, jnp.ones_like(nrm)))
            o_ref[0, h, r0:r0 + rows, :] = (y * fac).astype(o_ref.dtype)


def _gdn_prep(zz, conv_w):
    bsz, s, _ = zz.shape
    kw, cch = conv_w.shape
    heads = cch // (3 * HEAD_W)
    width = heads * HEAD_W
    return pl.pallas_call(
        functools.partial(_gdn_prep_kernel, kw=kw, heads=heads, qk_scale=HEAD_W ** -0.5),
        grid=(bsz, 3),
        in_specs=[pl.BlockSpec((1, s, width), lambda b, j: (b, 0, j)),
                  pl.BlockSpec((kw, width), lambda b, j: (0, j))],
        out_specs=pl.BlockSpec((1, heads, s, HEAD_W), lambda b, j: (b, j, 0, 0)),
        out_shape=jax.ShapeDtypeStruct((bsz, 3 * heads, s, HEAD_W), BF16),
        scratch_shapes=[pltpu.VMEM((s + 2 * SUBLANES, width), F32)],
        compiler_params=_cparams(2), name="gdn_prep",
    )(zz, conv_w)


def _gdn_gates_kernel(ba_ref, alog_ref, dtb_ref, o_ref, *, heads):
    s = ba_ref.shape[1]
    c = GDN_CHUNK
    ii = lax.broadcasted_iota(jnp.int32, (c, c), 0)
    jj = lax.broadcasted_iota(jnp.int32, (c, c), 1)
    tri_f = (ii >= jj).astype(F32)
    tri_b = (ii <= jj).astype(F32)
    lane = lax.broadcasted_iota(jnp.int32, (c, LANES), 1)
    for r0 in range(0, s, c):
        x = ba_ref[0, r0:r0 + c, :]
        beta = _sigmoid(x)
        z = x + dtb_ref[...]
        softplus = jnp.maximum(z, 0.0) + jnp.log1p(jnp.exp(-jnp.abs(z)))
        g = -jnp.exp(alog_ref[...]) * softplus
        gf = _dot(tri_f, g, precision=HIGHEST)
        gb = _dot(tri_b, g, precision=HIGHEST)
        gsum = jnp.where(lane >= 3 * heads, gb, gf)
        o_ref[0, r0:r0 + c, :] = jnp.where(lane < 2 * heads, beta, gsum)


def _gdn_gates(ba, a_log, dt_bias, heads):
    bsz, s, _ = ba.shape
    pad = lambda v: jnp.zeros((1, LANES), F32).at[0, 2 * heads:4 * heads].set(v.reshape(-1).astype(F32))
    return pl.pallas_call(
        functools.partial(_gdn_gates_kernel, heads=heads),
        grid=(bsz,),
        in_specs=[pl.BlockSpec((1, s, LANES), lambda b: (b, 0, 0)),
                  pl.BlockSpec((1, LANES), lambda b: (0, 0)),
                  pl.BlockSpec((1, LANES), lambda b: (0, 0))],
        out_specs=pl.BlockSpec((1, s, LANES), lambda b: (b, 0, 0)),
        out_shape=jax.ShapeDtypeStruct((bsz, s, LANES), F32),
        compiler_params=_cparams(1), name="gdn_gates",
    )(ba, pad(a_log), pad(dt_bias))


def _inv_dot(a, b):
    return _dot(a.astype(BF16), b.astype(BF16))


def _gdn_chunk_kernel(qkv_f, qkv_b, gc_f, gc_b, gr_f, gr_b, s0_ref, of_ref, ob_ref, sout_ref,
                      s_ref, u_ref, wq_ref, kt_ref, qkd_ref, cd_ref, *, heads, nsub):
    t = pl.program_id(1)
    nt = pl.num_programs(1)
    c = GDN_CHUNK
    nb = s0_ref.shape[0]
    nch = 2 * heads

    @pl.when(t == 0)
    def _():
        s_ref[...] = s0_ref[...]

    ii = lax.broadcasted_iota(jnp.int32, (c, 2 * c), 0)
    lane = lax.broadcasted_iota(jnp.int32, (c, 2 * c), 1)
    jj = lane % c
    left = lane < c
    eye_right = (ii == jj) & (lane >= c)
    dirs = ((qkv_f, gc_f, gr_f, ii >= jj, ii > jj, c - 1), (qkv_b, gc_b, gr_b, ii <= jj, ii < jj, 0))

    def local(ci, carry):
        rows = pl.ds(pl.multiple_of(ci * c, c), c)
        chains = []
        for bi in range(nb):
            for di, (qkv, gcr, grr, incl, strict, last) in enumerate(dirs):
                gc = gcr[bi, rows, :]
                gr = grr[bi, ci]
                for h in range(heads):
                    ch = di * heads + h
                    chains.append(dict(
                        bi=bi, ch=ch, incl=incl, strict=strict, last=last,
                        q=qkv[bi, h, rows, :], k=qkv[bi, heads + h, rows, :], v=qkv[bi, 2 * heads + h, rows, :],
                        beta=gc[:, ch:ch + 1],
                        gcol=gc[:, 2 * heads + ch:2 * heads + ch + 1],
                        grow=gr[2 * heads + ch:2 * heads + ch + 1, :]))
        kk = [_dot_nt(a["k"], jnp.concatenate([a["k"], a["k"]], axis=0)) for a in chains]
        qk = [_dot_nt(a["q"], a["k"]) for a in chains]
        dec = [jnp.where(a["incl"], jnp.exp(jnp.where(a["incl"], a["gcol"] - a["grow"], 0.0)), 0.0)
               for a in chains]
        zs = [jnp.where(left, -jnp.where(a["strict"], a["beta"] * kki * dci, 0.0), jnp.where(eye_right, 1.0, 0.0))
              for a, kki, dci in zip(chains, kk, dec)]
        for _ in range(int(math.log2(c))):
            zs = [_inv_dot(z[:, 0:c], z) + jnp.where(left, 0.0, z) for z in zs]
        egs = [jnp.exp(a["gcol"]) for a in chains]
        k32 = [a["k"].astype(F32) for a in chains]
        rhs = [jnp.concatenate([a["v"].astype(F32) * a["beta"], kf * (a["beta"] * eg)], axis=1)
               for a, kf, eg in zip(chains, k32, egs)]
        sols = [_inv_dot(z[:, c:2 * c], r) for z, r in zip(zs, rhs)]
        for a, sol, kf, eg, qki, dci in zip(chains, sols, k32, egs, qk, dec):
            bi, ch = a["bi"], a["ch"]
            glast = a["gcol"][a["last"]:a["last"] + 1, :]
            u_ref[bi, ch, ci] = sol[:, :HEAD_W]
            wq_ref[bi, ch, ci, 0:c, :] = sol[:, HEAD_W:].astype(BF16)
            wq_ref[bi, ch, ci, c:2 * c, :] = (a["q"].astype(F32) * eg).astype(BF16)
            kt_ref[bi, ch, ci] = (kf * jnp.exp(glast - a["gcol"])).astype(BF16)
            qkd_ref[bi, ch, ci] = (qki * dci[:, 0:c]).astype(BF16)
            cd_ref[bi, ch, ci] = jnp.broadcast_to(jnp.exp(glast), (1, LANES))
        return carry

    lax.fori_loop(0, nsub, local, 0)

    def step(si, carry):
        chains = []
        for bi in range(nb):
            for di, o_ref in enumerate((of_ref, ob_ref)):
                ci = si if di == 0 else nsub - 1 - si
                rows = pl.ds(pl.multiple_of(ci * c, c), c)
                chains += [(bi, di * heads + h, h, ci, rows, o_ref) for h in range(heads)]
        ss = [s_ref[bi, ch] for bi, ch, _, _, _, _ in chains]
        ws = [_dot(wq_ref[bi, ch, ci], s.astype(BF16))
              for (bi, ch, _, ci, _, _), s in zip(chains, ss)]
        vn = [(u_ref[bi, ch, ci] - w[0:c]).astype(BF16) for (bi, ch, _, ci, _, _), w in zip(chains, ws)]
        outs = [w[c:2 * c] + _dot(qkd_ref[bi, ch, ci], v) for (bi, ch, _, ci, _, _), w, v in zip(chains, ws, vn)]
        sn = [cd_ref[bi, ch, ci] * s + _dot_tn(kt_ref[bi, ch, ci], v)
              for (bi, ch, _, ci, _, _), s, v in zip(chains, ss, vn)]
        for (bi, ch, h, ci, rows, o_ref), o, s in zip(chains, outs, sn):
            o_ref[bi, h, rows, :] = o.astype(o_ref.dtype)
            s_ref[bi, ch] = s
        return carry

    lax.fori_loop(0, nsub, step, 0)

    @pl.when(t == nt - 1)
    def _():
        sout_ref[...] = s_ref[...]


def _gdn_chunked(qkv_hm, gcol, s0, heads):
    bsz, _, s, _ = qkv_hm.shape
    c = GDN_CHUNK
    nsub = min(GDN_CHUNKS_PER_STEP, s // c)
    tb = nsub * c
    nt = s // tb
    grow = gcol[:, :, :4 * heads].reshape(bsz, s // c, c, 4 * heads).transpose(0, 1, 3, 2)
    grow = jnp.concatenate([grow, grow], axis=-1)
    nb = 2 if bsz % 2 == 0 else 1
    fwd = lambda b, t: (b, 0, t, 0)
    bwd = lambda b, t: (b, 0, nt - 1 - t, 0)
    nch = 2 * heads
    o_sds = jax.ShapeDtypeStruct((bsz, heads, s, HEAD_W), BF16)
    return pl.pallas_call(
        functools.partial(_gdn_chunk_kernel, heads=heads, nsub=nsub),
        grid=(bsz // nb, nt),
        in_specs=[pl.BlockSpec((nb, 3 * heads, tb, HEAD_W), fwd),
                  pl.BlockSpec((nb, 3 * heads, tb, HEAD_W), bwd),
                  pl.BlockSpec((nb, tb, LANES), lambda b, t: (b, t, 0)),
                  pl.BlockSpec((nb, tb, LANES), lambda b, t: (b, nt - 1 - t, 0)),
                  pl.BlockSpec((nb, nsub, 4 * heads, 2 * c), lambda b, t: (b, t, 0, 0)),
                  pl.BlockSpec((nb, nsub, 4 * heads, 2 * c), lambda b, t: (b, nt - 1 - t, 0, 0)),
                  pl.BlockSpec((nb, nch, HEAD_W, HEAD_W), lambda b, t: (b, 0, 0, 0))],
        out_specs=(pl.BlockSpec((nb, heads, tb, HEAD_W), fwd),
                   pl.BlockSpec((nb, heads, tb, HEAD_W), bwd),
                   pl.BlockSpec((nb, nch, HEAD_W, HEAD_W), lambda b, t: (b, 0, 0, 0))),
        out_shape=(o_sds, o_sds, jax.ShapeDtypeStruct(s0.shape, F32)),
        scratch_shapes=[pltpu.VMEM((nb, nch, HEAD_W, HEAD_W), F32),
                        pltpu.VMEM((nb, nch, nsub, c, HEAD_W), F32),
                        pltpu.VMEM((nb, nch, nsub, 2 * c, HEAD_W), BF16),
                        pltpu.VMEM((nb, nch, nsub, c, HEAD_W), BF16),
                        pltpu.VMEM((nb, nch, nsub, c, c), BF16),
                        pltpu.VMEM((nb, nch, nsub, 1, LANES), F32)],
        compiler_params=_cparams(2), name="gdn_chunk",
    )(qkv_hm, qkv_hm, gcol, gcol, grow, grow, s0)


def _route_rows(logits_t, rb):
    s = _sigmoid(logits_t)
    sel = s + rb
    rows = [sel[i:i + 1, :] for i in range(N_EXPERTS)]
    srows = [s[i:i + 1, :] for i in range(N_EXPERTS)]
    gscore = []
    for g in range(N_GROUPS):
        a, b, c, d = rows[GROUP_SIZE * g:GROUP_SIZE * (g + 1)]
        hi_ab, lo_ab = jnp.maximum(a, b), jnp.minimum(a, b)
        hi_cd, lo_cd = jnp.maximum(c, d), jnp.minimum(c, d)
        m1 = jnp.maximum(hi_ab, hi_cd)
        m2 = jnp.maximum(jnp.maximum(lo_ab, lo_cd), jnp.minimum(hi_ab, hi_cd))
        gscore.append(m1 + m2)
    best = gscore[0]
    bg = jnp.zeros_like(best)
    for g in range(1, N_GROUPS):
        upd = gscore[g] > best
        bg = jnp.where(upd, float(g), bg)
        best = jnp.where(upd, gscore[g], best)

    def pick(rws, j):
        out = rws[(N_GROUPS - 1) * GROUP_SIZE + j]
        for g in range(N_GROUPS - 2, -1, -1):
            out = jnp.where(bg == float(g), rws[g * GROUP_SIZE + j], out)
        return out

    v = [pick(rows, j) for j in range(GROUP_SIZE)]
    sv = [pick(srows, j) for j in range(GROUP_SIZE)]
    i1, b1, w1 = jnp.zeros_like(best), v[0], sv[0]
    for j in range(1, GROUP_SIZE):
        upd = v[j] > b1
        i1 = jnp.where(upd, float(j), i1)
        b1 = jnp.where(upd, v[j], b1)
        w1 = jnp.where(upd, sv[j], w1)
    i2 = b2 = w2 = None
    for j in range(GROUP_SIZE):
        ok = i1 != float(j)
        if b2 is None:
            i2 = jnp.where(ok, 0.0, 1.0)
            b2 = jnp.where(ok, v[0], v[1])
            w2 = jnp.where(ok, sv[0], sv[1])
            continue
        upd = ok & (v[j] > b2)
        i2 = jnp.where(upd, float(j), i2)
        b2 = jnp.where(upd, v[j], b2)
        w2 = jnp.where(upd, sv[j], w2)
    tot = w1 + w2
    return bg * GROUP_SIZE + i1, bg * GROUP_SIZE + i2, w1 / tot, w2 / tot


def _slot_ranks(e1, e2, base1, base2):
    tm = e1.shape[1]
    eid = lax.broadcasted_iota(jnp.int32, (N_EXPERTS, tm), 0).astype(F32)
    oh1 = jnp.where(eid == e1, 1.0, 0.0)
    oh2 = jnp.where(eid == e2, 1.0, 0.0)
    before = (lax.broadcasted_iota(jnp.int32, (tm, tm), 0) < lax.broadcasted_iota(jnp.int32, (tm, tm), 1))
    before = jnp.where(before, 1.0, 0.0).astype(BF16)
    pre = _dot(jnp.concatenate([oh1, oh2], axis=0).astype(BF16), before)
    n1 = jnp.sum(oh1, axis=1, keepdims=True)
    n2 = jnp.sum(oh2, axis=1, keepdims=True)
    r1 = jnp.sum(oh1 * (pre[0:N_EXPERTS] + base1), axis=0, keepdims=True)
    r2 = jnp.sum(oh2 * (pre[N_EXPERTS:] + base2), axis=0, keepdims=True)
    return r1, r2, base1 + n1, base2 + n2


def _out_proj_kernel(*refs, pro, heads):
    rest = refs[4:] if pro == "gdn" else refs[1:]
    x_ref, mod_ref, w_ref, b_ref, g_ref, rw_ref, rb_ref, cnt0_ref, x1_ref, h2_ref, rt_ref, cnt_ref = rest
    tm = x_ref.shape[1]
    sub = tm // 2 if tm % (2 * LANES) == 0 else tm
    blocks = [slice(r, r + sub) for r in range(0, tm, sub)]

    @pl.when((pl.program_id(0) == 0) & (pl.program_id(1) == 0))
    def _():
        cnt_ref[...] = cnt0_ref[...]

    ys = []
    for blk in blocks:
        if pro == "gdn":
            of_ref, ob_ref, z_ref, gn_ref = refs[:4]
            parts = [_rms(of_ref[0, h, blk, :].astype(F32) + ob_ref[0, h, blk, :].astype(F32), gn_ref[...])
                     for h in range(heads)]
            a = (jnp.concatenate(parts, axis=1) * _silu(z_ref[0, blk, :].astype(F32))).astype(BF16)
        else:
            a = refs[0][0, blk, :]
        ys.append(_dot(a, w_ref[...]) + b_ref[...])
    cnt1, cnt2 = cnt_ref[:, 0:1], cnt_ref[:, 1:2]
    for blk, y in zip(blocks, ys):
        x1 = x_ref[0, blk, :] + mod_ref[0, 2:3, :] * y
        x1_ref[0, blk, :] = x1
        h2 = _norm_mod(x1, g_ref[...], mod_ref[0, 3:4, :], mod_ref[0, 4:5, :]).astype(BF16)
        h2_ref[0, blk, :] = h2
        e1, e2, w1, w2 = _route_rows(_dot_nt(rw_ref[...], h2), rb_ref[...])
        r1, r2, cnt1, cnt2 = _slot_ranks(e1, e2, cnt1, cnt2)
        zero = jnp.zeros_like(e1)
        rt_ref[0, :, blk] = jnp.concatenate([e1, e2, w1, w2, r1, r2, zero, zero], axis=0)
    lane = lax.broadcasted_iota(jnp.int32, cnt_ref.shape, 1)
    cnt_ref[...] = jnp.where(lane == 0, cnt1, jnp.where(lane == 1, cnt2, 0.0))


def _out_proj(pro, acts, x, mod, w, bias, norm_g, router_wt, router_b, cnt0, *, tm=512):
    g_, s, d = x.shape
    tm = _row_tile(s, tm)
    heads = d // HEAD_W
    full2 = lambda v: pl.BlockSpec(v.shape, lambda b, t: (0, 0))
    row3 = lambda width: pl.BlockSpec((1, tm, width), lambda b, t: (b, t, 0))
    if pro == "gdn":
        hm = pl.BlockSpec((1, heads, tm, HEAD_W), lambda b, t: (b, 0, t, 0))
        zcol = 3
        act_specs = [hm, hm, pl.BlockSpec((1, tm, d), lambda b, t: (b, t, zcol)), full2(acts[3])]
    else:
        act_specs = [row3(d)]
    in_specs = act_specs + [row3(d), pl.BlockSpec((1, 6, d), lambda b, t: (b, 0, 0)),
                            full2(w), full2(bias), full2(norm_g), full2(router_wt), full2(router_b), full2(cnt0)]
    return pl.pallas_call(
        functools.partial(_out_proj_kernel, pro=pro, heads=heads),
        grid=(g_, s // tm),
        in_specs=in_specs,
        out_specs=(row3(d), row3(d), pl.BlockSpec((1, 8, tm), lambda b, t: (b, 0, t)), full2(cnt0)),
        out_shape=(jax.ShapeDtypeStruct((g_, s, d), F32), jax.ShapeDtypeStruct((g_, s, d), BF16),
                   jax.ShapeDtypeStruct((g_, 8, s), F32), jax.ShapeDtypeStruct(cnt0.shape, F32)),
        compiler_params=_cparams(2), name="out_proj_" + pro,
    )(*acts, x, mod, w, bias, norm_g, router_wt, router_b, cnt0)


def _moe_ffn_kernel(te_ref, nv_ref, x_ref, wg_ref, wu_ref, wd_ref, *rest):
    o_ref = rest[-1]
    i = pl.program_id(0)

    @pl.when(i < nv_ref[0])
    def _():
        x = x_ref[...]
        a = (_silu(_dot(x, wg_ref[0, 0].astype(BF16))) * _dot(x, wu_ref[0, 0].astype(BF16))).astype(BF16)
        o_ref[...] = _dot(a, wd_ref[0, 0].astype(BF16)).astype(o_ref.dtype)

    @pl.when(i >= nv_ref[0])
    def _():
        o_ref[...] = jnp.zeros_like(o_ref)


def _moe_ffn(xs, tile_expert, n_valid, wg, wu, wd, layer, *, tm, n_tiles, tile_off=0, prev=None):
    r, d = xs.shape
    f = wg.shape[-1]
    xs_tiles = r // tm
    steps = n_tiles - tile_off if prev is None else xs_tiles
    last = xs_tiles - 1
    wspec = lambda shape: pl.BlockSpec(shape, lambda i, te, nv: (layer, te[jnp.minimum(i, last)], 0, 0))
    in_specs = [pl.BlockSpec((tm, d), lambda i, te, nv: (jnp.minimum(i, last), 0)),
                wspec((1, 1, d, f)), wspec((1, 1, d, f)), wspec((1, 1, f, d))]
    args = [tile_expert, n_valid, xs, wg, wu, wd]
    aliases = {}
    if prev is not None:
        in_specs.append(pl.BlockSpec(memory_space=pl.ANY))
        args.append(prev)
        aliases = {len(args) - 1: 0}
    return pl.pallas_call(
        _moe_ffn_kernel,
        grid_spec=pltpu.PrefetchScalarGridSpec(
            num_scalar_prefetch=2, grid=(steps,), in_specs=in_specs,
            out_specs=pl.BlockSpec((tm, d), lambda i, te, nv: (i + tile_off, 0))),
        out_shape=jax.ShapeDtypeStruct((n_tiles * tm, d), BF16),
        input_output_aliases=aliases,
        compiler_params=_cparams(1), name="moe_ffn",
    )(*args)


def _combine_kernel(x_ref, y0_ref, y1_ref, rt_ref, mod_ref, g_ref, o_ref, *, final):
    rt = rt_ref[0]
    y = rt[:, 2:3] * y0_ref[0].astype(F32) + rt[:, 3:4] * y1_ref[0].astype(F32)
    x2 = x_ref[0] + mod_ref[0, 5:6, :] * y
    o_ref[0] = _rms(x2, g_ref[...]) if final else x2


def _combine(x1, y0, y1, route_t, mod, final_g, *, final, tm=512):
    g_, s, d = x1.shape
    tm = _row_tile(s, tm)
    row3 = pl.BlockSpec((1, tm, d), lambda b, t: (b, t, 0))
    return pl.pallas_call(
        functools.partial(_combine_kernel, final=final),
        grid=(g_, s // tm),
        in_specs=[row3, row3, row3,
                  pl.BlockSpec((1, tm, 8), lambda b, t: (b, t, 0)),
                  pl.BlockSpec((1, 6, d), lambda b, t: (b, 0, 0)),
                  pl.BlockSpec(final_g.shape, lambda b, t: (0, 0))],
        out_specs=row3,
        out_shape=jax.ShapeDtypeStruct((g_, s, d), F32),
        compiler_params=_cparams(2), name="moe_combine",
    )(x1, y0, y1, route_t, mod, final_g)


def _invert_slots_kernel(seg_ref, pos_ref, src_ref, *, n, n_slots, n_rows):
    def fill(lo, hi):
        def body(r, carry):
            src_ref[r] = lax.rem(r, n)
            return carry
        lax.fori_loop(lo, hi, body, 0)

    end = 0
    for e in range(N_EXPERTS):
        first, used, reserved = seg_ref[e], seg_ref[N_EXPERTS + e], seg_ref[2 * N_EXPERTS + e]
        fill(first + used, first + reserved)
        end = first + reserved
    fill(end, n_rows)

    rows_per_choice = n // LANES
    for k in range(n_slots // n):
        def row(j, carry, k=k):
            for c in range(LANES):
                src_ref[pos_ref[k * rows_per_choice + j, c]] = j * LANES + c
            return carry
        lax.fori_loop(0, rows_per_choice, row, 0)


def _invert_slots(pos_all, seg, n, n_rows):
    n_slots = pos_all.size
    assert n % LANES == 0 and n_slots % n == 0
    smem = pl.BlockSpec(memory_space=pltpu.SMEM)
    return pl.pallas_call(
        functools.partial(_invert_slots_kernel, n=n, n_slots=n_slots, n_rows=n_rows),
        in_specs=[smem, smem], out_specs=smem,
        out_shape=jax.ShapeDtypeStruct((n_rows,), jnp.int32),
        name="invert_slots",
    )(seg, pos_all.reshape(n_slots // LANES, LANES))


def _moe(h2_list, route_list, counts, wg, wu, wd, layer, *, tm=512):
    d = h2_list[0].shape[-1]
    h2 = jnp.concatenate([h.reshape(-1, d) for h in h2_list], axis=0)
    n = h2.shape[0]
    n_tiles = n // tm + N_EXPERTS
    ys_list, pos_lists = [], []
    for c in range(2):
        cnt = counts[:, c].astype(jnp.int32)
        padded = ((cnt + tm - 1) // tm) * tm
        ends = jnp.cumsum(padded)
        starts = (ends - padded).astype(F32)
        pos_list = []
        for r in route_list:
            e, rank = r[:, c, :], r[:, 4 + c, :]
            first = jnp.zeros_like(e)
            for k in range(N_EXPERTS):
                first = jnp.where(e == float(k), starts[k], first)
            pos_list.append((first + rank).astype(jnp.int32))
        pos_all = jnp.concatenate([p.reshape(1, -1) for p in pos_list], axis=1)
        seg = jnp.concatenate([(ends - padded), cnt, padded]).astype(jnp.int32)
        src = _invert_slots(pos_all, seg, n, n_tiles * tm)
        n_valid = (ends[-1] // tm).astype(jnp.int32)
        tile_start = jnp.minimum(jnp.arange(n_tiles, dtype=jnp.int32), n_valid - 1) * tm
        tile_expert = jnp.sum((tile_start[:, None] >= ends[None, :]).astype(jnp.int32), axis=1)
        tile_expert = jnp.minimum(tile_expert, N_EXPERTS - 1)
        xs = h2.at[src].get(mode="promise_in_bounds")
        ys_list.append(_moe_ffn(xs, tile_expert, n_valid.reshape(1), wg, wu, wd, layer, tm=tm, n_tiles=n_tiles))
        pos_lists.append(pos_list)
    outs = []
    for si, (h, r) in enumerate(zip(h2_list, route_list)):
        y0, y1 = (ys_list[c].at[pos_lists[c][si].reshape(-1)].get(mode="promise_in_bounds").reshape(h.shape)
                  for c in range(2))
        outs.append((y0, y1, r.transpose(0, 2, 1)))
    return outs


def _rope_tables(seqlen):
    pos = jnp.arange(seqlen, dtype=jnp.int32)
    row = (pos // GRID_W).astype(F32)
    col = (pos % GRID_W).astype(F32)
    inv_freq = ROPE_THETA ** (-jnp.arange(ROPE_FREQS, dtype=F32) / ROPE_FREQS)
    lane = jnp.arange(LANES)
    within = lane % DA_HEAD_DIM
    axis = within // (2 * ROPE_FREQS)
    half = (within % (2 * ROPE_FREQS)) // ROPE_FREQS
    ang = jnp.where(axis[None, :] == 0, row[:, None], col[:, None]) * inv_freq[within % ROPE_FREQS][None, :]
    sign = jnp.where(half == 0, -1.0, 1.0).astype(F32)
    return jnp.cos(ang), jnp.sin(ang) * sign[None, :]


def kernel(x, c, ctx, c_ctx, ada_w, ada_b, norm_mix_g, norm_ffn_g, final_norm_g, conv_pw1_w, conv_pw1_b, conv_dw_w, conv_dw_b, conv_ln_g, conv_ln_b, conv_pw2_w, conv_pw2_b, diff_w_qkv, diff_lambda, diff_subln_g, diff_w_o, gdn_w_in, gdn_conv_w, gdn_a_log, gdn_dt_bias, gdn_norm_g, gdn_w_o, router_w, router_b, moe_w_gate, moe_w_up, moe_w_down):
    bsz, seqlen, d = x.shape
    n_ctx = ctx.shape[1]
    depth = ada_w.shape[0]
    heads = d // HEAD_W
    row = lambda v: v.reshape(1, -1).astype(F32)

    mods = _ada_all(jnp.concatenate([c, c_ctx[None, :]], axis=0), ada_w, ada_b)
    mods = mods.reshape(depth, bsz + 1, 6, d)
    router_wt = router_w.T.astype(BF16)
    router_bc = router_b.reshape(N_EXPERTS, 1).astype(F32)
    cos_t, sin_t = _rope_tables(seqlen)
    zero_bias = jnp.zeros((1, d), F32)

    def layer(i, lat, cx, mods_i, w):
        gb = lat.shape[0]
        last = i == depth - 1
        kind, j = i % N_MIXERS, i // N_MIXERS
        need_ctx = (not last) or kind != 0
        m_lat, m_ctx = mods_i
        gmix = row(norm_mix_g[i])
        streams = [(lat, m_lat)] + ([(cx, m_ctx)] if need_ctx else [])
        acts = []
        if kind == 0:
            for xs, ms in streams:
                a = _in_proj("glu", xs, ms, gmix, [w["w1"], row(conv_pw1_b[j])])
                shp = a.shape
                a = a.reshape(gb, -1, d)
                a = _dwconv_ln_silu(a, conv_dw_w[j].astype(F32), row(conv_dw_b[j]),
                                    row(conv_ln_g[j]), row(conv_ln_b[j]))
                acts.append(("plain", [a.reshape(shp)]))
        elif kind == 1:
            lam_init = 0.8 - 0.6 * math.exp(-0.3 * i)
            qscale = DA_HEAD_DIM ** -0.5 * math.log2(math.e)
            qkv_l = _in_proj("qkv", lat, m_lat, gmix, [w["wq"]], (cos_t, sin_t), rope=True, qscale=qscale)
            qkv_c = _in_proj("qkv", cx, m_ctx, gmix, [w["wq"]], qscale=qscale).reshape(gb, n_ctx, 3 * d)
            lam_v, sub_g = diff_lambda[j].astype(F32), row(diff_subln_g[j])
            o_l = _diff_attn_core(qkv_l, [qkv_c, qkv_l], lam_v, sub_g, lam_init)
            acts.append(("plain", [o_l]))
            if not last:
                o_c = _diff_attn_core(qkv_c, [qkv_c], lam_v, sub_g, lam_init)
                acts.append(("plain", [o_c.reshape(cx.shape)]))
        else:
            conv_w = gdn_conv_w[j].astype(F32)
            zz_l, ba_l = _in_proj("gdn", lat, m_lat, gmix, [w["w_main"], w["w_ba"]])
            zz_c, ba_c = _in_proj("gdn", cx, m_ctx, gmix, [w["w_main"], w["w_ba"]])
            zz_c = zz_c.reshape(gb, n_ctx, -1)
            ba_c = ba_c.reshape(gb, n_ctx, LANES)
            s0 = jnp.zeros((gb, 2 * heads, HEAD_W, HEAD_W), F32)
            ocf, ocb, s1 = _gdn_chunked(_gdn_prep(zz_c, conv_w),
                                        _gdn_gates(ba_c, gdn_a_log[j], gdn_dt_bias[j], heads), s0, heads)
            olf, olb, _ = _gdn_chunked(_gdn_prep(zz_l, conv_w),
                                       _gdn_gates(ba_l, gdn_a_log[j], gdn_dt_bias[j], heads), s1, heads)
            gn = row(gdn_norm_g[j])
            acts.append(("gdn", [olf, olb, zz_l, gn]))
            if not last:
                acts.append(("gdn", [ocf, ocb, zz_c, gn]))

        gffn = row(norm_ffn_g[i])
        x1s, h2s, routes = [], [], []
        counts = jnp.zeros((N_EXPERTS, LANES), F32)
        for (xs, ms), (pro, a) in zip(streams, acts):
            if pro == "gdn" and xs.shape[0] != a[0].shape[0]:
                xin = xs.reshape(gb, -1, d)
                msb = jnp.broadcast_to(ms, (gb, 6, d))
                x1, h2, rt, counts = _out_proj(pro, a, xin, msb, w["w_out"], w["b_out"], gffn,
                                               router_wt, router_bc, counts)
                x1, h2 = x1.reshape(xs.shape), h2.reshape(xs.shape)
                rt = rt.transpose(1, 0, 2).reshape(1, 8, -1)
            else:
                x1, h2, rt, counts = _out_proj(pro, a, xs, ms, w["w_out"], w["b_out"], gffn,
                                               router_wt, router_bc, counts)
            x1s.append(x1)
            h2s.append(h2)
            routes.append(rt)
        moe_out = _moe(h2s, routes, counts, moe_w_gate, moe_w_up, moe_w_down, i)
        fg = row(final_norm_g)
        lat = _combine(x1s[0], *moe_out[0], m_lat, fg, final=last)
        if not last:
            cx = _combine(x1s[1], *moe_out[1], m_ctx, fg, final=False)
        return lat, cx

    def layer_weights(i):
        kind, j = i % N_MIXERS, i // N_MIXERS
        if kind == 0:
            return dict(w1=conv_pw1_w[j].astype(BF16), w_out=conv_pw2_w[j].astype(BF16), b_out=row(conv_pw2_b[j]))
        if kind == 1:
            return dict(wq=diff_w_qkv[j].astype(BF16), w_out=diff_w_o[j].astype(BF16), b_out=zero_bias)
        cch = gdn_conv_w.shape[-1]
        w_in = gdn_w_in[j]
        w_ba = jnp.zeros((d, LANES), BF16).at[:, :4 * heads].set(w_in[:, cch + d:].astype(BF16))
        return dict(w_main=w_in[:, :cch + d].astype(BF16), w_ba=w_ba,
                    w_out=gdn_w_o[j].astype(BF16), b_out=zero_bias)

    lat = x
    cx = ctx.reshape(1, bsz * n_ctx, d)
    for i in range(depth):
        lat, cx = layer(i, lat, cx, (mods[i, :bsz], mods[i, bsz:]), layer_weights(i))
    return lat
```
